```python
import math
import jax, jax.numpy as jnp
from jax import lax
import numpy as np

D_MODEL = 4096
BATCH = 4
SEQ = 2048
DEPTH = 1
DEC_BATCH = 128
DEC_SEQ = 1
PAST_LEN = 8192
PAGE_SIZE = 128

N_HEADS = 32
N_KV_HEADS = 8
HEAD_DIM = 64
GROUP = N_HEADS // N_KV_HEADS
ATTN_W = N_HEADS * HEAD_DIM
KV_W = N_KV_HEADS * HEAD_DIM
CONV_W = D_MODEL // 2
CONV_GROUPS = 16
CONV_K = 3
WINDOW = 128
Q_BLOCK = 128
N_BUCKETS = 32
MAX_DISTANCE = 128
D_FF = 11008
FFN_CONV_K = 3
EPS = 1e-5
NEG = -1e30
IN_W = 3 * CONV_W + ATTN_W + 2 * KV_W + 2 * D_MODEL

kernel_name = "hybrid_gated_conv_swa_convffn_step"


def rmsnorm(x, g):
    xf = x.astype(jnp.float32)
    r = lax.rsqrt(jnp.mean(xf * xf, axis=-1, keepdims=True) + EPS)
    return (xf * r).astype(x.dtype) * g


def t5_bucket(dist):
    max_exact = N_BUCKETS // 2
    d = jnp.maximum(dist, 0)
    df = jnp.maximum(d, 1).astype(jnp.float32)
    large = max_exact + (jnp.log(df / max_exact) / math.log(MAX_DISTANCE / max_exact)
                         * (N_BUCKETS - max_exact)).astype(jnp.int32)
    large = jnp.minimum(large, N_BUCKETS - 1)
    return jnp.where(d < max_exact, d, large)


def rel_bias_heads(rel_bias, dist):
    b = rel_bias[t5_bucket(dist)].astype(jnp.float32)
    b = jnp.transpose(b, (2, 0, 1))
    return b.reshape(N_KV_HEADS, GROUP, dist.shape[0], dist.shape[1])


def sink_softmax(s, sinks):
    sk = sinks.astype(jnp.float32).reshape(N_KV_HEADS, GROUP)[:, :, None, None]
    m = jnp.maximum(jnp.max(s, axis=-1, keepdims=True), sk)
    e = jnp.exp(s - m)
    denom = jnp.sum(e, axis=-1, keepdims=True) + jnp.exp(sk - m)
    return e / denom


def causal_dwconv(buf, u, w):
    k_w = w.shape[0]
    t = u.shape[1]
    ext = jnp.concatenate([buf, u], axis=1)
    y = w[0] * ext[:, 0:t]
    for j in range(1, k_w):
        y = y + w[j] * ext[:, j:j + t]
    return y, ext[:, ext.shape[1] - (k_w - 1):]


def swa_prompt(q, k, v, sinks, rel_bias):
    b, s_len = q.shape[0], q.shape[1]
    nb = s_len // Q_BLOCK
    qb = q.reshape(b, nb, Q_BLOCK, N_KV_HEADS, GROUP, HEAD_DIM)
    kb = k.reshape(b, nb, Q_BLOCK, N_KV_HEADS, HEAD_DIM)
    vb = v.reshape(b, nb, Q_BLOCK, N_KV_HEADS, HEAD_DIM)
    pad = ((0, 0), (1, 0), (0, 0), (0, 0), (0, 0))
    kband = jnp.concatenate([jnp.pad(kb, pad)[:, :-1], kb], axis=2)
    vband = jnp.concatenate([jnp.pad(vb, pad)[:, :-1], vb], axis=2)
    scale = HEAD_DIM ** -0.5
    sc = jnp.einsum('bnqkgd,bnjkd->bnkgqj', qb, kband,
                    preferred_element_type=jnp.float32) * scale
    qi = jnp.arange(Q_BLOCK)[:, None]
    kj = jnp.arange(2 * Q_BLOCK)[None, :]
    dist = qi + Q_BLOCK - kj
    sc = sc + rel_bias_heads(rel_bias, dist)
    kpos = (jnp.arange(nb)[:, None] - 1) * Q_BLOCK + jnp.arange(2 * Q_BLOCK)[None, :]
    valid = ((dist >= 0) & (dist <= WINDOW))[None] & (kpos >= 0)[:, None, :]
    sc = jnp.where(valid[None, :, None, None], sc, NEG)
    p = sink_softmax(sc, sinks).astype(v.dtype)
    o = jnp.einsum('bnkgqj,bnjkd->bnqkgd', p, vband).reshape(b, s_len, ATTN_W)
    lw = min(WINDOW, s_len)
    return o, k[:, s_len - lw:], v[:, s_len - lw:]


def swa_sample(q, k, v, k_buf, v_buf, sinks, rel_bias):
    db, t = q.shape[0], q.shape[1]
    lb = k_buf.shape[1]
    kall = jnp.concatenate([k_buf, k], axis=1)
    vall = jnp.concatenate([v_buf, v], axis=1)
    qpos = PAST_LEN + jnp.arange(t)
    kpos = PAST_LEN - lb + jnp.arange(lb + t)
    dist = qpos[:, None] - kpos[None, :]
    valid = (dist >= 0) & (dist <= WINDOW)
    qg = q.reshape(db, t, N_KV_HEADS, GROUP, HEAD_DIM)
    scale = HEAD_DIM ** -0.5
    sc = jnp.einsum('btkgd,bjkd->bkgtj', qg, kall,
                    preferred_element_type=jnp.float32) * scale
    sc = sc + rel_bias_heads(rel_bias, dist)
    sc = jnp.where(valid, sc, NEG)
    p = sink_softmax(sc, sinks).astype(v.dtype)
    o = jnp.einsum('bkgtj,bjkd->btkgd', p, vall).reshape(db, t, ATTN_W)
    return o, kall[:, t:], vall[:, t:]


def decoder_layer(x, conv_buf, ffn_buf, attend, attn_norm_g, w_in, conv_w, w_branch_a,
                  w_branch_b, w_out, ffn_norm_g, w_ffn_gate, w_ffn_up, ffn_conv_w,
                  ffn_conv_b, w_ffn_down):
    b, t, _ = x.shape
    h = rmsnorm(x, attn_norm_g)
    proj = h @ w_in
    splits = np.cumsum([CONV_W, CONV_W, CONV_W, ATTN_W, KV_W, KV_W, D_MODEL]).tolist()
    cb, cc, ch, q, k, v, ga, gb = jnp.split(proj, splits, axis=-1)
    z, conv_state = causal_dwconv(conv_buf, cc * ch, conv_w)
    branch_a = (cb * z) @ w_branch_a
    o, k_state, v_state = attend(q.reshape(b, t, N_HEADS, HEAD_DIM),
                                 k.reshape(b, t, N_KV_HEADS, HEAD_DIM),
                                 v.reshape(b, t, N_KV_HEADS, HEAD_DIM))
    branch_b = o @ w_branch_b
    merged = jax.nn.sigmoid(ga) * branch_a + jax.nn.sigmoid(gb) * branch_b
    x = x + merged @ w_out
    h2 = rmsnorm(x, ffn_norm_g)
    gc, ffn_state = causal_dwconv(ffn_buf, h2 @ w_ffn_gate, ffn_conv_w)
    f = jax.nn.silu(gc + ffn_conv_b) * (h2 @ w_ffn_up)
    x = x + f @ w_ffn_down
    return x, k_state, v_state, conv_state, ffn_state


def setup_inputs(seed: int = 0) -> dict:
    key = jax.random.key(seed)
    ks = jax.random.split(key, 24)
    f32 = jnp.float32
    win_buf = min(WINDOW, PAST_LEN)

    def nrm(k, shape, scale):
        return jax.random.normal(k, shape, f32) * scale

    return {
        "x_prompt": nrm(ks[0], (BATCH, SEQ, D_MODEL), 1.0),
        "x_sample": nrm(ks[1], (DEC_BATCH, DEC_SEQ, D_MODEL), 1.0),
        "state_k_window": nrm(ks[2], (DEPTH, DEC_BATCH, win_buf, N_KV_HEADS, HEAD_DIM), 1.0),
        "state_v_window": nrm(ks[3], (DEPTH, DEC_BATCH, win_buf, N_KV_HEADS, HEAD_DIM), 1.0),
        "state_conv": nrm(ks[4], (DEPTH, DEC_BATCH, CONV_K - 1, CONV_W), 1.0),
        "state_ffn_conv": nrm(ks[5], (DEPTH, DEC_BATCH, FFN_CONV_K - 1, D_FF), 1.0),
        "attn_norm_g": 1.0 + nrm(ks[6], (DEPTH, D_MODEL), 0.02),
        "w_in": nrm(ks[7], (DEPTH, D_MODEL, IN_W), D_MODEL ** -0.5),
        "conv_w": nrm(ks[8], (DEPTH, CONV_K, CONV_W), CONV_K ** -0.5),
        "w_branch_a": nrm(ks[9], (DEPTH, CONV_W, D_MODEL), CONV_W ** -0.5),
        "w_branch_b": nrm(ks[10], (DEPTH, ATTN_W, D_MODEL), ATTN_W ** -0.5),
        "sinks": nrm(ks[11], (DEPTH, N_HEADS), 1.0),
        "w_out": nrm(ks[12], (DEPTH, D_MODEL, D_MODEL), D_MODEL ** -0.5),
        "ffn_norm_g": 1.0 + nrm(ks[13], (DEPTH, D_MODEL), 0.02),
        "w_ffn_gate": nrm(ks[14], (DEPTH, D_MODEL, D_FF), D_MODEL ** -0.5),
        "w_ffn_up": nrm(ks[15], (DEPTH, D_MODEL, D_FF), D_MODEL ** -0.5),
        "ffn_conv_w": nrm(ks[16], (DEPTH, FFN_CONV_K, D_FF), FFN_CONV_K ** -0.5),
        "ffn_conv_b": nrm(ks[17], (DEPTH, D_FF), 0.01),
        "w_ffn_down": nrm(ks[18], (DEPTH, D_FF, D_MODEL), D_FF ** -0.5),
        "rel_bias": nrm(ks[19], (N_BUCKETS, N_HEADS), 0.5),
        "final_norm_g": 1.0 + nrm(ks[20], (D_MODEL,), 0.02),
    }


def reference(x_prompt, x_sample, state_k_window, state_v_window, state_conv, state_ffn_conv,
              attn_norm_g, w_in, conv_w, w_branch_a, w_branch_b, sinks, w_out, ffn_norm_g,
              w_ffn_gate, w_ffn_up, ffn_conv_w, ffn_conv_b, w_ffn_down, rel_bias,
              final_norm_g):
    xp, xs = x_prompt, x_sample
    pk, pv, pc, pf = [], [], [], []
    sk, sv, sc, sf = [], [], [], []
    for l in range(DEPTH):
        weights = (attn_norm_g[l], w_in[l], conv_w[l], w_branch_a[l], w_branch_b[l], w_out[l],
                   ffn_norm_g[l], w_ffn_gate[l], w_ffn_up[l], ffn_conv_w[l], ffn_conv_b[l],
                   w_ffn_down[l])
        sinks_l = sinks[l]
        zc = jnp.zeros((xp.shape[0], CONV_K - 1, CONV_W), xp.dtype)
        zf = jnp.zeros((xp.shape[0], FFN_CONV_K - 1, D_FF), xp.dtype)
        attend_p = lambda q, k, v, s_=sinks_l: swa_prompt(q, k, v, s_, rel_bias)
        xp, k1, v1, c1, f1 = decoder_layer(xp, zc, zf, attend_p, *weights)
        pk.append(k1); pv.append(v1); pc.append(c1); pf.append(f1)
        kb_l, vb_l = state_k_window[l], state_v_window[l]
        attend_s = lambda q, k, v, s_=sinks_l, kb=kb_l, vb=vb_l: swa_sample(q, k, v, kb, vb, s_, rel_bias)
        xs, k2, v2, c2, f2 = decoder_layer(xs, state_conv[l], state_ffn_conv[l], attend_s, *weights)
        sk.append(k2); sv.append(v2); sc.append(c2); sf.append(f2)
    y_prompt = rmsnorm(xp, final_norm_g)
    y_sample = rmsnorm(xs, final_norm_g)
    return (y_prompt, y_sample, jnp.stack(pk), jnp.stack(pv), jnp.stack(pc), jnp.stack(pf),
            jnp.stack(sk), jnp.stack(sv), jnp.stack(sc), jnp.stack(sf))
```

```python
import functools
import math

import numpy as np
import jax
import jax.numpy as jnp
from jax import lax
from jax.experimental import pallas as pl
from jax.experimental.pallas import tpu as pltpu

F32 = jnp.float32
BF16 = jnp.bfloat16

D_MODEL = 4096
BATCH = 4
SEQ = 2048
DEC_BATCH = 128
N_HEADS = 32
N_KV_HEADS = 8
HEAD_DIM = 64
GROUP = N_HEADS // N_KV_HEADS
ATTN_W = N_HEADS * HEAD_DIM
KV_W = N_KV_HEADS * HEAD_DIM
CONV_W = D_MODEL // 2
CONV_K = 3
WINDOW = 128
Q_BLOCK = 128
N_BUCKETS = 32
MAX_DISTANCE = 128
D_FF = 11008
FFN_CONV_K = 3
EPS = 1e-5
NEG = -1e30
IN_W = 3 * CONV_W + ATTN_W + 2 * KV_W + 2 * D_MODEL
M_PROMPT = BATCH * SEQ

OFF_CB = 0
OFF_CC = CONV_W
OFF_CH = 2 * CONV_W
OFF_Q = 3 * CONV_W
OFF_K = OFF_Q + ATTN_W
OFF_V = OFF_K + KV_W
OFF_GA = OFF_V + KV_W
OFF_GB = OFF_GA + D_MODEL

V7X_VMEM_BYTES = 64 * 1024 * 1024
VMEM_LIMIT = V7X_VMEM_BYTES - 6 * 1024 * 1024
SUBLANES = 8
LANES = 128
FF_TILE = 1024
D_FF_PAD = -(-D_FF // FF_TILE) * FF_TILE


def _cparams(*sem):
    return pltpu.CompilerParams(dimension_semantics=sem, vmem_limit_bytes=VMEM_LIMIT)


def _dot(a, b):
    return jnp.dot(a, b, preferred_element_type=F32)


def _sigmoid(x):
    return 1.0 / (1.0 + jnp.exp(-x))


def _rms_kernel(x_ref, g_ref, o_ref):
    x = x_ref[...]
    r = lax.rsqrt(jnp.mean(x * x, axis=-1, keepdims=True) + EPS)
    o_ref[...] = ((x * r) * g_ref[...]).astype(o_ref.dtype)


def _rmsnorm(x, g, out_dtype, block_rows):
    rows, d = x.shape
    return pl.pallas_call(
        _rms_kernel,
        grid=(rows // block_rows,),
        in_specs=[pl.BlockSpec((block_rows, d), lambda i: (i, 0)),
                  pl.BlockSpec((1, d), lambda i: (0, 0))],
        out_specs=pl.BlockSpec((block_rows, d), lambda i: (i, 0)),
        out_shape=jax.ShapeDtypeStruct((rows, d), out_dtype),
        compiler_params=_cparams("parallel"),
        name="rmsnorm",
    )(x, g.reshape(1, d))


def _proj_kernel(xp_ref, xs_ref, w_ref, op_ref, os_ref):
    w = w_ref[...]
    op_ref[...] = _dot(xp_ref[...], w).astype(op_ref.dtype)

    @pl.when(pl.program_id(1) == pl.num_programs(1) - 1)
    def _():
        os_ref[...] = _dot(xs_ref[...], w).astype(os_ref.dtype)


def _in_proj(hp, hs, w, bm, bn):
    mp, k = hp.shape
    ms = hs.shape[0]
    n = w.shape[1]
    return pl.pallas_call(
        _proj_kernel,
        grid=(n // bn, mp // bm),
        in_specs=[pl.BlockSpec((bm, k), lambda j, i: (i, 0)),
                  pl.BlockSpec((ms, k), lambda j, i: (0, 0)),
                  pl.BlockSpec((k, bn), lambda j, i: (0, j))],
        out_specs=[pl.BlockSpec((bm, bn), lambda j, i: (i, j)),
                   pl.BlockSpec((ms, bn), lambda j, i: (0, j))],
        out_shape=[jax.ShapeDtypeStruct((mp, n), BF16),
                   jax.ShapeDtypeStruct((ms, n), BF16)],
        compiler_params=_cparams("arbitrary", "arbitrary"),
        name="in_proj",
    )(hp, hs, w)


def _mixa_prompt_kernel(cb_ref, cc_ref, ch_ref, w_ref, a_ref, st_ref, ubuf_ref):
    u = cc_ref[...].astype(F32) * ch_ref[...].astype(F32)
    ubuf_ref[0:SUBLANES, :] = jnp.zeros((SUBLANES, u.shape[1]), F32)
    ubuf_ref[SUBLANES:SUBLANES + SEQ, :] = u
    u1 = ubuf_ref[SUBLANES - 1:SUBLANES - 1 + SEQ, :]
    u2 = ubuf_ref[SUBLANES - 2:SUBLANES - 2 + SEQ, :]
    z = w_ref[0:1, :] * u2 + w_ref[1:2, :] * u1 + w_ref[2:3, :] * u
    a_ref[...] = (cb_ref[...].astype(F32) * z).astype(a_ref.dtype)
    st_ref[0] = u[SEQ - (CONV_K - 1):, :]


def _mixer_a_prompt(proj_p, conv_w, bc):
    nb = CONV_W // bc
    return pl.pallas_call(
        _mixa_prompt_kernel,
        grid=(BATCH, nb),
        in_specs=[pl.BlockSpec((SEQ, bc), lambda b, j: (b, OFF_CB // bc + j)),
                  pl.BlockSpec((SEQ, bc), lambda b, j: (b, OFF_CC // bc + j)),
                  pl.BlockSpec((SEQ, bc), lambda b, j: (b, OFF_CH // bc + j)),
                  pl.BlockSpec((CONV_K, bc), lambda b, j: (0, j))],
        out_specs=[pl.BlockSpec((SEQ, bc), lambda b, j: (b, j)),
                   pl.BlockSpec((1, CONV_K - 1, bc), lambda b, j: (b, 0, j))],
        out_shape=[jax.ShapeDtypeStruct((M_PROMPT, CONV_W), BF16),
                   jax.ShapeDtypeStruct((BATCH, CONV_K - 1, CONV_W), F32)],
        scratch_shapes=[pltpu.VMEM((SUBLANES + SEQ, bc), F32)],
        compiler_params=_cparams("parallel", "parallel"),
        name="mixer_a_prompt",
    )(proj_p, proj_p, proj_p, conv_w)


def _mixa_sample_kernel(cb_ref, cc_ref, ch_ref, w_ref, s0_ref, s1_ref, a_ref, u_ref):
    u = cc_ref[...].astype(F32) * ch_ref[...].astype(F32)
    z = w_ref[0:1, :] * s0_ref[...] + w_ref[1:2, :] * s1_ref[...] + w_ref[2:3, :] * u
    a_ref[...] = (cb_ref[...].astype(F32) * z).astype(a_ref.dtype)
    u_ref[...] = u


def _mixer_a_sample(proj_s, conv_w, s0, s1, bc):
    nb = CONV_W // bc
    ms = proj_s.shape[0]
    return pl.pallas_call(
        _mixa_sample_kernel,
        grid=(nb,),
        in_specs=[pl.BlockSpec((ms, bc), lambda j: (0, OFF_CB // bc + j)),
                  pl.BlockSpec((ms, bc), lambda j: (0, OFF_CC // bc + j)),
                  pl.BlockSpec((ms, bc), lambda j: (0, OFF_CH // bc + j)),
                  pl.BlockSpec((CONV_K, bc), lambda j: (0, j)),
                  pl.BlockSpec((ms, bc), lambda j: (0, j)),
                  pl.BlockSpec((ms, bc), lambda j: (0, j))],
        out_specs=[pl.BlockSpec((ms, bc), lambda j: (0, j)),
                   pl.BlockSpec((ms, bc), lambda j: (0, j))],
        out_shape=[jax.ShapeDtypeStruct((ms, CONV_W), BF16),
                   jax.ShapeDtypeStruct((ms, CONV_W), F32)],
        compiler_params=_cparams("parallel"),
        name="mixer_a_sample",
    )(proj_s, proj_s, proj_s, conv_w, s0, s1)


def _bucket_table():
    qi = np.arange(Q_BLOCK)[:, None]
    kj = np.arange(2 * Q_BLOCK)[None, :]
    dist = qi + Q_BLOCK - kj
    max_exact = N_BUCKETS // 2
    d = np.maximum(dist, 0)
    df = np.maximum(d, 1).astype(np.float32)
    large = max_exact + (np.log(df / np.float32(max_exact))
                         / np.float32(math.log(MAX_DISTANCE / max_exact))
                         * np.float32(N_BUCKETS - max_exact)).astype(np.int32)
    large = np.minimum(large, N_BUCKETS - 1)
    bucket = np.where(d < max_exact, d, large).astype(np.int32)
    valid = ((dist >= 0) & (dist <= WINDOW)).astype(np.int32)
    return bucket, valid


def _bias_kernel(rb_ref, bucket_ref, valid_ref, o_ref):
    h = pl.program_id(0)
    bucket = bucket_ref[...]
    acc = jnp.zeros(bucket.shape, F32)
    for b in range(N_BUCKETS):
        acc = jnp.where(bucket == b, rb_ref[b, h], acc)
    o_ref[0] = jnp.where(valid_ref[...] != 0, acc, NEG)


def _bias_table(rel_bias):
    bucket, valid = _bucket_table()
    shp = (Q_BLOCK, 2 * Q_BLOCK)
    return pl.pallas_call(
        _bias_kernel,
        grid=(N_HEADS,),
        in_specs=[pl.BlockSpec(memory_space=pltpu.SMEM),
                  pl.BlockSpec(shp, lambda h: (0, 0)),
                  pl.BlockSpec(shp, lambda h: (0, 0))],
        out_specs=pl.BlockSpec((1,) + shp, lambda h: (h, 0, 0)),
        out_shape=jax.ShapeDtypeStruct((N_HEADS,) + shp, F32),
        compiler_params=_cparams("arbitrary"),
        name="rel_bias_table",
    )(rel_bias, jnp.asarray(bucket), jnp.asarray(valid))


def _attn_prompt_kernel(sink_ref, q_ref, kp_ref, kc_ref, vp_ref, vc_ref, bias_ref, o_ref):
    n = pl.program_id(1)
    scale = HEAD_DIM ** -0.5
    col = lax.broadcasted_iota(jnp.int32, (Q_BLOCK, 2 * Q_BLOCK), 1)
    first = jnp.where((col < Q_BLOCK) & (n == 0), NEG, 0.0).astype(F32)
    for kv in range(N_KV_HEADS):
        ks = slice(kv * HEAD_DIM, (kv + 1) * HEAD_DIM)
        kband = jnp.concatenate([kp_ref[:, ks], kc_ref[:, ks]], axis=0)
        vband = jnp.concatenate([vp_ref[:, ks], vc_ref[:, ks]], axis=0)
        outs = []
        for g in range(GROUP):
            h = kv * GROUP + g
            hs = slice(h * HEAD_DIM, (h + 1) * HEAD_DIM)
            s = lax.dot_general(q_ref[:, hs], kband, (((1,), (1,)), ((), ())),
                                preferred_element_type=F32)
            s = s * scale + (bias_ref[h] + first)
            sk = sink_ref[h]
            m = jnp.maximum(jnp.max(s, axis=-1, keepdims=True), sk)
            e = jnp.exp(s - m)
            denom = jnp.sum(e, axis=-1, keepdims=True) + jnp.exp(sk - m)
            p = (e / denom).astype(BF16)
            outs.append(_dot(p, vband))
        gs = slice(kv * GROUP * HEAD_DIM, (kv + 1) * GROUP * HEAD_DIM)
        o_ref[:, gs] = jnp.concatenate(outs, axis=-1).astype(o_ref.dtype)


def _attn_prompt(proj_p, bias_tab, sinks):
    nblk = SEQ // Q_BLOCK
    qcol = OFF_Q // ATTN_W
    kcol = OFF_K // KV_W
    vcol = OFF_V // KV_W

    def cur(b, n):
        return b * nblk + n

    def prev(b, n):
        return b * nblk + jnp.maximum(n - 1, 0)

    return pl.pallas_call(
        _attn_prompt_kernel,
        grid=(BATCH, nblk),
        in_specs=[pl.BlockSpec(memory_space=pltpu.SMEM),
                  pl.BlockSpec((Q_BLOCK, ATTN_W), lambda b, n: (cur(b, n), qcol)),
                  pl.BlockSpec((Q_BLOCK, KV_W), lambda b, n: (prev(b, n), kcol)),
                  pl.BlockSpec((Q_BLOCK, KV_W), lambda b, n: (cur(b, n), kcol)),
                  pl.BlockSpec((Q_BLOCK, KV_W), lambda b, n: (prev(b, n), vcol)),
                  pl.BlockSpec((Q_BLOCK, KV_W), lambda b, n: (cur(b, n), vcol)),
                  pl.BlockSpec((N_HEADS, Q_BLOCK, 2 * Q_BLOCK), lambda b, n: (0, 0, 0))],
        out_specs=pl.BlockSpec((Q_BLOCK, ATTN_W), lambda b, n: (cur(b, n), 0)),
        out_shape=jax.ShapeDtypeStruct((M_PROMPT, ATTN_W), BF16),
        compiler_params=_cparams("parallel", "arbitrary"),
        name="attn_prompt",
    )(sinks, proj_p, proj_p, proj_p, proj_p, proj_p, bias_tab)


SAMPLES_PER_STEP = 8


def _attn_sample_kernel(q_ref, kn_ref, vn_ref, kb_ref, vb_ref, bias_ref, sink_ref,
                        o_ref, ko_ref, vo_ref):
    scale = HEAD_DIM ** -0.5
    erow = lax.broadcasted_iota(jnp.int32, (HEAD_DIM, KV_W), 0)
    ecol = lax.broadcasted_iota(jnp.int32, (HEAD_DIM, KV_W), 1)
    expand = (ecol % HEAD_DIM == erow).astype(BF16)
    hrow = lax.broadcasted_iota(jnp.int32, (N_HEADS, KV_W), 0)
    hcol = lax.broadcasted_iota(jnp.int32, (N_HEADS, KV_W), 1)
    own = (hrow // GROUP) == (hcol // HEAD_DIM)
    bias_w = bias_ref[:, 0:WINDOW]
    bias_n = bias_ref[:, WINDOW:WINDOW + 1]
    sk = sink_ref[...]
    for b in range(SAMPLES_PER_STEP):
        q = q_ref[b]
        qrow = jnp.where(own, _dot(q, expand), 0.0)
        kb = kb_ref[b]
        vb = vb_ref[b]
        kn = kn_ref[b:b + 1, :]
        vn = vn_ref[b:b + 1, :]
        s_w = lax.dot_general(qrow.astype(BF16), kb.astype(BF16),
                              (((1,), (1,)), ((), ())), preferred_element_type=F32)
        s_w = s_w * scale + bias_w
        s_n = jnp.sum(qrow * kn, axis=-1, keepdims=True) * scale + bias_n
        m = jnp.maximum(jnp.maximum(jnp.max(s_w, axis=-1, keepdims=True), s_n), sk)
        e_w = jnp.exp(s_w - m)
        e_n = jnp.exp(s_n - m)
        denom = jnp.sum(e_w, axis=-1, keepdims=True) + e_n + jnp.exp(sk - m)
        p_w = (e_w / denom).astype(BF16)
        o_all = _dot(p_w, vb.astype(BF16)) + (e_n / denom) * vn
        o_own = jnp.where(own, o_all, 0.0)
        o = o_own[:, 0:HEAD_DIM]
        for kv in range(1, N_KV_HEADS):
            o = o + o_own[:, kv * HEAD_DIM:(kv + 1) * HEAD_DIM]
        o_ref[b] = o.astype(o_ref.dtype)
        ko_ref[b] = jnp.concatenate([kb[1:, :], kn], axis=0)
        vo_ref[b] = jnp.concatenate([vb[1:, :], vn], axis=0)


def _attn_sample(q3, k_new, v_new, k_buf, v_buf, bias_s, sinks):
    g = SAMPLES_PER_STEP
    nb = DEC_BATCH // g
    win = pl.BlockSpec((g, WINDOW, KV_W), lambda i: (i, 0, 0))
    row = pl.BlockSpec((g, KV_W), lambda i: (i, 0))
    return pl.pallas_call(
        _attn_sample_kernel,
        grid=(nb,),
        in_specs=[pl.BlockSpec((g, N_HEADS, HEAD_DIM), lambda i: (i, 0, 0)),
                  row, row, win, win,
                  pl.BlockSpec((N_HEADS, WINDOW + 1), lambda i: (0, 0)),
                  pl.BlockSpec((N_HEADS, 1), lambda i: (0, 0))],
        out_specs=[pl.BlockSpec((g, N_HEADS, HEAD_DIM), lambda i: (i, 0, 0)), win, win],
        out_shape=[jax.ShapeDtypeStruct((DEC_BATCH, N_HEADS, HEAD_DIM), BF16),
                   jax.ShapeDtypeStruct((DEC_BATCH, WINDOW, KV_W), F32),
                   jax.ShapeDtypeStruct((DEC_BATCH, WINDOW, KV_W), F32)],
        compiler_params=_cparams("parallel"),
        name="attn_sample",
    )(q3, k_new, v_new, k_buf, v_buf, bias_s, sinks.reshape(N_HEADS, 1))


def _merge_kernel(ap_ref, bp_ref, as_ref, bs_ref, wa_ref, wb_ref,
                  gap_ref, gbp_ref, gas_ref, gbs_ref, op_ref, os_ref):
    wa = wa_ref[...]
    wb = wb_ref[...]

    def merged(a, b, ga, gb):
        return (_sigmoid(ga.astype(F32)) * _dot(a, wa)
                + _sigmoid(gb.astype(F32)) * _dot(b, wb))

    op_ref[...] = merged(ap_ref[...], bp_ref[...], gap_ref[...], gbp_ref[...]
                         ).astype(op_ref.dtype)

    @pl.when(pl.program_id(1) == pl.num_programs(1) - 1)
    def _():
        os_ref[...] = merged(as_ref[...], bs_ref[...], gas_ref[...], gbs_ref[...]
                             ).astype(os_ref.dtype)


def _branch_merge(a_p, o_p, a_s, o_s, wa, wb, proj_p, proj_s, bm, bn):
    mp, k = a_p.shape
    ms = a_s.shape[0]
    n = wa.shape[1]
    ga0 = OFF_GA // bn
    gb0 = OFF_GB // bn
    return pl.pallas_call(
        _merge_kernel,
        grid=(n // bn, mp // bm),
        in_specs=[pl.BlockSpec((bm, k), lambda j, i: (i, 0)),
                  pl.BlockSpec((bm, k), lambda j, i: (i, 0)),
                  pl.BlockSpec((ms, k), lambda j, i: (0, 0)),
                  pl.BlockSpec((ms, k), lambda j, i: (0, 0)),
                  pl.BlockSpec((k, bn), lambda j, i: (0, j)),
                  pl.BlockSpec((k, bn), lambda j, i: (0, j)),
                  pl.BlockSpec((bm, bn), lambda j, i: (i, ga0 + j)),
                  pl.BlockSpec((bm, bn), lambda j, i: (i, gb0 + j)),
                  pl.BlockSpec((ms, bn), lambda j, i: (0, ga0 + j)),
                  pl.BlockSpec((ms, bn), lambda j, i: (0, gb0 + j))],
        out_specs=[pl.BlockSpec((bm, bn), lambda j, i: (i, j)),
                   pl.BlockSpec((ms, bn), lambda j, i: (0, j))],
        out_shape=[jax.ShapeDtypeStruct((mp, n), BF16),
                   jax.ShapeDtypeStruct((ms, n), BF16)],
        compiler_params=_cparams("arbitrary", "arbitrary"),
        name="branch_merge",
    )(a_p, o_p, a_s, o_s, wa, wb, proj_p, proj_p, proj_s, proj_s)


def _resid_kernel(xp_ref, xs_ref, w_ref, rp_ref, rs_ref, op_ref, os_ref, accp_ref, accs_ref):
    i = pl.program_id(1)
    k = pl.program_id(2)
    last_i = i == pl.num_programs(1) - 1
    last_k = k == pl.num_programs(2) - 1
    w = w_ref[...]

    @pl.when(k == 0)
    def _():
        accp_ref[...] = rp_ref[...]

    accp_ref[...] += _dot(xp_ref[...], w)

    @pl.when(last_k)
    def _():
        op_ref[...] = accp_ref[...]

    @pl.when(last_i & (k == 0))
    def _():
        accs_ref[...] = rs_ref[...]

    @pl.when(last_i)
    def _():
        accs_ref[...] += _dot(xs_ref[...], w)

    @pl.when(last_i & last_k)
    def _():
        os_ref[...] = accs_ref[...]


def _resid_matmul(xp, xs, w, rp, rs, bm, bn, bk, name):
    mp, k = xp.shape
    ms = xs.shape[0]
    n = w.shape[1]
    return pl.pallas_call(
        _resid_kernel,
        grid=(n // bn, mp // bm, k // bk),
        in_specs=[pl.BlockSpec((bm, bk), lambda j, i, c: (i, c)),
                  pl.BlockSpec((ms, bk), lambda j, i, c: (0, c)),
                  pl.BlockSpec((bk, bn), lambda j, i, c: (c, j)),
                  pl.BlockSpec((bm, bn), lambda j, i, c: (i, j)),
                  pl.BlockSpec((ms, bn), lambda j, i, c: (0, j))],
        out_specs=[pl.BlockSpec((bm, bn), lambda j, i, c: (i, j)),
                   pl.BlockSpec((ms, bn), lambda j, i, c: (0, j))],
        out_shape=[jax.ShapeDtypeStruct((mp, n), F32),
                   jax.ShapeDtypeStruct((ms, n), F32)],
        scratch_shapes=[pltpu.VMEM((bm, bn), F32), pltpu.VMEM((ms, bn), F32)],
        compiler_params=_cparams("arbitrary", "arbitrary", "arbitrary"),
        name=name,
    )(xp, xs, w, rp, rs)


def _ffn_up_kernel(hp_ref, hs_ref, wg_ref, wu_ref, cw_ref, cb_ref, s0_ref, s1_ref,
                   fp_ref, fs_ref, gs_ref, st_ref, carry_ref, gbuf_ref, *, tiles_per_seq):
    i = pl.program_id(1)
    bm = hp_ref.shape[0]
    wg = wg_ref[...]
    wu = wu_ref[...]
    w0 = cw_ref[0:1, :]
    w1 = cw_ref[1:2, :]
    w2 = cw_ref[2:3, :]
    bias = cb_ref[...]

    @pl.when(i % tiles_per_seq == 0)
    def _():
        carry_ref[...] = jnp.zeros(carry_ref.shape, F32)

    hp = hp_ref[...]
    g = _dot(hp, wg)
    up = _dot(hp, wu)
    gbuf_ref[0:SUBLANES, :] = carry_ref[...]
    gbuf_ref[SUBLANES:SUBLANES + bm, :] = g
    g1 = gbuf_ref[SUBLANES - 1:SUBLANES - 1 + bm, :]
    g2 = gbuf_ref[SUBLANES - 2:SUBLANES - 2 + bm, :]
    gc = w0 * g2 + w1 * g1 + w2 * g + bias
    fp_ref[...] = (gc * _sigmoid(gc) * up).astype(fp_ref.dtype)
    tail = g[bm - SUBLANES:, :]
    carry_ref[...] = tail
    st_ref[0] = tail

    @pl.when(i == pl.num_programs(1) - 1)
    def _():
        hs = hs_ref[...]
        gsm = _dot(hs, wg)
        ups = _dot(hs, wu)
        gcs = w0 * s0_ref[...] + w1 * s1_ref[...] + w2 * gsm + bias
        fs_ref[...] = (gcs * _sigmoid(gcs) * ups).astype(fs_ref.dtype)
        gs_ref[...] = gsm


def _ffn_up(hp, hs, wg, wu, cw, cb, s0, s1, bm, bn):
    mp, k = hp.shape
    ms = hs.shape[0]
    n = wg.shape[1]
    tiles_per_seq = SEQ // bm
    return pl.pallas_call(
        functools.partial(_ffn_up_kernel, tiles_per_seq=tiles_per_seq),
        grid=(n // bn, mp // bm),
        in_specs=[pl.BlockSpec((bm, k), lambda j, i: (i, 0)),
                  pl.BlockSpec((ms, k), lambda j, i: (0, 0)),
                  pl.BlockSpec((k, bn), lambda j, i: (0, j)),
                  pl.BlockSpec((k, bn), lambda j, i: (0, j)),
                  pl.BlockSpec((FFN_CONV_K, bn), lambda j, i: (0, j)),
                  pl.BlockSpec((1, bn), lambda j, i: (0, j)),
                  pl.BlockSpec((ms, bn), lambda j, i: (0, j)),
                  pl.BlockSpec((ms, bn), lambda j, i: (0, j))],
        out_specs=[pl.BlockSpec((bm, bn), lambda j, i: (i, j)),
                   pl.BlockSpec((ms, bn), lambda j, i: (0, j)),
                   pl.BlockSpec((ms, bn), lambda j, i: (0, j)),
                   pl.BlockSpec((1, SUBLANES, bn), lambda j, i: (i // tiles_per_seq, 0, j))],
        out_shape=[jax.ShapeDtypeStruct((mp, n), BF16),
                   jax.ShapeDtypeStruct((ms, n), BF16),
                   jax.ShapeDtypeStruct((ms, n), F32),
                   jax.ShapeDtypeStruct((BATCH, SUBLANES, n), F32)],
        scratch_shapes=[pltpu.VMEM((SUBLANES, bn), F32),
                        pltpu.VMEM((SUBLANES + bm, bn), F32)],
        compiler_params=_cparams("arbitrary", "arbitrary"),
        name="ffn_up",
    )(hp, hs, wg, wu, cw, cb, s0, s1)


def _pad_cols(a, width):
    return jnp.pad(a, ((0, 0), (0, width - a.shape[1])))


def kernel(x_prompt, x_sample, state_k_window, state_v_window, state_conv, state_ffn_conv,
           attn_norm_g, w_in, conv_w, w_branch_a, w_branch_b, sinks, w_out, ffn_norm_g,
           w_ffn_gate, w_ffn_up, ffn_conv_w, ffn_conv_b, w_ffn_down, rel_bias,
           final_norm_g):
    xp = x_prompt.reshape(M_PROMPT, D_MODEL)
    xs = x_sample.reshape(DEC_BATCH, D_MODEL)

    w_in_b = w_in[0].astype(BF16)
    wa_b = w_branch_a[0].astype(BF16)
    wb_b = w_branch_b[0].astype(BF16)
    wo_b = w_out[0].astype(BF16)
    wg_b = _pad_cols(w_ffn_gate[0].astype(BF16), D_FF_PAD)
    wu_b = _pad_cols(w_ffn_up[0].astype(BF16), D_FF_PAD)
    wd_b = jnp.pad(w_ffn_down[0].astype(BF16), ((0, D_FF_PAD - D_FF), (0, 0)))
    fcw = _pad_cols(ffn_conv_w[0], D_FF_PAD)
    fcb = _pad_cols(ffn_conv_b[0].reshape(1, D_FF), D_FF_PAD)
    fs0 = _pad_cols(state_ffn_conv[0][:, 0, :], D_FF_PAD)
    fs1 = _pad_cols(state_ffn_conv[0][:, 1, :], D_FF_PAD)

    hp = _rmsnorm(xp, attn_norm_g[0], BF16, 256)
    hs = _rmsnorm(xs, attn_norm_g[0], BF16, DEC_BATCH)
    proj_p, proj_s = _in_proj(hp, hs, w_in_b, 1024, 1024)

    a_p, conv_p = _mixer_a_prompt(proj_p, conv_w[0], 256)
    cs0 = state_conv[0][:, 0, :]
    cs1 = state_conv[0][:, 1, :]
    a_s, u_s = _mixer_a_sample(proj_s, conv_w[0], cs0, cs1, 512)

    bias_tab = _bias_table(rel_bias)
    o_p = _attn_prompt(proj_p, bias_tab, sinks[0])
    q3 = proj_s[:, OFF_Q:OFF_Q + ATTN_W].reshape(DEC_BATCH, N_HEADS, HEAD_DIM)
    k_new = proj_s[:, OFF_K:OFF_K + KV_W].astype(F32)
    v_new = proj_s[:, OFF_V:OFF_V + KV_W].astype(F32)
    bias_s = bias_tab[:, Q_BLOCK - 1, Q_BLOCK - 1:]
    o_s3, k_win_s, v_win_s = _attn_sample(
        q3, k_new, v_new,
        state_k_window[0].reshape(DEC_BATCH, WINDOW, KV_W),
        state_v_window[0].reshape(DEC_BATCH, WINDOW, KV_W), bias_s, sinks[0])
    o_s = o_s3.reshape(DEC_BATCH, ATTN_W)

    mg_p, mg_s = _branch_merge(a_p, o_p, a_s, o_s, wa_b, wb_b, proj_p, proj_s, 1024, 512)
    x1p, x1s = _resid_matmul(mg_p, mg_s, wo_b, xp, xs, 1024, 1024, D_MODEL // 2, "out_proj")

    h2p = _rmsnorm(x1p, ffn_norm_g[0], BF16, 256)
    h2s = _rmsnorm(x1s, ffn_norm_g[0], BF16, DEC_BATCH)
    f_p, f_s, g_s, g_tail = _ffn_up(h2p, h2s, wg_b, wu_b, fcw, fcb, fs0, fs1, 1024, 512)
    x2p, x2s = _resid_matmul(f_p, f_s, wd_b, x1p, x1s, 1024, 1024, D_FF_PAD // 4, "ffn_down")
    y_p = _rmsnorm(x2p, final_norm_g, F32, 256)
    y_s = _rmsnorm(x2s, final_norm_g, F32, DEC_BATCH)

    kv_p = proj_p.reshape(BATCH, SEQ, IN_W)[:, SEQ - WINDOW:, OFF_K:OFF_GA].astype(F32)
    k_win_p = kv_p[:, :, :KV_W].reshape(1, BATCH, WINDOW, N_KV_HEADS, HEAD_DIM)
    v_win_p = kv_p[:, :, KV_W:].reshape(1, BATCH, WINDOW, N_KV_HEADS, HEAD_DIM)
    ffn_p = g_tail[:, SUBLANES - (FFN_CONV_K - 1):, :D_FF]
    conv_s = jnp.stack([cs1, u_s], axis=1)
    ffn_s = jnp.stack([fs1[:, :D_FF], g_s[:, :D_FF]], axis=1)

    return (y_p.reshape(BATCH, SEQ, D_MODEL),
            y_s.reshape(DEC_BATCH, 1, D_MODEL),
            k_win_p, v_win_p, conv_p[None], ffn_p[None],
            k_win_s.reshape(1, DEC_BATCH, WINDOW, N_KV_HEADS, HEAD_DIM),
            v_win_s.reshape(1, DEC_BATCH, WINDOW, N_KV_HEADS, HEAD_DIM),
            conv_s[None], ffn_s[None])
```

```python
import functools
import math

import numpy as np
import jax
import jax.numpy as jnp
from jax import lax
from jax.experimental import pallas as pl
from jax.experimental.pallas import tpu as pltpu

F32 = jnp.float32
BF16 = jnp.bfloat16

D_MODEL = 4096
BATCH = 4
SEQ = 2048
DEC_BATCH = 128
N_HEADS = 32
N_KV_HEADS = 8
HEAD_DIM = 64
GROUP = N_HEADS // N_KV_HEADS
ATTN_W = N_HEADS * HEAD_DIM
KV_W = N_KV_HEADS * HEAD_DIM
CONV_W = D_MODEL // 2
CONV_K = 3
WINDOW = 128
Q_BLOCK = 128
N_BUCKETS = 32
MAX_DISTANCE = 128
D_FF = 11008
FFN_CONV_K = 3
EPS = 1e-5
NEG = -1e30
IN_W = 3 * CONV_W + ATTN_W + 2 * KV_W + 2 * D_MODEL
M_PROMPT = BATCH * SEQ

OFF_CB = 0
OFF_CC = CONV_W
OFF_CH = 2 * CONV_W
OFF_Q = 3 * CONV_W
OFF_K = OFF_Q + ATTN_W
OFF_V = OFF_K + KV_W
OFF_GA = OFF_V + KV_W
OFF_GB = OFF_GA + D_MODEL

V7X_VMEM_BYTES = 64 * 1024 * 1024
VMEM_LIMIT = V7X_VMEM_BYTES - 6 * 1024 * 1024
SUBLANES = 8


def _cparams(*sem):
    return pltpu.CompilerParams(dimension_semantics=sem, vmem_limit_bytes=VMEM_LIMIT)


def _dot(a, b):
    return jnp.dot(a, b, preferred_element_type=F32)


def _sigmoid(x):
    return 1.0 / (1.0 + jnp.exp(-x))


def _ws_grid(n, bn, mp, bm):
    return (pl.cdiv(n, bn) + 1, mp // bm)


def _ws_rows(j, i):
    return (jnp.where(j > 0, i, 0), 0)


def _ws_wchunk(nj):
    return lambda j, i: (i, jnp.minimum(j, nj - 1))


def _ws_tile(col0=0):
    return lambda j, i: (jnp.where(j > 0, i, 0), col0 + jnp.maximum(j - 1, 0))


def _ws_stile(col0=0):
    return lambda j, i: (0, col0 + jnp.maximum(j - 1, 0))


def _ws_cast(wf_refs, dst_refs):
    i = pl.program_id(1)
    for wf_ref, dst_ref in zip(wf_refs, dst_refs):
        rows = wf_ref.shape[0]
        r0 = pl.multiple_of(i * rows, rows)
        dst_ref[pl.ds(r0, rows), :] = wf_ref[...].astype(BF16)


def _ws_phases(wf_refs, slot0, slot1, compute):
    j = pl.program_id(0)

    @pl.when(j == 0)
    def _():
        _ws_cast(wf_refs, slot0)

    @pl.when((j > 0) & (j % 2 == 1))
    def _():
        _ws_cast(wf_refs, slot1)
        compute(slot0)

    @pl.when((j > 0) & (j % 2 == 0))
    def _():
        _ws_cast(wf_refs, slot0)
        compute(slot1)


def _is_last_row_tile():
    return pl.program_id(1) == pl.num_programs(1) - 1


def _rms_kernel(x_ref, g_ref, o_ref):
    x = x_ref[...]
    r = lax.rsqrt(jnp.mean(x * x, axis=-1, keepdims=True) + EPS)
    o_ref[...] = ((x * r) * g_ref[...]).astype(o_ref.dtype)


def _rmsnorm(x, g, out_dtype, block_rows):
    rows, d = x.shape
    return pl.pallas_call(
        _rms_kernel,
        grid=(rows // block_rows,),
        in_specs=[pl.BlockSpec((block_rows, d), lambda i: (i, 0)),
                  pl.BlockSpec((1, d), lambda i: (0, 0))],
        out_specs=pl.BlockSpec((block_rows, d), lambda i: (i, 0)),
        out_shape=jax.ShapeDtypeStruct((rows, d), out_dtype),
        compiler_params=_cparams("parallel"),
        name="rmsnorm",
    )(x, g.reshape(1, d))


def _proj_kernel(xp_ref, xs_ref, wf_ref, op_ref, os_ref, w0_ref, w1_ref):
    def compute(slot):
        (w_ref,) = slot
        op_ref[...] = _dot(xp_ref[...], w_ref[...]).astype(op_ref.dtype)

        @pl.when(_is_last_row_tile())
        def _():
            os_ref[...] = _dot(xs_ref[...], w_ref[...]).astype(os_ref.dtype)

    _ws_phases([wf_ref], [w0_ref], [w1_ref], compute)


def _in_proj(hp, hs, w, bm, bn):
    mp, k = hp.shape
    ms = hs.shape[0]
    n = w.shape[1]
    grid = _ws_grid(n, bn, mp, bm)
    return pl.pallas_call(
        _proj_kernel,
        grid=grid,
        in_specs=[pl.BlockSpec((bm, k), _ws_rows),
                  pl.BlockSpec((ms, k), lambda j, i: (0, 0)),
                  pl.BlockSpec((k // grid[1], bn), _ws_wchunk(grid[0] - 1))],
        out_specs=[pl.BlockSpec((bm, bn), _ws_tile()),
                   pl.BlockSpec((ms, bn), _ws_stile())],
        out_shape=[jax.ShapeDtypeStruct((mp, n), BF16),
                   jax.ShapeDtypeStruct((ms, n), BF16)],
        scratch_shapes=[pltpu.VMEM((k, bn), BF16), pltpu.VMEM((k, bn), BF16)],
        compiler_params=_cparams("arbitrary", "arbitrary"),
        name="in_proj",
    )(hp, hs, w)


def _mixa_prompt_kernel(cb_ref, cc_ref, ch_ref, w_ref, a_ref, st_ref, ubuf_ref):
    u = cc_ref[...].astype(F32) * ch_ref[...].astype(F32)
    ubuf_ref[0:SUBLANES, :] = jnp.zeros((SUBLANES, u.shape[1]), F32)
    ubuf_ref[SUBLANES:SUBLANES + SEQ, :] = u
    u1 = ubuf_ref[SUBLANES - 1:SUBLANES - 1 + SEQ, :]
    u2 = ubuf_ref[SUBLANES - 2:SUBLANES - 2 + SEQ, :]
    z = w_ref[0:1, :] * u2 + w_ref[1:2, :] * u1 + w_ref[2:3, :] * u
    a_ref[...] = (cb_ref[...].astype(F32) * z).astype(a_ref.dtype)
    st_ref[0] = u[SEQ - (CONV_K - 1):, :]


def _mixer_a_prompt(proj_p, conv_w, bc):
    nb = CONV_W // bc
    return pl.pallas_call(
        _mixa_prompt_kernel,
        grid=(BATCH, nb),
        in_specs=[pl.BlockSpec((SEQ, bc), lambda b, j: (b, OFF_CB // bc + j)),
                  pl.BlockSpec((SEQ, bc), lambda b, j: (b, OFF_CC // bc + j)),
                  pl.BlockSpec((SEQ, bc), lambda b, j: (b, OFF_CH // bc + j)),
                  pl.BlockSpec((CONV_K, bc), lambda b, j: (0, j))],
        out_specs=[pl.BlockSpec((SEQ, bc), lambda b, j: (b, j)),
                   pl.BlockSpec((1, CONV_K - 1, bc), lambda b, j: (b, 0, j))],
        out_shape=[jax.ShapeDtypeStruct((M_PROMPT, CONV_W), BF16),
                   jax.ShapeDtypeStruct((BATCH, CONV_K - 1, CONV_W), F32)],
        scratch_shapes=[pltpu.VMEM((SUBLANES + SEQ, bc), F32)],
        compiler_params=_cparams("parallel", "parallel"),
        name="mixer_a_prompt",
    )(proj_p, proj_p, proj_p, conv_w)


def _mixa_sample_kernel(cb_ref, cc_ref, ch_ref, w_ref, s0_ref, s1_ref, a_ref, u_ref):
    u = cc_ref[...].astype(F32) * ch_ref[...].astype(F32)
    z = w_ref[0:1, :] * s0_ref[...] + w_ref[1:2, :] * s1_ref[...] + w_ref[2:3, :] * u
    a_ref[...] = (cb_ref[...].astype(F32) * z).astype(a_ref.dtype)
    u_ref[...] = u


def _mixer_a_sample(proj_s, conv_w, s0, s1, bc):
    nb = CONV_W // bc
    ms = proj_s.shape[0]
    return pl.pallas_call(
        _mixa_sample_kernel,
        grid=(nb,),
        in_specs=[pl.BlockSpec((ms, bc), lambda j: (0, OFF_CB // bc + j)),
                  pl.BlockSpec((ms, bc), lambda j: (0, OFF_CC // bc + j)),
                  pl.BlockSpec((ms, bc), lambda j: (0, OFF_CH // bc + j)),
                  pl.BlockSpec((CONV_K, bc), lambda j: (0, j)),
                  pl.BlockSpec((ms, bc), lambda j: (0, j)),
                  pl.BlockSpec((ms, bc), lambda j: (0, j))],
        out_specs=[pl.BlockSpec((ms, bc), lambda j: (0, j)),
                   pl.BlockSpec((ms, bc), lambda j: (0, j))],
        out_shape=[jax.ShapeDtypeStruct((ms, CONV_W), BF16),
                   jax.ShapeDtypeStruct((ms, CONV_W), F32)],
        compiler_params=_cparams("parallel"),
        name="mixer_a_sample",
    )(proj_s, proj_s, proj_s, conv_w, s0, s1)


def _bucket_table():
    qi = np.arange(Q_BLOCK)[:, None]
    kj = np.arange(2 * Q_BLOCK)[None, :]
    dist = qi + Q_BLOCK - kj
    max_exact = N_BUCKETS // 2
    d = np.maximum(dist, 0)
    df = np.maximum(d, 1).astype(np.float32)
    large = max_exact + (np.log(df / np.float32(max_exact))
                         / np.float32(math.log(MAX_DISTANCE / max_exact))
                         * np.float32(N_BUCKETS - max_exact)).astype(np.int32)
    large = np.minimum(large, N_BUCKETS - 1)
    bucket = np.where(d < max_exact, d, large).astype(np.int32)
    valid = ((dist >= 0) & (dist <= WINDOW)).astype(np.int32)
    return bucket, valid


def _bias_kernel(rb_ref, bucket_ref, valid_ref, o_ref):
    h = pl.program_id(0)
    bucket = bucket_ref[...]
    acc = jnp.zeros(bucket.shape, F32)
    for b in range(N_BUCKETS):
        acc = jnp.where(bucket == b, rb_ref[b, h], acc)
    o_ref[0] = jnp.where(valid_ref[...] != 0, acc, NEG)


def _bias_table(rel_bias):
    bucket, valid = _bucket_table()
    shp = (Q_BLOCK, 2 * Q_BLOCK)
    return pl.pallas_call(
        _bias_kernel,
        grid=(N_HEADS,),
        in_specs=[pl.BlockSpec(memory_space=pltpu.SMEM),
                  pl.BlockSpec(shp, lambda h: (0, 0)),
                  pl.BlockSpec(shp, lambda h: (0, 0))],
        out_specs=pl.BlockSpec((1,) + shp, lambda h: (h, 0, 0)),
        out_shape=jax.ShapeDtypeStruct((N_HEADS,) + shp, F32),
        compiler_params=_cparams("arbitrary"),
        name="rel_bias_table",
    )(rel_bias, jnp.asarray(bucket), jnp.asarray(valid))


def _attn_prompt_kernel(sink_ref, q_ref, kp_ref, kc_ref, vp_ref, vc_ref, bias_ref, o_ref):
    n = pl.program_id(1)
    scale = HEAD_DIM ** -0.5
    col = lax.broadcasted_iota(jnp.int32, (Q_BLOCK, 2 * Q_BLOCK), 1)
    first = jnp.where((col < Q_BLOCK) & (n == 0), NEG, 0.0).astype(F32)
    for kv in range(N_KV_HEADS):
        ks = slice(kv * HEAD_DIM, (kv + 1) * HEAD_DIM)
        kband = jnp.concatenate([kp_ref[:, ks], kc_ref[:, ks]], axis=0)
        vband = jnp.concatenate([vp_ref[:, ks], vc_ref[:, ks]], axis=0)
        outs = []
        for g in range(GROUP):
            h = kv * GROUP + g
            hs = slice(h * HEAD_DIM, (h + 1) * HEAD_DIM)
            s = lax.dot_general(q_ref[:, hs], kband, (((1,), (1,)), ((), ())),
                                preferred_element_type=F32)
            s = s * scale + (bias_ref[h] + first)
            sk = sink_ref[h]
            m = jnp.maximum(jnp.max(s, axis=-1, keepdims=True), sk)
            e = jnp.exp(s - m)
            denom = jnp.sum(e, axis=-1, keepdims=True) + jnp.exp(sk - m)
            p = (e / denom).astype(BF16)
            outs.append(_dot(p, vband))
        gs = slice(kv * GROUP * HEAD_DIM, (kv + 1) * GROUP * HEAD_DIM)
        o_ref[:, gs] = jnp.concatenate(outs, axis=-1).astype(o_ref.dtype)


def _attn_prompt(proj_p, bias_tab, sinks):
    nblk = SEQ // Q_BLOCK
    qcol = OFF_Q // ATTN_W
    kcol = OFF_K // KV_W
    vcol = OFF_V // KV_W

    def cur(b, n):
        return b * nblk + n

    def prev(b, n):
        return b * nblk + jnp.maximum(n - 1, 0)

    return pl.pallas_call(
        _attn_prompt_kernel,
        grid=(BATCH, nblk),
        in_specs=[pl.BlockSpec(memory_space=pltpu.SMEM),
                  pl.BlockSpec((Q_BLOCK, ATTN_W), lambda b, n: (cur(b, n), qcol)),
                  pl.BlockSpec((Q_BLOCK, KV_W), lambda b, n: (prev(b, n), kcol)),
                  pl.BlockSpec((Q_BLOCK, KV_W), lambda b, n: (cur(b, n), kcol)),
                  pl.BlockSpec((Q_BLOCK, KV_W), lambda b, n: (prev(b, n), vcol)),
                  pl.BlockSpec((Q_BLOCK, KV_W), lambda b, n: (cur(b, n), vcol)),
                  pl.BlockSpec((N_HEADS, Q_BLOCK, 2 * Q_BLOCK), lambda b, n: (0, 0, 0))],
        out_specs=pl.BlockSpec((Q_BLOCK, ATTN_W), lambda b, n: (cur(b, n), 0)),
        out_shape=jax.ShapeDtypeStruct((M_PROMPT, ATTN_W), BF16),
        compiler_params=_cparams("parallel", "arbitrary"),
        name="attn_prompt",
    )(sinks, proj_p, proj_p, proj_p, proj_p, proj_p, bias_tab)


SAMPLES_PER_STEP = 8


def _attn_sample_kernel(q_ref, kn_ref, vn_ref, kb_ref, vb_ref, bias_ref, sink_ref,
                        o_ref, ko_ref, vo_ref):
    scale = HEAD_DIM ** -0.5
    erow = lax.broadcasted_iota(jnp.int32, (HEAD_DIM, KV_W), 0)
    ecol = lax.broadcasted_iota(jnp.int32, (HEAD_DIM, KV_W), 1)
    expand = (ecol % HEAD_DIM == erow).astype(BF16)
    hrow = lax.broadcasted_iota(jnp.int32, (N_HEADS, KV_W), 0)
    hcol = lax.broadcasted_iota(jnp.int32, (N_HEADS, KV_W), 1)
    own = (hrow // GROUP) == (hcol // HEAD_DIM)
    bias_w = bias_ref[:, 0:WINDOW]
    bias_n = bias_ref[:, WINDOW:WINDOW + 1]
    sk = sink_ref[...]
    for b in range(SAMPLES_PER_STEP):
        q = q_ref[b]
        qrow = jnp.where(own, _dot(q, expand), 0.0)
        kb = kb_ref[b]
        vb = vb_ref[b]
        kn = kn_ref[b:b + 1, :]
        vn = vn_ref[b:b + 1, :]
        s_w = lax.dot_general(qrow.astype(BF16), kb.astype(BF16),
                              (((1,), (1,)), ((), ())), preferred_element_type=F32)
        s_w = s_w * scale + bias_w
        s_n = jnp.sum(qrow * kn, axis=-1, keepdims=True) * scale + bias_n
        m = jnp.maximum(jnp.maximum(jnp.max(s_w, axis=-1, keepdims=True), s_n), sk)
        e_w = jnp.exp(s_w - m)
        e_n = jnp.exp(s_n - m)
        denom = jnp.sum(e_w, axis=-1, keepdims=True) + e_n + jnp.exp(sk - m)
        p_w = (e_w / denom).astype(BF16)
        o_all = _dot(p_w, vb.astype(BF16)) + (e_n / denom) * vn
        o_own = jnp.where(own, o_all, 0.0)
        o = o_own[:, 0:HEAD_DIM]
        for kv in range(1, N_KV_HEADS):
            o = o + o_own[:, kv * HEAD_DIM:(kv + 1) * HEAD_DIM]
        o_ref[b] = o.astype(o_ref.dtype)
        ko_ref[b] = jnp.concatenate([kb[1:, :], kn], axis=0)
        vo_ref[b] = jnp.concatenate([vb[1:, :], vn], axis=0)


def _attn_sample(q3, k_new, v_new, k_buf, v_buf, bias_s, sinks):
    g = SAMPLES_PER_STEP
    nb = DEC_BATCH // g
    win = pl.BlockSpec((g, WINDOW, KV_W), lambda i: (i, 0, 0))
    row = pl.BlockSpec((g, KV_W), lambda i: (i, 0))
    return pl.pallas_call(
        _attn_sample_kernel,
        grid=(nb,),
        in_specs=[pl.BlockSpec((g, N_HEADS, HEAD_DIM), lambda i: (i, 0, 0)),
                  row, row, win, win,
                  pl.BlockSpec((N_HEADS, WINDOW + 1), lambda i: (0, 0)),
                  pl.BlockSpec((N_HEADS, 1), lambda i: (0, 0))],
        out_specs=[pl.BlockSpec((g, N_HEADS, HEAD_DIM), lambda i: (i, 0, 0)), win, win],
        out_shape=[jax.ShapeDtypeStruct((DEC_BATCH, N_HEADS, HEAD_DIM), BF16),
                   jax.ShapeDtypeStruct((DEC_BATCH, WINDOW, KV_W), F32),
                   jax.ShapeDtypeStruct((DEC_BATCH, WINDOW, KV_W), F32)],
        compiler_params=_cparams("parallel"),
        name="attn_sample",
    )(q3, k_new, v_new, k_buf, v_buf, bias_s, sinks.reshape(N_HEADS, 1))


def _merge_kernel(ap_ref, bp_ref, as_ref, bs_ref, waf_ref, wbf_ref,
                  gap_ref, gbp_ref, gas_ref, gbs_ref, op_ref, os_ref,
                  wa0_ref, wb0_ref, wa1_ref, wb1_ref):
    def compute(slot):
        wa_ref, wb_ref = slot

        def merged(a_ref, b_ref, ga_ref, gb_ref):
            return (_sigmoid(ga_ref[...].astype(F32)) * _dot(a_ref[...], wa_ref[...])
                    + _sigmoid(gb_ref[...].astype(F32)) * _dot(b_ref[...], wb_ref[...]))

        op_ref[...] = merged(ap_ref, bp_ref, gap_ref, gbp_ref).astype(op_ref.dtype)

        @pl.when(_is_last_row_tile())
        def _():
            os_ref[...] = merged(as_ref, bs_ref, gas_ref, gbs_ref).astype(os_ref.dtype)

    _ws_phases([waf_ref, wbf_ref], [wa0_ref, wb0_ref], [wa1_ref, wb1_ref], compute)


def _branch_merge(a_p, o_p, a_s, o_s, wa, wb, proj_p, proj_s, bm, bn):
    mp, k = a_p.shape
    ms = a_s.shape[0]
    n = wa.shape[1]
    grid = _ws_grid(n, bn, mp, bm)
    ga0 = OFF_GA // bn
    gb0 = OFF_GB // bn
    wchunk = pl.BlockSpec((k // grid[1], bn), _ws_wchunk(grid[0] - 1))
    wslot = pltpu.VMEM((k, bn), BF16)
    return pl.pallas_call(
        _merge_kernel,
        grid=grid,
        in_specs=[pl.BlockSpec((bm, k), _ws_rows),
                  pl.BlockSpec((bm, k), _ws_rows),
                  pl.BlockSpec((ms, k), lambda j, i: (0, 0)),
                  pl.BlockSpec((ms, k), lambda j, i: (0, 0)),
                  wchunk, wchunk,
                  pl.BlockSpec((bm, bn), _ws_tile(ga0)),
                  pl.BlockSpec((bm, bn), _ws_tile(gb0)),
                  pl.BlockSpec((ms, bn), _ws_stile(ga0)),
                  pl.BlockSpec((ms, bn), _ws_stile(gb0))],
        out_specs=[pl.BlockSpec((bm, bn), _ws_tile()),
                   pl.BlockSpec((ms, bn), _ws_stile())],
        out_shape=[jax.ShapeDtypeStruct((mp, n), BF16),
                   jax.ShapeDtypeStruct((ms, n), BF16)],
        scratch_shapes=[wslot, wslot, wslot, wslot],
        compiler_params=_cparams("arbitrary", "arbitrary"),
        name="branch_merge",
    )(a_p, o_p, a_s, o_s, wa, wb, proj_p, proj_p, proj_s, proj_s)


def _resid_kernel(xp_ref, xs_ref, wf_ref, rp_ref, rs_ref, op_ref, os_ref, w0_ref, w1_ref):
    def compute(slot):
        (w_ref,) = slot
        op_ref[...] = rp_ref[...] + _dot(xp_ref[...], w_ref[...])

        @pl.when(_is_last_row_tile())
        def _():
            os_ref[...] = rs_ref[...] + _dot(xs_ref[...], w_ref[...])

    _ws_phases([wf_ref], [w0_ref], [w1_ref], compute)


def _resid_matmul(xp, xs, w, rp, rs, bm, bn, name):
    mp, k = xp.shape
    ms = xs.shape[0]
    n = w.shape[1]
    grid = _ws_grid(n, bn, mp, bm)
    return pl.pallas_call(
        _resid_kernel,
        grid=grid,
        in_specs=[pl.BlockSpec((bm, k), _ws_rows),
                  pl.BlockSpec((ms, k), lambda j, i: (0, 0)),
                  pl.BlockSpec((k // grid[1], bn), _ws_wchunk(grid[0] - 1)),
                  pl.BlockSpec((bm, bn), _ws_tile()),
                  pl.BlockSpec((ms, bn), _ws_stile())],
        out_specs=[pl.BlockSpec((bm, bn), _ws_tile()),
                   pl.BlockSpec((ms, bn), _ws_stile())],
        out_shape=[jax.ShapeDtypeStruct((mp, n), F32),
                   jax.ShapeDtypeStruct((ms, n), F32)],
        scratch_shapes=[pltpu.VMEM((k, bn), BF16), pltpu.VMEM((k, bn), BF16)],
        compiler_params=_cparams("arbitrary", "arbitrary"),
        name=name,
    )(xp, xs, w, rp, rs)


def _ffn_up_kernel(hp_ref, hs_ref, wgf_ref, wuf_ref, cw_ref, cb_ref, s0_ref, s1_ref,
                   fp_ref, fs_ref, gs_ref, st_ref,
                   wg0_ref, wu0_ref, wg1_ref, wu1_ref, carry_ref, gbuf_ref, *, tiles_per_seq):
    i = pl.program_id(1)
    bm = hp_ref.shape[0]

    def compute(slot):
        wg_ref, wu_ref = slot
        w0 = cw_ref[0:1, :]
        w1 = cw_ref[1:2, :]
        w2 = cw_ref[2:3, :]
        bias = cb_ref[...]

        @pl.when(i % tiles_per_seq == 0)
        def _():
            carry_ref[...] = jnp.zeros(carry_ref.shape, F32)

        g = _dot(hp_ref[...], wg_ref[...])
        gbuf_ref[0:SUBLANES, :] = carry_ref[...]
        gbuf_ref[SUBLANES:SUBLANES + bm, :] = g
        g1 = gbuf_ref[SUBLANES - 1:SUBLANES - 1 + bm, :]
        g2 = gbuf_ref[SUBLANES - 2:SUBLANES - 2 + bm, :]
        gc = w0 * g2 + w1 * g1 + w2 * g + bias
        up = _dot(hp_ref[...], wu_ref[...])
        fp_ref[...] = (gc * _sigmoid(gc) * up).astype(fp_ref.dtype)
        tail = gbuf_ref[bm:SUBLANES + bm, :]
        carry_ref[...] = tail
        st_ref[0] = tail

        @pl.when(_is_last_row_tile())
        def _():
            gsm = _dot(hs_ref[...], wg_ref[...])
            ups = _dot(hs_ref[...], wu_ref[...])
            gcs = w0 * s0_ref[...] + w1 * s1_ref[...] + w2 * gsm + bias
            fs_ref[...] = (gcs * _sigmoid(gcs) * ups).astype(fs_ref.dtype)
            gs_ref[...] = gsm

    _ws_phases([wgf_ref, wuf_ref], [wg0_ref, wu0_ref], [wg1_ref, wu1_ref], compute)


def _ffn_up(hp, hs, wg, wu, cw, cb, s0, s1, bm, bn):
    mp, k = hp.shape
    ms = hs.shape[0]
    n = wg.shape[1]
    grid = _ws_grid(n, bn, mp, bm)
    tiles_per_seq = SEQ // bm
    wchunk = pl.BlockSpec((k // grid[1], bn), _ws_wchunk(grid[0] - 1))
    wslot = pltpu.VMEM((k, bn), BF16)
    col = _ws_stile()

    def tail_map(j, i):
        return (jnp.where(j > 0, i, 0) // tiles_per_seq, 0, jnp.maximum(j - 1, 0))

    return pl.pallas_call(
        functools.partial(_ffn_up_kernel, tiles_per_seq=tiles_per_seq),
        grid=grid,
        in_specs=[pl.BlockSpec((bm, k), _ws_rows),
                  pl.BlockSpec((ms, k), lambda j, i: (0, 0)),
                  wchunk, wchunk,
                  pl.BlockSpec((FFN_CONV_K, bn), col),
                  pl.BlockSpec((1, bn), col),
                  pl.BlockSpec((ms, bn), col),
                  pl.BlockSpec((ms, bn), col)],
        out_specs=[pl.BlockSpec((bm, bn), _ws_tile()),
                   pl.BlockSpec((ms, bn), col),
                   pl.BlockSpec((ms, bn), col),
                   pl.BlockSpec((1, SUBLANES, bn), tail_map)],
        out_shape=[jax.ShapeDtypeStruct((mp, n), BF16),
                   jax.ShapeDtypeStruct((ms, n), BF16),
                   jax.ShapeDtypeStruct((ms, n), F32),
                   jax.ShapeDtypeStruct((BATCH, SUBLANES, n), F32)],
        scratch_shapes=[wslot, wslot, wslot, wslot,
                        pltpu.VMEM((SUBLANES, bn), F32),
                        pltpu.VMEM((SUBLANES + bm, bn), F32)],
        compiler_params=_cparams("arbitrary", "arbitrary"),
        name="ffn_up",
    )(hp, hs, wg, wu, cw, cb, s0, s1)


def kernel(x_prompt, x_sample, state_k_window, state_v_window, state_conv, state_ffn_conv,
           attn_norm_g, w_in, conv_w, w_branch_a, w_branch_b, sinks, w_out, ffn_norm_g,
           w_ffn_gate, w_ffn_up, ffn_conv_w, ffn_conv_b, w_ffn_down, rel_bias,
           final_norm_g):
    xp = x_prompt.reshape(M_PROMPT, D_MODEL)
    xs = x_sample.reshape(DEC_BATCH, D_MODEL)

    fs0 = state_ffn_conv[0][:, 0, :]
    fs1 = state_ffn_conv[0][:, 1, :]

    hp = _rmsnorm(xp, attn_norm_g[0], BF16, 256)
    hs = _rmsnorm(xs, attn_norm_g[0], BF16, DEC_BATCH)
    proj_p, proj_s = _in_proj(hp, hs, w_in[0], 1024, 1024)

    a_p, conv_p = _mixer_a_prompt(proj_p, conv_w[0], 256)
    cs0 = state_conv[0][:, 0, :]
    cs1 = state_conv[0][:, 1, :]
    a_s, u_s = _mixer_a_sample(proj_s, conv_w[0], cs0, cs1, 512)

    bias_tab = _bias_table(rel_bias)
    o_p = _attn_prompt(proj_p, bias_tab, sinks[0])
    q3 = proj_s[:, OFF_Q:OFF_Q + ATTN_W].reshape(DEC_BATCH, N_HEADS, HEAD_DIM)
    k_new = proj_s[:, OFF_K:OFF_K + KV_W].astype(F32)
    v_new = proj_s[:, OFF_V:OFF_V + KV_W].astype(F32)
    bias_s = bias_tab[:, Q_BLOCK - 1, Q_BLOCK - 1:]
    o_s3, k_win_s, v_win_s = _attn_sample(
        q3, k_new, v_new,
        state_k_window[0].reshape(DEC_BATCH, WINDOW, KV_W),
        state_v_window[0].reshape(DEC_BATCH, WINDOW, KV_W), bias_s, sinks[0])
    o_s = o_s3.reshape(DEC_BATCH, ATTN_W)

    mg_p, mg_s = _branch_merge(a_p, o_p, a_s, o_s, w_branch_a[0], w_branch_b[0],
                               proj_p, proj_s, 1024, 512)
    x1p, x1s = _resid_matmul(mg_p, mg_s, w_out[0], xp, xs, 1024, 512, "out_proj")

    h2p = _rmsnorm(x1p, ffn_norm_g[0], BF16, 256)
    h2s = _rmsnorm(x1s, ffn_norm_g[0], BF16, DEC_BATCH)
    f_p, f_s, g_s, g_tail = _ffn_up(h2p, h2s, w_ffn_gate[0], w_ffn_up[0], ffn_conv_w[0],
                                    ffn_conv_b, fs0, fs1, 1024, 512)
    x2p, x2s = _resid_matmul(f_p, f_s, w_ffn_down[0], x1p, x1s, 512, 512, "ffn_down")
    y_p = _rmsnorm(x2p, final_norm_g, F32, 256)
    y_s = _rmsnorm(x2s, final_norm_g, F32, DEC_BATCH)

    kv_p = proj_p.reshape(BATCH, SEQ, IN_W)[:, SEQ - WINDOW:, OFF_K:OFF_GA].astype(F32)
    k_win_p = kv_p[:, :, :KV_W].reshape(1, BATCH, WINDOW, N_KV_HEADS, HEAD_DIM)
    v_win_p = kv_p[:, :, KV_W:].reshape(1, BATCH, WINDOW, N_KV_HEADS, HEAD_DIM)
    ffn_p = g_tail[:, SUBLANES - (FFN_CONV_K - 1):, :]
    conv_s = jnp.stack([cs1, u_s], axis=1)
    ffn_s = jnp.stack([fs1, g_s], axis=1)

    return (y_p.reshape(BATCH, SEQ, D_MODEL),
            y_s.reshape(DEC_BATCH, 1, D_MODEL),
            k_win_p, v_win_p, conv_p[None], ffn_p[None],
            k_win_s.reshape(1, DEC_BATCH, WINDOW, N_KV_HEADS, HEAD_DIM),
            v_win_s.reshape(1, DEC_BATCH, WINDOW, N_KV_HEADS, HEAD_DIM),
            conv_s[None], ffn_s[None])
```

```python
import functools
import math

import numpy as np
import jax
import jax.numpy as jnp
from jax import lax
from jax.experimental import pallas as pl
from jax.experimental.pallas import tpu as pltpu

F32 = jnp.float32
BF16 = jnp.bfloat16

D_MODEL = 4096
BATCH = 4
SEQ = 2048
DEC_BATCH = 128
N_HEADS = 32
N_KV_HEADS = 8
HEAD_DIM = 64
GROUP = N_HEADS // N_KV_HEADS
ATTN_W = N_HEADS * HEAD_DIM
KV_W = N_KV_HEADS * HEAD_DIM
CONV_W = D_MODEL // 2
CONV_K = 3
WINDOW = 128
Q_BLOCK = 128
N_BUCKETS = 32
MAX_DISTANCE = 128
D_FF = 11008
FFN_CONV_K = 3
EPS = 1e-5
NEG = -1e30
IN_W = 3 * CONV_W + ATTN_W + 2 * KV_W + 2 * D_MODEL
M_PROMPT = BATCH * SEQ

OFF_CB = 0
OFF_CC = CONV_W
OFF_CH = 2 * CONV_W
OFF_Q = 3 * CONV_W
OFF_K = OFF_Q + ATTN_W
OFF_V = OFF_K + KV_W
OFF_GA = OFF_V + KV_W
OFF_GB = OFF_GA + D_MODEL

V7X_VMEM_BYTES = 64 * 1024 * 1024
VMEM_LIMIT = V7X_VMEM_BYTES - 6 * 1024 * 1024
SUBLANES = 8


def _cparams(*sem):
    return pltpu.CompilerParams(dimension_semantics=sem, vmem_limit_bytes=VMEM_LIMIT)


def _dot(a, b):
    return jnp.dot(a, b, preferred_element_type=F32)


def _sigmoid(x):
    return 1.0 / (1.0 + jnp.exp(-x))


def _ws_grid(n, bn, mp, bm):
    return (pl.cdiv(n, bn) + 1, mp // bm)


def _ws_rows(j, i):
    return (jnp.where(j > 0, i, 0), 0)


def _ws_wchunk(nj):
    return lambda j, i: (i, jnp.minimum(j, nj - 1))


def _ws_tile(col0=0):
    return lambda j, i: (jnp.where(j > 0, i, 0), col0 + jnp.maximum(j - 1, 0))


def _ws_stile(col0=0):
    return lambda j, i: (0, col0 + jnp.maximum(j - 1, 0))


def _ws_cast(wf_refs, dst_refs):
    i = pl.program_id(1)
    for wf_ref, dst_ref in zip(wf_refs, dst_refs):
        rows = wf_ref.shape[0]
        r0 = pl.multiple_of(i * rows, rows)
        dst_ref[pl.ds(r0, rows), :] = wf_ref[...].astype(BF16)


def _ws_phases(wf_refs, slot0, slot1, compute):
    j = pl.program_id(0)

    @pl.when(j == 0)
    def _():
        _ws_cast(wf_refs, slot0)

    @pl.when((j > 0) & (j % 2 == 1))
    def _():
        _ws_cast(wf_refs, slot1)
        compute(slot0)

    @pl.when((j > 0) & (j % 2 == 0))
    def _():
        _ws_cast(wf_refs, slot0)
        compute(slot1)


def _is_last_row_tile():
    return pl.program_id(1) == pl.num_programs(1) - 1


def _rms_kernel(x_ref, g_ref, o_ref):
    x = x_ref[...]
    r = lax.rsqrt(jnp.mean(x * x, axis=-1, keepdims=True) + EPS)
    o_ref[...] = ((x * r) * g_ref[...]).astype(o_ref.dtype)


def _rmsnorm(x, g, out_dtype, block_rows):
    rows, d = x.shape
    return pl.pallas_call(
        _rms_kernel,
        grid=(rows // block_rows,),
        in_specs=[pl.BlockSpec((block_rows, d), lambda i: (i, 0)),
                  pl.BlockSpec((1, d), lambda i: (0, 0))],
        out_specs=pl.BlockSpec((block_rows, d), lambda i: (i, 0)),
        out_shape=jax.ShapeDtypeStruct((rows, d), out_dtype),
        compiler_params=_cparams("parallel"),
        name="rmsnorm",
    )(x, g.reshape(1, d))


def _proj_kernel(xp_ref, xs_ref, wf_ref, op_ref, os_ref, w0_ref, w1_ref):
    def compute(slot):
        (w_ref,) = slot
        op_ref[...] = _dot(xp_ref[...], w_ref[...]).astype(op_ref.dtype)

        @pl.when(_is_last_row_tile())
        def _():
            os_ref[...] = _dot(xs_ref[...], w_ref[...]).astype(os_ref.dtype)

    _ws_phases([wf_ref], [w0_ref], [w1_ref], compute)


def _in_proj(hp, hs, w, bm, bn):
    mp, k = hp.shape
    ms = hs.shape[0]
    n = w.shape[1]
    grid = _ws_grid(n, bn, mp, bm)
    return pl.pallas_call(
        _proj_kernel,
        grid=grid,
        in_specs=[pl.BlockSpec((bm, k), _ws_rows),
                  pl.BlockSpec((ms, k), lambda j, i: (0, 0)),
                  pl.BlockSpec((k // grid[1], bn), _ws_wchunk(grid[0] - 1))],
        out_specs=[pl.BlockSpec((bm, bn), _ws_tile()),
                   pl.BlockSpec((ms, bn), _ws_stile())],
        out_shape=[jax.ShapeDtypeStruct((mp, n), BF16),
                   jax.ShapeDtypeStruct((ms, n), BF16)],
        scratch_shapes=[pltpu.VMEM((k, bn), BF16), pltpu.VMEM((k, bn), BF16)],
        compiler_params=_cparams("arbitrary", "arbitrary"),
        name="in_proj",
    )(hp, hs, w)


def _mixa_prompt_kernel(cb_ref, cc_ref, ch_ref, w_ref, a_ref, st_ref, ubuf_ref):
    u = cc_ref[...].astype(F32) * ch_ref[...].astype(F32)
    ubuf_ref[0:SUBLANES, :] = jnp.zeros((SUBLANES, u.shape[1]), F32)
    ubuf_ref[SUBLANES:SUBLANES + SEQ, :] = u
    u1 = ubuf_ref[SUBLANES - 1:SUBLANES - 1 + SEQ, :]
    u2 = ubuf_ref[SUBLANES - 2:SUBLANES - 2 + SEQ, :]
    z = w_ref[0:1, :] * u2 + w_ref[1:2, :] * u1 + w_ref[2:3, :] * u
    a_ref[...] = (cb_ref[...].astype(F32) * z).astype(a_ref.dtype)
    st_ref[0] = u[SEQ - (CONV_K - 1):, :]


def _mixer_a_prompt(proj_p, conv_w, bc):
    nb = CONV_W // bc
    return pl.pallas_call(
        _mixa_prompt_kernel,
        grid=(BATCH, nb),
        in_specs=[pl.BlockSpec((SEQ, bc), lambda b, j: (b, OFF_CB // bc + j)),
                  pl.BlockSpec((SEQ, bc), lambda b, j: (b, OFF_CC // bc + j)),
                  pl.BlockSpec((SEQ, bc), lambda b, j: (b, OFF_CH // bc + j)),
                  pl.BlockSpec((CONV_K, bc), lambda b, j: (0, j))],
        out_specs=[pl.BlockSpec((SEQ, bc), lambda b, j: (b, j)),
                   pl.BlockSpec((1, CONV_K - 1, bc), lambda b, j: (b, 0, j))],
        out_shape=[jax.ShapeDtypeStruct((M_PROMPT, CONV_W), BF16),
                   jax.ShapeDtypeStruct((BATCH, CONV_K - 1, CONV_W), F32)],
        scratch_shapes=[pltpu.VMEM((SUBLANES + SEQ, bc), F32)],
        compiler_params=_cparams("parallel", "parallel"),
        name="mixer_a_prompt",
    )(proj_p, proj_p, proj_p, conv_w)


def _mixa_sample_kernel(cb_ref, cc_ref, ch_ref, w_ref, s0_ref, s1_ref, a_ref, u_ref):
    u = cc_ref[...].astype(F32) * ch_ref[...].astype(F32)
    z = w_ref[0:1, :] * s0_ref[...] + w_ref[1:2, :] * s1_ref[...] + w_ref[2:3, :] * u
    a_ref[...] = (cb_ref[...].astype(F32) * z).astype(a_ref.dtype)
    u_ref[...] = u


def _mixer_a_sample(proj_s, conv_w, s0, s1, bc):
    nb = CONV_W // bc
    ms = proj_s.shape[0]
    return pl.pallas_call(
        _mixa_sample_kernel,
        grid=(nb,),
        in_specs=[pl.BlockSpec((ms, bc), lambda j: (0, OFF_CB // bc + j)),
                  pl.BlockSpec((ms, bc), lambda j: (0, OFF_CC // bc + j)),
                  pl.BlockSpec((ms, bc), lambda j: (0, OFF_CH // bc + j)),
                  pl.BlockSpec((CONV_K, bc), lambda j: (0, j)),
                  pl.BlockSpec((ms, bc), lambda j: (0, j)),
                  pl.BlockSpec((ms, bc), lambda j: (0, j))],
        out_specs=[pl.BlockSpec((ms, bc), lambda j: (0, j)),
                   pl.BlockSpec((ms, bc), lambda j: (0, j))],
        out_shape=[jax.ShapeDtypeStruct((ms, CONV_W), BF16),
                   jax.ShapeDtypeStruct((ms, CONV_W), F32)],
        compiler_params=_cparams("parallel"),
        name="mixer_a_sample",
    )(proj_s, proj_s, proj_s, conv_w, s0, s1)


def _bucket_table():
    qi = np.arange(Q_BLOCK)[None, :]
    kj = np.arange(2 * Q_BLOCK)[:, None]
    dist = qi + Q_BLOCK - kj
    max_exact = N_BUCKETS // 2
    d = np.maximum(dist, 0)
    df = np.maximum(d, 1).astype(np.float32)
    large = max_exact + (np.log(df / np.float32(max_exact))
                         / np.float32(math.log(MAX_DISTANCE / max_exact))
                         * np.float32(N_BUCKETS - max_exact)).astype(np.int32)
    large = np.minimum(large, N_BUCKETS - 1)
    bucket = np.where(d < max_exact, d, large).astype(np.int32)
    valid = ((dist >= 0) & (dist <= WINDOW)).astype(np.int32)
    return bucket, valid


def _bias_kernel(rb_ref, bucket_ref, valid_ref, o_ref):
    bucket = bucket_ref[...]
    row = lax.broadcasted_iota(jnp.int32, bucket.shape, 0)
    keep = (valid_ref[...] != 0) & ((pl.program_id(0) == 0) | (row >= Q_BLOCK))

    def head(h, carry):
        acc = jnp.zeros(bucket.shape, F32)
        for b in range(N_BUCKETS):
            acc = jnp.where(bucket == b, rb_ref[b, h], acc)
        o_ref[0, h] = jnp.where(keep, acc, NEG)
        return carry

    lax.fori_loop(0, N_HEADS, head, 0)


def _bias_table(rel_bias):
    bucket, valid = _bucket_table()
    shp = (2 * Q_BLOCK, Q_BLOCK)
    return pl.pallas_call(
        _bias_kernel,
        grid=(2,),
        in_specs=[pl.BlockSpec(memory_space=pltpu.SMEM),
                  pl.BlockSpec(shp, lambda v: (0, 0)),
                  pl.BlockSpec(shp, lambda v: (0, 0))],
        out_specs=pl.BlockSpec((1, N_HEADS) + shp, lambda v: (v, 0, 0, 0)),
        out_shape=jax.ShapeDtypeStruct((2, N_HEADS) + shp, F32),
        compiler_params=_cparams("arbitrary"),
        name="rel_bias_table",
    )(rel_bias, jnp.asarray(bucket), jnp.asarray(valid))


def _attn_prompt_kernel(sink_ref, q_ref, kp_ref, kc_ref, vp_ref, vc_ref, bias_ref, o_ref):
    scale = HEAD_DIM ** -0.5
    for kv in range(N_KV_HEADS):
        ks = slice(kv * HEAD_DIM, (kv + 1) * HEAD_DIM)
        heads = range(kv * GROUP, (kv + 1) * GROUP)
        kband = jnp.concatenate([kp_ref[:, ks], kc_ref[:, ks]], axis=0)
        vband = jnp.concatenate([vp_ref[:, ks], vc_ref[:, ks]], axis=0)
        qg = jnp.concatenate(
            [q_ref[:, h * HEAD_DIM:(h + 1) * HEAD_DIM] for h in heads], axis=0) * scale
        st = lax.dot_general(kband, qg, (((1,), (1,)), ((), ())),
                             preferred_element_type=F32)
        st = st + jnp.concatenate([bias_ref[0, h] for h in heads], axis=1)
        sk = jnp.concatenate([jnp.full((1, Q_BLOCK), sink_ref[h], F32) for h in heads], axis=1)
        m = jnp.maximum(jnp.max(st, axis=0, keepdims=True), sk)
        e = jnp.exp(st - m)
        denom = jnp.sum(e, axis=0, keepdims=True) + jnp.exp(sk - m)
        pt = (e * (1.0 / denom)).astype(BF16)
        ot = lax.dot_general(vband, pt, (((0,), (0,)), ((), ())),
                             preferred_element_type=F32)
        o2 = jnp.concatenate(
            [ot[:, g * Q_BLOCK:(g + 1) * Q_BLOCK] for g in range(GROUP)], axis=0)
        gs = slice(kv * GROUP * HEAD_DIM, (kv + 1) * GROUP * HEAD_DIM)
        o_ref[:, gs] = o2.T.astype(o_ref.dtype)


def _attn_prompt(proj_p, bias_tab, sinks):
    nblk = SEQ // Q_BLOCK
    qcol = OFF_Q // ATTN_W
    kcol = OFF_K // KV_W
    vcol = OFF_V // KV_W

    def cur(b, n):
        return b * nblk + n

    def prev(b, n):
        return b * nblk + jnp.maximum(n - 1, 0)

    return pl.pallas_call(
        _attn_prompt_kernel,
        grid=(BATCH, nblk),
        in_specs=[pl.BlockSpec(memory_space=pltpu.SMEM),
                  pl.BlockSpec((Q_BLOCK, ATTN_W), lambda b, n: (cur(b, n), qcol)),
                  pl.BlockSpec((Q_BLOCK, KV_W), lambda b, n: (prev(b, n), kcol)),
                  pl.BlockSpec((Q_BLOCK, KV_W), lambda b, n: (cur(b, n), kcol)),
                  pl.BlockSpec((Q_BLOCK, KV_W), lambda b, n: (prev(b, n), vcol)),
                  pl.BlockSpec((Q_BLOCK, KV_W), lambda b, n: (cur(b, n), vcol)),
                  pl.BlockSpec((1, N_HEADS, 2 * Q_BLOCK, Q_BLOCK),
                               lambda b, n: (jnp.where(n == 0, 1, 0), 0, 0, 0))],
        out_specs=pl.BlockSpec((Q_BLOCK, ATTN_W), lambda b, n: (cur(b, n), 0)),
        out_shape=jax.ShapeDtypeStruct((M_PROMPT, ATTN_W), BF16),
        compiler_params=_cparams("parallel", "arbitrary"),
        name="attn_prompt",
    )(sinks, proj_p, proj_p, proj_p, proj_p, proj_p, bias_tab)


SAMPLES_PER_STEP = 8


def _attn_sample_kernel(q_ref, kn_ref, vn_ref, kb_ref, vb_ref, bias_ref, sink_ref,
                        o_ref, ko_ref, vo_ref):
    g = SAMPLES_PER_STEP
    scale = HEAD_DIM ** -0.5
    erow = lax.broadcasted_iota(jnp.int32, (HEAD_DIM, KV_W), 0)
    ecol = lax.broadcasted_iota(jnp.int32, (HEAD_DIM, KV_W), 1)
    expand = (ecol % HEAD_DIM == erow).astype(BF16)
    hrow = lax.broadcasted_iota(jnp.int32, (N_HEADS, KV_W), 0)
    hcol = lax.broadcasted_iota(jnp.int32, (N_HEADS, KV_W), 1)
    own = ((hrow // GROUP) == (hcol // HEAD_DIM))[None]
    bias_w = bias_ref[:, 0:WINDOW][None]
    bias_n = bias_ref[:, WINDOW:WINDOW + 1][None]
    sk = sink_ref[...][None]
    contract_last = (((1,), (1,)), ((), ()))

    q_all = q_ref[...].reshape(g * N_HEADS, HEAD_DIM) * scale
    qrow = jnp.where(own, _dot(q_all, expand).reshape(g, N_HEADS, KV_W), 0.0)
    qrow_b = qrow.astype(BF16)
    kn = kn_ref[...][:, None, :]
    vn = vn_ref[...][:, None, :]
    s_w = jnp.stack([lax.dot_general(qrow_b[b], kb_ref[b].astype(BF16), contract_last,
                                     preferred_element_type=F32) for b in range(g)])
    s_w = s_w + bias_w
    s_n = jnp.sum(qrow * kn, axis=-1, keepdims=True) + bias_n
    m = jnp.maximum(jnp.maximum(jnp.max(s_w, axis=-1, keepdims=True), s_n), sk)
    e_w = jnp.exp(s_w - m)
    e_n = jnp.exp(s_n - m)
    r = 1.0 / (jnp.sum(e_w, axis=-1, keepdims=True) + e_n + jnp.exp(sk - m))
    p_w = (e_w * r).astype(BF16)
    o_all = jnp.stack([_dot(p_w[b], vb_ref[b].astype(BF16)) for b in range(g)])
    o_own = jnp.where(own, o_all + (e_n * r) * vn, 0.0)
    o = o_own[:, :, 0:HEAD_DIM]
    for kv in range(1, N_KV_HEADS):
        o = o + o_own[:, :, kv * HEAD_DIM:(kv + 1) * HEAD_DIM]
    o_ref[...] = o.astype(o_ref.dtype)
    for b in range(g):
        ko_ref[b] = jnp.concatenate([kb_ref[b, 1:, :], kn_ref[b:b + 1, :]], axis=0)
        vo_ref[b] = jnp.concatenate([vb_ref[b, 1:, :], vn_ref[b:b + 1, :]], axis=0)


def _attn_sample(q3, k_new, v_new, k_buf, v_buf, bias_s, sinks):
    g = SAMPLES_PER_STEP
    nb = DEC_BATCH // g
    win = pl.BlockSpec((g, WINDOW, KV_W), lambda i: (i, 0, 0))
    row = pl.BlockSpec((g, KV_W), lambda i: (i, 0))
    return pl.pallas_call(
        _attn_sample_kernel,
        grid=(nb,),
        in_specs=[pl.BlockSpec((g, N_HEADS, HEAD_DIM), lambda i: (i, 0, 0)),
                  row, row, win, win,
                  pl.BlockSpec((N_HEADS, WINDOW + 1), lambda i: (0, 0)),
                  pl.BlockSpec((N_HEADS, 1), lambda i: (0, 0))],
        out_specs=[pl.BlockSpec((g, N_HEADS, HEAD_DIM), lambda i: (i, 0, 0)), win, win],
        out_shape=[jax.ShapeDtypeStruct((DEC_BATCH, N_HEADS, HEAD_DIM), BF16),
                   jax.ShapeDtypeStruct((DEC_BATCH, WINDOW, KV_W), F32),
                   jax.ShapeDtypeStruct((DEC_BATCH, WINDOW, KV_W), F32)],
        compiler_params=_cparams("parallel"),
        name="attn_sample",
    )(q3, k_new, v_new, k_buf, v_buf, bias_s, sinks.reshape(N_HEADS, 1))


def _merge_kernel(ap_ref, bp_ref, as_ref, bs_ref, waf_ref, wbf_ref,
                  gap_ref, gbp_ref, gas_ref, gbs_ref, op_ref, os_ref,
                  wa0_ref, wb0_ref, wa1_ref, wb1_ref):
    def compute(slot):
        wa_ref, wb_ref = slot

        def merged(a_ref, b_ref, ga_ref, gb_ref):
            return (_sigmoid(ga_ref[...].astype(F32)) * _dot(a_ref[...], wa_ref[...])
                    + _sigmoid(gb_ref[...].astype(F32)) * _dot(b_ref[...], wb_ref[...]))

        op_ref[...] = merged(ap_ref, bp_ref, gap_ref, gbp_ref).astype(op_ref.dtype)

        @pl.when(_is_last_row_tile())
        def _():
            os_ref[...] = merged(as_ref, bs_ref, gas_ref, gbs_ref).astype(os_ref.dtype)

    _ws_phases([waf_ref, wbf_ref], [wa0_ref, wb0_ref], [wa1_ref, wb1_ref], compute)


def _branch_merge(a_p, o_p, a_s, o_s, wa, wb, proj_p, proj_s, bm, bn):
    mp, k = a_p.shape
    ms = a_s.shape[0]
    n = wa.shape[1]
    grid = _ws_grid(n, bn, mp, bm)
    ga0 = OFF_GA // bn
    gb0 = OFF_GB // bn
    wchunk = pl.BlockSpec((k // grid[1], bn), _ws_wchunk(grid[0] - 1))
    wslot = pltpu.VMEM((k, bn), BF16)
    return pl.pallas_call(
        _merge_kernel,
        grid=grid,
        in_specs=[pl.BlockSpec((bm, k), _ws_rows),
                  pl.BlockSpec((bm, k), _ws_rows),
                  pl.BlockSpec((ms, k), lambda j, i: (0, 0)),
                  pl.BlockSpec((ms, k), lambda j, i: (0, 0)),
                  wchunk, wchunk,
                  pl.BlockSpec((bm, bn), _ws_tile(ga0)),
                  pl.BlockSpec((bm, bn), _ws_tile(gb0)),
                  pl.BlockSpec((ms, bn), _ws_stile(ga0)),
                  pl.BlockSpec((ms, bn), _ws_stile(gb0))],
        out_specs=[pl.BlockSpec((bm, bn), _ws_tile()),
                   pl.BlockSpec((ms, bn), _ws_stile())],
        out_shape=[jax.ShapeDtypeStruct((mp, n), BF16),
                   jax.ShapeDtypeStruct((ms, n), BF16)],
        scratch_shapes=[wslot, wslot, wslot, wslot],
        compiler_params=_cparams("arbitrary", "arbitrary"),
        name="branch_merge",
    )(a_p, o_p, a_s, o_s, wa, wb, proj_p, proj_p, proj_s, proj_s)


def _resid_kernel(xp_ref, xs_ref, wf_ref, rp_ref, rs_ref, op_ref, os_ref, w0_ref, w1_ref):
    def compute(slot):
        (w_ref,) = slot
        op_ref[...] = rp_ref[...] + _dot(xp_ref[...], w_ref[...])

        @pl.when(_is_last_row_tile())
        def _():
            os_ref[...] = rs_ref[...] + _dot(xs_ref[...], w_ref[...])

    _ws_phases([wf_ref], [w0_ref], [w1_ref], compute)


def _resid_matmul(xp, xs, w, rp, rs, bm, bn, name):
    mp, k = xp.shape
    ms = xs.shape[0]
    n = w.shape[1]
    grid = _ws_grid(n, bn, mp, bm)
    return pl.pallas_call(
        _resid_kernel,
        grid=grid,
        in_specs=[pl.BlockSpec((bm, k), _ws_rows),
                  pl.BlockSpec((ms, k), lambda j, i: (0, 0)),
                  pl.BlockSpec((k // grid[1], bn), _ws_wchunk(grid[0] - 1)),
                  pl.BlockSpec((bm, bn), _ws_tile()),
                  pl.BlockSpec((ms, bn), _ws_stile())],
        out_specs=[pl.BlockSpec((bm, bn), _ws_tile()),
                   pl.BlockSpec((ms, bn), _ws_stile())],
        out_shape=[jax.ShapeDtypeStruct((mp, n), F32),
                   jax.ShapeDtypeStruct((ms, n), F32)],
        scratch_shapes=[pltpu.VMEM((k, bn), BF16), pltpu.VMEM((k, bn), BF16)],
        compiler_params=_cparams("arbitrary", "arbitrary"),
        name=name,
    )(xp, xs, w, rp, rs)


def _ffn_up_kernel(hp_ref, hs_ref, wgf_ref, wuf_ref, cw_ref, cb_ref, s0_ref, s1_ref,
                   fp_ref, fs_ref, gs_ref, st_ref,
                   wg0_ref, wu0_ref, wg1_ref, wu1_ref, carry_ref, gbuf_ref, *, tiles_per_seq):
    i = pl.program_id(1)
    bm = hp_ref.shape[0]

    def compute(slot):
        wg_ref, wu_ref = slot
        w0 = cw_ref[0:1, :]
        w1 = cw_ref[1:2, :]
        w2 = cw_ref[2:3, :]
        bias = cb_ref[...]

        @pl.when(i % tiles_per_seq == 0)
        def _():
            carry_ref[...] = jnp.zeros(carry_ref.shape, F32)

        g = _dot(hp_ref[...], wg_ref[...])
        gbuf_ref[0:SUBLANES, :] = carry_ref[...]
        gbuf_ref[SUBLANES:SUBLANES + bm, :] = g
        g1 = gbuf_ref[SUBLANES - 1:SUBLANES - 1 + bm, :]
        g2 = gbuf_ref[SUBLANES - 2:SUBLANES - 2 + bm, :]
        gc = w0 * g2 + w1 * g1 + w2 * g + bias
        up = _dot(hp_ref[...], wu_ref[...])
        fp_ref[...] = (gc * _sigmoid(gc) * up).astype(fp_ref.dtype)
        tail = gbuf_ref[bm:SUBLANES + bm, :]
        carry_ref[...] = tail
        st_ref[0] = tail

        @pl.when(_is_last_row_tile())
        def _():
            gsm = _dot(hs_ref[...], wg_ref[...])
            ups = _dot(hs_ref[...], wu_ref[...])
            gcs = w0 * s0_ref[...] + w1 * s1_ref[...] + w2 * gsm + bias
            fs_ref[...] = (gcs * _sigmoid(gcs) * ups).astype(fs_ref.dtype)
            gs_ref[...] = gsm

    _ws_phases([wgf_ref, wuf_ref], [wg0_ref, wu0_ref], [wg1_ref, wu1_ref], compute)


def _ffn_up(hp, hs, wg, wu, cw, cb, s0, s1, bm, bn):
    mp, k = hp.shape
    ms = hs.shape[0]
    n = wg.shape[1]
    grid = _ws_grid(n, bn, mp, bm)
    tiles_per_seq = SEQ // bm
    wchunk = pl.BlockSpec((k // grid[1], bn), _ws_wchunk(grid[0] - 1))
    wslot = pltpu.VMEM((k, bn), BF16)
    col = _ws_stile()

    def tail_map(j, i):
        return (jnp.where(j > 0, i, 0) // tiles_per_seq, 0, jnp.maximum(j - 1, 0))

    return pl.pallas_call(
        functools.partial(_ffn_up_kernel, tiles_per_seq=tiles_per_seq),
        grid=grid,
        in_specs=[pl.BlockSpec((bm, k), _ws_rows),
                  pl.BlockSpec((ms, k), lambda j, i: (0, 0)),
                  wchunk, wchunk,
                  pl.BlockSpec((FFN_CONV_K, bn), col),
                  pl.BlockSpec((1, bn), col),
                  pl.BlockSpec((ms, bn), col),
                  pl.BlockSpec((ms, bn), col)],
        out_specs=[pl.BlockSpec((bm, bn), _ws_tile()),
                   pl.BlockSpec((ms, bn), col),
                   pl.BlockSpec((ms, bn), col),
                   pl.BlockSpec((1, SUBLANES, bn), tail_map)],
        out_shape=[jax.ShapeDtypeStruct((mp, n), BF16),
                   jax.ShapeDtypeStruct((ms, n), BF16),
                   jax.ShapeDtypeStruct((ms, n), F32),
                   jax.ShapeDtypeStruct((BATCH, SUBLANES, n), F32)],
        scratch_shapes=[wslot, wslot, wslot, wslot,
                        pltpu.VMEM((SUBLANES, bn), F32),
                        pltpu.VMEM((SUBLANES + bm, bn), F32)],
        compiler_params=_cparams("arbitrary", "arbitrary"),
        name="ffn_up",
    )(hp, hs, wg, wu, cw, cb, s0, s1)


def kernel(x_prompt, x_sample, state_k_window, state_v_window, state_conv, state_ffn_conv,
           attn_norm_g, w_in, conv_w, w_branch_a, w_branch_b, sinks, w_out, ffn_norm_g,
           w_ffn_gate, w_ffn_up, ffn_conv_w, ffn_conv_b, w_ffn_down, rel_bias,
           final_norm_g):
    xp = x_prompt.reshape(M_PROMPT, D_MODEL)
    xs = x_sample.reshape(DEC_BATCH, D_MODEL)

    fs0 = state_ffn_conv[0][:, 0, :]
    fs1 = state_ffn_conv[0][:, 1, :]

    hp = _rmsnorm(xp, attn_norm_g[0], BF16, 256)
    hs = _rmsnorm(xs, attn_norm_g[0], BF16, DEC_BATCH)
    proj_p, proj_s = _in_proj(hp, hs, w_in[0], 1024, 1024)

    a_p, conv_p = _mixer_a_prompt(proj_p, conv_w[0], 256)
    cs0 = state_conv[0][:, 0, :]
    cs1 = state_conv[0][:, 1, :]
    a_s, u_s = _mixer_a_sample(proj_s, conv_w[0], cs0, cs1, 512)

    bias_tab = _bias_table(rel_bias)
    o_p = _attn_prompt(proj_p, bias_tab, sinks[0])
    q3 = proj_s[:, OFF_Q:OFF_Q + ATTN_W].reshape(DEC_BATCH, N_HEADS, HEAD_DIM)
    k_new = proj_s[:, OFF_K:OFF_K + KV_W].astype(F32)
    v_new = proj_s[:, OFF_V:OFF_V + KV_W].astype(F32)
    bias_s = bias_tab[0, :, Q_BLOCK - 1:, Q_BLOCK - 1]
    o_s3, k_win_s, v_win_s = _attn_sample(
        q3, k_new, v_new,
        state_k_window[0].reshape(DEC_BATCH, WINDOW, KV_W),
        state_v_window[0].reshape(DEC_BATCH, WINDOW, KV_W), bias_s, sinks[0])
    o_s = o_s3.reshape(DEC_BATCH, ATTN_W)

    mg_p, mg_s = _branch_merge(a_p, o_p, a_s, o_s, w_branch_a[0], w_branch_b[0],
                               proj_p, proj_s, 1024, 512)
    x1p, x1s = _resid_matmul(mg_p, mg_s, w_out[0], xp, xs, 1024, 512, "out_proj")

    h2p = _rmsnorm(x1p, ffn_norm_g[0], BF16, 256)
    h2s = _rmsnorm(x1s, ffn_norm_g[0], BF16, DEC_BATCH)
    f_p, f_s, g_s, g_tail = _ffn_up(h2p, h2s, w_ffn_gate[0], w_ffn_up[0], ffn_conv_w[0],
                                    ffn_conv_b, fs0, fs1, 1024, 512)
    x2p, x2s = _resid_matmul(f_p, f_s, w_ffn_down[0], x1p, x1s, 512, 512, "ffn_down")
    y_p = _rmsnorm(x2p, final_norm_g, F32, 256)
    y_s = _rmsnorm(x2s, final_norm_g, F32, DEC_BATCH)

    kv_p = proj_p.reshape(BATCH, SEQ, IN_W)[:, SEQ - WINDOW:, OFF_K:OFF_GA].astype(F32)
    k_win_p = kv_p[:, :, :KV_W].reshape(1, BATCH, WINDOW, N_KV_HEADS, HEAD_DIM)
    v_win_p = kv_p[:, :, KV_W:].reshape(1, BATCH, WINDOW, N_KV_HEADS, HEAD_DIM)
    ffn_p = g_tail[:, SUBLANES - (FFN_CONV_K - 1):, :]
    conv_s = jnp.stack([cs1, u_s], axis=1)
    ffn_s = jnp.stack([fs1, g_s], axis=1)

    return (y_p.reshape(BATCH, SEQ, D_MODEL),
            y_s.reshape(DEC_BATCH, 1, D_MODEL),
            k_win_p, v_win_p, conv_p[None], ffn_p[None],
            k_win_s.reshape(1, DEC_BATCH, WINDOW, N_KV_HEADS, HEAD_DIM),
            v_win_s.reshape(1, DEC_BATCH, WINDOW, N_KV_HEADS, HEAD_DIM),
            conv_s[None], ffn_s[None])
```

```python
import functools
import math

import numpy as np
import jax
import jax.numpy as jnp
from jax import lax
from jax.experimental import pallas as pl
from jax.experimental.pallas import tpu as pltpu

F32 = jnp.float32
BF16 = jnp.bfloat16

D_MODEL = 4096
BATCH = 4
SEQ = 2048
DEC_BATCH = 128
N_HEADS = 32
N_KV_HEADS = 8
HEAD_DIM = 64
GROUP = N_HEADS // N_KV_HEADS
ATTN_W = N_HEADS * HEAD_DIM
KV_W = N_KV_HEADS * HEAD_DIM
CONV_W = D_MODEL // 2
CONV_K = 3
WINDOW = 128
Q_BLOCK = 128
N_BUCKETS = 32
MAX_DISTANCE = 128
D_FF = 11008
FFN_CONV_K = 3
EPS = 1e-5
NEG = -1e30
IN_W = 3 * CONV_W + ATTN_W + 2 * KV_W + 2 * D_MODEL
M_PROMPT = BATCH * SEQ

W_CB = 0
W_CC = CONV_W
W_CH = 2 * CONV_W
W_REST = 3 * CONV_W
OFF_Q = 0
OFF_K = OFF_Q + ATTN_W
OFF_V = OFF_K + KV_W
OFF_GA = OFF_V + KV_W
OFF_GB = OFF_GA + D_MODEL
PROJ_W = OFF_GB + D_MODEL

V7X_VMEM_BYTES = 64 * 1024 * 1024
VMEM_LIMIT = V7X_VMEM_BYTES - 6 * 1024 * 1024
SUBLANES = 8


def _cparams(*sem):
    return pltpu.CompilerParams(dimension_semantics=sem, vmem_limit_bytes=VMEM_LIMIT)


def _dot(a, b):
    return jnp.dot(a, b, preferred_element_type=F32)


def _sigmoid(x):
    return 1.0 / (1.0 + jnp.exp(-x))


def _ws_grid(n, bn, mp, bm):
    return (pl.cdiv(n, bn) + 1, mp // bm)


def _ws_rows(j, i):
    return (jnp.where(j > 0, i, 0), 0)


def _ws_wchunk(nj, col0=0):
    return lambda j, i: (i, col0 + jnp.minimum(j, nj - 1))


def _ws_tile(col0=0):
    return lambda j, i: (jnp.where(j > 0, i, 0), col0 + jnp.maximum(j - 1, 0))


def _ws_stile(col0=0):
    return lambda j, i: (0, col0 + jnp.maximum(j - 1, 0))


def _ws_cast(wf_refs, dst_refs):
    i = pl.program_id(1)
    for wf_ref, dst_ref in zip(wf_refs, dst_refs):
        rows = wf_ref.shape[0]
        r0 = pl.multiple_of(i * rows, rows)
        dst_ref[pl.ds(r0, rows), :] = wf_ref[...].astype(BF16)


def _ws_phases(wf_refs, slot0, slot1, compute):
    j = pl.program_id(0)

    @pl.when(j == 0)
    def _():
        _ws_cast(wf_refs, slot0)

    @pl.when((j > 0) & (j % 2 == 1))
    def _():
        _ws_cast(wf_refs, slot1)
        compute(slot0)

    @pl.when((j > 0) & (j % 2 == 0))
    def _():
        _ws_cast(wf_refs, slot0)
        compute(slot1)


def _is_last_row_tile():
    return pl.program_id(1) == pl.num_programs(1) - 1


def _rms_kernel(x_ref, g_ref, o_ref):
    x = x_ref[...]
    r = lax.rsqrt(jnp.mean(x * x, axis=-1, keepdims=True) + EPS)
    o_ref[...] = ((x * r) * g_ref[...]).astype(o_ref.dtype)


def _rmsnorm(x, g, out_dtype, block_rows):
    rows, d = x.shape
    return pl.pallas_call(
        _rms_kernel,
        grid=(rows // block_rows,),
        in_specs=[pl.BlockSpec((block_rows, d), lambda i: (i, 0)),
                  pl.BlockSpec((1, d), lambda i: (0, 0))],
        out_specs=pl.BlockSpec((block_rows, d), lambda i: (i, 0)),
        out_shape=jax.ShapeDtypeStruct((rows, d), out_dtype),
        compiler_params=_cparams("parallel"),
        name="rmsnorm",
    )(x, g.reshape(1, d))


def _proj_kernel(xp_ref, xs_ref, wf_ref, op_ref, os_ref, w0_ref, w1_ref):
    def compute(slot):
        (w_ref,) = slot
        op_ref[...] = _dot(xp_ref[...], w_ref[...]).astype(op_ref.dtype)

        @pl.when(_is_last_row_tile())
        def _():
            os_ref[...] = _dot(xs_ref[...], w_ref[...]).astype(os_ref.dtype)

    _ws_phases([wf_ref], [w0_ref], [w1_ref], compute)


def _in_proj(hp, hs, w, col0, n, bm, bn):
    mp, k = hp.shape
    ms = hs.shape[0]
    grid = _ws_grid(n, bn, mp, bm)
    return pl.pallas_call(
        _proj_kernel,
        grid=grid,
        in_specs=[pl.BlockSpec((bm, k), _ws_rows),
                  pl.BlockSpec((ms, k), lambda j, i: (0, 0)),
                  pl.BlockSpec((k // grid[1], bn), _ws_wchunk(grid[0] - 1, col0 // bn))],
        out_specs=[pl.BlockSpec((bm, bn), _ws_tile()),
                   pl.BlockSpec((ms, bn), _ws_stile())],
        out_shape=[jax.ShapeDtypeStruct((mp, n), BF16),
                   jax.ShapeDtypeStruct((ms, n), BF16)],
        scratch_shapes=[pltpu.VMEM((k, bn), BF16), pltpu.VMEM((k, bn), BF16)],
        compiler_params=_cparams("arbitrary", "arbitrary"),
        name="in_proj",
    )(hp, hs, w)


def _mixa_kernel(hp_ref, hs_ref, wbf_ref, wcf_ref, whf_ref, cw_ref, s0_ref, s1_ref,
                 ap_ref, as_ref, us_ref, st_ref,
                 wb0_ref, wc0_ref, wh0_ref, wb1_ref, wc1_ref, wh1_ref, carry_ref, ubuf_ref,
                 *, tiles_per_seq):
    i = pl.program_id(1)
    bm = hp_ref.shape[0]

    def compute(slot):
        wb_ref, wc_ref, wh_ref = slot
        w0 = cw_ref[0:1, :]
        w1 = cw_ref[1:2, :]
        w2 = cw_ref[2:3, :]

        @pl.when(i % tiles_per_seq == 0)
        def _():
            carry_ref[...] = jnp.zeros(carry_ref.shape, F32)

        u = _dot(hp_ref[...], wc_ref[...]) * _dot(hp_ref[...], wh_ref[...])
        ubuf_ref[0:SUBLANES, :] = carry_ref[...]
        ubuf_ref[SUBLANES:SUBLANES + bm, :] = u
        u1 = ubuf_ref[SUBLANES - 1:SUBLANES - 1 + bm, :]
        u2 = ubuf_ref[SUBLANES - 2:SUBLANES - 2 + bm, :]
        z = w0 * u2 + w1 * u1 + w2 * u
        ap_ref[...] = (_dot(hp_ref[...], wb_ref[...]) * z).astype(ap_ref.dtype)
        tail = ubuf_ref[bm:SUBLANES + bm, :]
        carry_ref[...] = tail
        st_ref[0] = tail

        @pl.when(_is_last_row_tile())
        def _():
            us = _dot(hs_ref[...], wc_ref[...]) * _dot(hs_ref[...], wh_ref[...])
            zs = w0 * s0_ref[...] + w1 * s1_ref[...] + w2 * us
            as_ref[...] = (_dot(hs_ref[...], wb_ref[...]) * zs).astype(as_ref.dtype)
            us_ref[...] = us

    _ws_phases([wbf_ref, wcf_ref, whf_ref], [wb0_ref, wc0_ref, wh0_ref],
               [wb1_ref, wc1_ref, wh1_ref], compute)


def _mixer_a(hp, hs, w_in, conv_w, s0, s1, bm, bn):
    mp, k = hp.shape
    ms = hs.shape[0]
    n = CONV_W
    grid = _ws_grid(n, bn, mp, bm)
    nj = grid[0] - 1
    tiles_per_seq = SEQ // bm
    wslot = pltpu.VMEM((k, bn), BF16)
    col = _ws_stile()

    def wchunk(col0):
        return pl.BlockSpec((k // grid[1], bn), _ws_wchunk(nj, col0 // bn))

    def tail_map(j, i):
        return (jnp.where(j > 0, i, 0) // tiles_per_seq, 0, jnp.maximum(j - 1, 0))

    return pl.pallas_call(
        functools.partial(_mixa_kernel, tiles_per_seq=tiles_per_seq),
        grid=grid,
        in_specs=[pl.BlockSpec((bm, k), _ws_rows),
                  pl.BlockSpec((ms, k), lambda j, i: (0, 0)),
                  wchunk(W_CB), wchunk(W_CC), wchunk(W_CH),
                  pl.BlockSpec((CONV_K, bn), col),
                  pl.BlockSpec((ms, bn), col),
                  pl.BlockSpec((ms, bn), col)],
        out_specs=[pl.BlockSpec((bm, bn), _ws_tile()),
                   pl.BlockSpec((ms, bn), col),
                   pl.BlockSpec((ms, bn), col),
                   pl.BlockSpec((1, SUBLANES, bn), tail_map)],
        out_shape=[jax.ShapeDtypeStruct((mp, n), BF16),
                   jax.ShapeDtypeStruct((ms, n), BF16),
                   jax.ShapeDtypeStruct((ms, n), F32),
                   jax.ShapeDtypeStruct((BATCH, SUBLANES, n), F32)],
        scratch_shapes=[wslot] * 6 + [pltpu.VMEM((SUBLANES, bn), F32),
                                      pltpu.VMEM((SUBLANES + bm, bn), F32)],
        compiler_params=_cparams("arbitrary", "arbitrary"),
        name="mixer_a",
    )(hp, hs, w_in, w_in, w_in, conv_w, s0, s1)


def _bucket_table():
    qi = np.arange(Q_BLOCK)[None, :]
    kj = np.arange(2 * Q_BLOCK)[:, None]
    dist = qi + Q_BLOCK - kj
    max_exact = N_BUCKETS // 2
    d = np.maximum(dist, 0)
    df = np.maximum(d, 1).astype(np.float32)
    large = max_exact + (np.log(df / np.float32(max_exact))
                         / np.float32(math.log(MAX_DISTANCE / max_exact))
                         * np.float32(N_BUCKETS - max_exact)).astype(np.int32)
    large = np.minimum(large, N_BUCKETS - 1)
    bucket = np.where(d < max_exact, d, large).astype(np.int32)
    valid = ((dist >= 0) & (dist <= WINDOW)).astype(np.int32)
    return bucket, valid


def _bias_kernel(rb_ref, bucket_ref, valid_ref, o_ref):
    bucket = bucket_ref[...]
    row = lax.broadcasted_iota(jnp.int32, bucket.shape, 0)
    keep = (valid_ref[...] != 0) & ((pl.program_id(0) == 0) | (row >= Q_BLOCK))

    def head(h, carry):
        acc = jnp.zeros(bucket.shape, F32)
        for b in range(N_BUCKETS):
            acc = jnp.where(bucket == b, rb_ref[b, h], acc)
        o_ref[0, h] = jnp.where(keep, acc, NEG)
        return carry

    lax.fori_loop(0, N_HEADS, head, 0)


def _bias_table(rel_bias):
    bucket, valid = _bucket_table()
    shp = (2 * Q_BLOCK, Q_BLOCK)
    return pl.pallas_call(
        _bias_kernel,
        grid=(2,),
        in_specs=[pl.BlockSpec(memory_space=pltpu.SMEM),
                  pl.BlockSpec(shp, lambda v: (0, 0)),
                  pl.BlockSpec(shp, lambda v: (0, 0))],
        out_specs=pl.BlockSpec((1, N_HEADS) + shp, lambda v: (v, 0, 0, 0)),
        out_shape=jax.ShapeDtypeStruct((2, N_HEADS) + shp, F32),
        compiler_params=_cparams("arbitrary"),
        name="rel_bias_table",
    )(rel_bias, jnp.asarray(bucket), jnp.asarray(valid))


def _attn_prompt_kernel(sink_ref, q_ref, kp_ref, kc_ref, vp_ref, vc_ref, bias_ref, o_ref):
    scale = HEAD_DIM ** -0.5
    for kv in range(N_KV_HEADS):
        ks = slice(kv * HEAD_DIM, (kv + 1) * HEAD_DIM)
        heads = range(kv * GROUP, (kv + 1) * GROUP)
        kband = jnp.concatenate([kp_ref[:, ks], kc_ref[:, ks]], axis=0)
        vband = jnp.concatenate([vp_ref[:, ks], vc_ref[:, ks]], axis=0)
        qg = jnp.concatenate(
            [q_ref[:, h * HEAD_DIM:(h + 1) * HEAD_DIM] for h in heads], axis=0) * scale
        st = lax.dot_general(kband, qg, (((1,), (1,)), ((), ())),
                             preferred_element_type=F32)
        st = st + jnp.concatenate([bias_ref[0, h] for h in heads], axis=1)
        sk = jnp.concatenate([jnp.full((1, Q_BLOCK), sink_ref[h], F32) for h in heads], axis=1)
        m = jnp.maximum(jnp.max(st, axis=0, keepdims=True), sk)
        e = jnp.exp(st - m)
        denom = jnp.sum(e, axis=0, keepdims=True) + jnp.exp(sk - m)
        pt = (e * (1.0 / denom)).astype(BF16)
        ot = lax.dot_general(vband, pt, (((0,), (0,)), ((), ())),
                             preferred_element_type=F32)
        o2 = jnp.concatenate(
            [ot[:, g * Q_BLOCK:(g + 1) * Q_BLOCK] for g in range(GROUP)], axis=0)
        gs = slice(kv * GROUP * HEAD_DIM, (kv + 1) * GROUP * HEAD_DIM)
        o_ref[:, gs] = o2.T.astype(o_ref.dtype)


def _attn_prompt(proj_p, bias_tab, sinks):
    nblk = SEQ // Q_BLOCK
    qcol = OFF_Q // ATTN_W
    kcol = OFF_K // KV_W
    vcol = OFF_V // KV_W

    def cur(b, n):
        return b * nblk + n

    def prev(b, n):
        return b * nblk + jnp.maximum(n - 1, 0)

    return pl.pallas_call(
        _attn_prompt_kernel,
        grid=(BATCH, nblk),
        in_specs=[pl.BlockSpec(memory_space=pltpu.SMEM),
                  pl.BlockSpec((Q_BLOCK, ATTN_W), lambda b, n: (cur(b, n), qcol)),
                  pl.BlockSpec((Q_BLOCK, KV_W), lambda b, n: (prev(b, n), kcol)),
                  pl.BlockSpec((Q_BLOCK, KV_W), lambda b, n: (cur(b, n), kcol)),
                  pl.BlockSpec((Q_BLOCK, KV_W), lambda b, n: (prev(b, n), vcol)),
                  pl.BlockSpec((Q_BLOCK, KV_W), lambda b, n: (cur(b, n), vcol)),
                  pl.BlockSpec((1, N_HEADS, 2 * Q_BLOCK, Q_BLOCK),
                               lambda b, n: (jnp.where(n == 0, 1, 0), 0, 0, 0))],
        out_specs=pl.BlockSpec((Q_BLOCK, ATTN_W), lambda b, n: (cur(b, n), 0)),
        out_shape=jax.ShapeDtypeStruct((M_PROMPT, ATTN_W), BF16),
        compiler_params=_cparams("parallel", "arbitrary"),
        name="attn_prompt",
    )(sinks, proj_p, proj_p, proj_p, proj_p, proj_p, bias_tab)


SAMPLES_PER_STEP = 8


def _attn_sample_kernel(q_ref, kn_ref, vn_ref, kb_ref, vb_ref, bias_ref, sink_ref,
                        o_ref, ko_ref, vo_ref):
    g = SAMPLES_PER_STEP
    scale = HEAD_DIM ** -0.5
    erow = lax.broadcasted_iota(jnp.int32, (HEAD_DIM, KV_W), 0)
    ecol = lax.broadcasted_iota(jnp.int32, (HEAD_DIM, KV_W), 1)
    expand = (ecol % HEAD_DIM == erow).astype(BF16)
    hrow = lax.broadcasted_iota(jnp.int32, (N_HEADS, KV_W), 0)
    hcol = lax.broadcasted_iota(jnp.int32, (N_HEADS, KV_W), 1)
    own = ((hrow // GROUP) == (hcol // HEAD_DIM))[None]
    bias_w = bias_ref[:, 0:WINDOW][None]
    bias_n = bias_ref[:, WINDOW:WINDOW + 1][None]
    sk = sink_ref[...][None]
    contract_last = (((1,), (1,)), ((), ()))

    q_all = q_ref[...].reshape(g * N_HEADS, HEAD_DIM) * scale
    qrow = jnp.where(own, _dot(q_all, expand).reshape(g, N_HEADS, KV_W), 0.0)
    qrow_b = qrow.astype(BF16)
    kn = kn_ref[...][:, None, :]
    vn = vn_ref[...][:, None, :]
    s_w = jnp.stack([lax.dot_general(qrow_b[b], kb_ref[b].astype(BF16), contract_last,
                                     preferred_element_type=F32) for b in range(g)])
    s_w = s_w + bias_w
    s_n = jnp.sum(qrow * kn, axis=-1, keepdims=True) + bias_n
    m = jnp.maximum(jnp.maximum(jnp.max(s_w, axis=-1, keepdims=True), s_n), sk)
    e_w = jnp.exp(s_w - m)
    e_n = jnp.exp(s_n - m)
    r = 1.0 / (jnp.sum(e_w, axis=-1, keepdims=True) + e_n + jnp.exp(sk - m))
    p_w = (e_w * r).astype(BF16)
    o_all = jnp.stack([_dot(p_w[b], vb_ref[b].astype(BF16)) for b in range(g)])
    o_own = jnp.where(own, o_all + (e_n * r) * vn, 0.0)
    o = o_own[:, :, 0:HEAD_DIM]
    for kv in range(1, N_KV_HEADS):
        o = o + o_own[:, :, kv * HEAD_DIM:(kv + 1) * HEAD_DIM]
    o_ref[...] = o.astype(o_ref.dtype)
    for b in range(g):
        ko_ref[b] = jnp.concatenate([kb_ref[b, 1:, :], kn_ref[b:b + 1, :]], axis=0)
        vo_ref[b] = jnp.concatenate([vb_ref[b, 1:, :], vn_ref[b:b + 1, :]], axis=0)


def _attn_sample(q3, k_new, v_new, k_buf, v_buf, bias_s, sinks):
    g = SAMPLES_PER_STEP
    nb = DEC_BATCH // g
    win = pl.BlockSpec((g, WINDOW, KV_W), lambda i: (i, 0, 0))
    row = pl.BlockSpec((g, KV_W), lambda i: (i, 0))
    return pl.pallas_call(
        _attn_sample_kernel,
        grid=(nb,),
        in_specs=[pl.BlockSpec((g, N_HEADS, HEAD_DIM), lambda i: (i, 0, 0)),
                  row, row, win, win,
                  pl.BlockSpec((N_HEADS, WINDOW + 1), lambda i: (0, 0)),
                  pl.BlockSpec((N_HEADS, 1), lambda i: (0, 0))],
        out_specs=[pl.BlockSpec((g, N_HEADS, HEAD_DIM), lambda i: (i, 0, 0)), win, win],
        out_shape=[jax.ShapeDtypeStruct((DEC_BATCH, N_HEADS, HEAD_DIM), BF16),
                   jax.ShapeDtypeStruct((DEC_BATCH, WINDOW, KV_W), F32),
                   jax.ShapeDtypeStruct((DEC_BATCH, WINDOW, KV_W), F32)],
        compiler_params=_cparams("parallel"),
        name="attn_sample",
    )(q3, k_new, v_new, k_buf, v_buf, bias_s, sinks.reshape(N_HEADS, 1))


def _merge_kernel(ap_ref, bp_ref, as_ref, bs_ref, waf_ref, wbf_ref,
                  gap_ref, gbp_ref, gas_ref, gbs_ref, op_ref, os_ref,
                  wa0_ref, wb0_ref, wa1_ref, wb1_ref):
    def compute(slot):
        wa_ref, wb_ref = slot

        def merged(a_ref, b_ref, ga_ref, gb_ref):
            return (_sigmoid(ga_ref[...].astype(F32)) * _dot(a_ref[...], wa_ref[...])
                    + _sigmoid(gb_ref[...].astype(F32)) * _dot(b_ref[...], wb_ref[...]))

        op_ref[...] = merged(ap_ref, bp_ref, gap_ref, gbp_ref).astype(op_ref.dtype)

        @pl.when(_is_last_row_tile())
        def _():
            os_ref[...] = merged(as_ref, bs_ref, gas_ref, gbs_ref).astype(os_ref.dtype)

    _ws_phases([waf_ref, wbf_ref], [wa0_ref, wb0_ref], [wa1_ref, wb1_ref], compute)


def _branch_merge(a_p, o_p, a_s, o_s, wa, wb, proj_p, proj_s, bm, bn):
    mp, k = a_p.shape
    ms = a_s.shape[0]
    n = wa.shape[1]
    grid = _ws_grid(n, bn, mp, bm)
    ga0 = OFF_GA // bn
    gb0 = OFF_GB // bn
    wchunk = pl.BlockSpec((k // grid[1], bn), _ws_wchunk(grid[0] - 1))
    wslot = pltpu.VMEM((k, bn), BF16)
    return pl.pallas_call(
        _merge_kernel,
        grid=grid,
        in_specs=[pl.BlockSpec((bm, k), _ws_rows),
                  pl.BlockSpec((bm, k), _ws_rows),
                  pl.BlockSpec((ms, k), lambda j, i: (0, 0)),
                  pl.BlockSpec((ms, k), lambda j, i: (0, 0)),
                  wchunk, wchunk,
                  pl.BlockSpec((bm, bn), _ws_tile(ga0)),
                  pl.BlockSpec((bm, bn), _ws_tile(gb0)),
                  pl.BlockSpec((ms, bn), _ws_stile(ga0)),
                  pl.BlockSpec((ms, bn), _ws_stile(gb0))],
        out_specs=[pl.BlockSpec((bm, bn), _ws_tile()),
                   pl.BlockSpec((ms, bn), _ws_stile())],
        out_shape=[jax.ShapeDtypeStruct((mp, n), BF16),
                   jax.ShapeDtypeStruct((ms, n), BF16)],
        scratch_shapes=[wslot, wslot, wslot, wslot],
        compiler_params=_cparams("arbitrary", "arbitrary"),
        name="branch_merge",
    )(a_p, o_p, a_s, o_s, wa, wb, proj_p, proj_p, proj_s, proj_s)


def _resid_kernel(xp_ref, xs_ref, wf_ref, rp_ref, rs_ref, op_ref, os_ref, w0_ref, w1_ref):
    def compute(slot):
        (w_ref,) = slot
        op_ref[...] = rp_ref[...] + _dot(xp_ref[...], w_ref[...])

        @pl.when(_is_last_row_tile())
        def _():
            os_ref[...] = rs_ref[...] + _dot(xs_ref[...], w_ref[...])

    _ws_phases([wf_ref], [w0_ref], [w1_ref], compute)


def _resid_matmul(xp, xs, w, rp, rs, bm, bn, name):
    mp, k = xp.shape
    ms = xs.shape[0]
    n = w.shape[1]
    grid = _ws_grid(n, bn, mp, bm)
    return pl.pallas_call(
        _resid_kernel,
        grid=grid,
        in_specs=[pl.BlockSpec((bm, k), _ws_rows),
                  pl.BlockSpec((ms, k), lambda j, i: (0, 0)),
                  pl.BlockSpec((k // grid[1], bn), _ws_wchunk(grid[0] - 1)),
                  pl.BlockSpec((bm, bn), _ws_tile()),
                  pl.BlockSpec((ms, bn), _ws_stile())],
        out_specs=[pl.BlockSpec((bm, bn), _ws_tile()),
                   pl.BlockSpec((ms, bn), _ws_stile())],
        out_shape=[jax.ShapeDtypeStruct((mp, n), F32),
                   jax.ShapeDtypeStruct((ms, n), F32)],
        scratch_shapes=[pltpu.VMEM((k, bn), BF16), pltpu.VMEM((k, bn), BF16)],
        compiler_params=_cparams("arbitrary", "arbitrary"),
        name=name,
    )(xp, xs, w, rp, rs)


LANES = 128


def _lane_partials(v):
    part = v[:, 0:LANES]
    for c in range(1, v.shape[1] // LANES):
        part = part + v[:, c * LANES:(c + 1) * LANES]
    return part


def _outproj_kernel(xp_ref, xs_ref, wf_ref, rp_ref, rs_ref, g_ref,
                    op_ref, os_ref, bp_ref, bs_ref, ssp_ref, sss_ref,
                    w0_ref, w1_ref, accp_ref, accs_ref):
    j = pl.program_id(0)
    i = pl.program_id(1)
    bm = xp_ref.shape[0]

    def compute(slot):
        (w_ref,) = slot

        def tile(x_ref, r_ref, o_ref, b_ref, acc_view, ss_ref):
            x1 = r_ref[...] + _dot(x_ref[...], w_ref[...])
            o_ref[...] = x1
            b_ref[...] = (x1 * g_ref[...]).astype(b_ref.dtype)
            total = jnp.where(j == 1, 0.0, acc_view[...]) + _lane_partials(x1 * x1)
            acc_view[...] = total
            ss_ref[...] = total

        rows = pl.ds(pl.multiple_of(i * bm, bm), bm)
        tile(xp_ref, rp_ref, op_ref, bp_ref, accp_ref.at[rows, :], ssp_ref)

        @pl.when(_is_last_row_tile())
        def _():
            tile(xs_ref, rs_ref, os_ref, bs_ref, accs_ref, sss_ref)

    _ws_phases([wf_ref], [w0_ref], [w1_ref], compute)


def _out_proj(xp, xs, w, rp, rs, gain, bm, bn):
    mp, k = xp.shape
    ms = xs.shape[0]
    n = w.shape[1]
    grid = _ws_grid(n, bn, mp, bm)
    last = grid[0] - 1
    ssp_map = lambda j, i: (jnp.where(j == last, i, 0), 0)
    return pl.pallas_call(
        _outproj_kernel,
        grid=grid,
        in_specs=[pl.BlockSpec((bm, k), _ws_rows),
                  pl.BlockSpec((ms, k), lambda j, i: (0, 0)),
                  pl.BlockSpec((k // grid[1], bn), _ws_wchunk(grid[0] - 1)),
                  pl.BlockSpec((bm, bn), _ws_tile()),
                  pl.BlockSpec((ms, bn), _ws_stile()),
                  pl.BlockSpec((1, bn), _ws_stile())],
        out_specs=[pl.BlockSpec((bm, bn), _ws_tile()),
                   pl.BlockSpec((ms, bn), _ws_stile()),
                   pl.BlockSpec((bm, bn), _ws_tile()),
                   pl.BlockSpec((ms, bn), _ws_stile()),
                   pl.BlockSpec((bm, LANES), ssp_map),
                   pl.BlockSpec((ms, LANES), lambda j, i: (0, 0))],
        out_shape=[jax.ShapeDtypeStruct((mp, n), F32),
                   jax.ShapeDtypeStruct((ms, n), F32),
                   jax.ShapeDtypeStruct((mp, n), BF16),
                   jax.ShapeDtypeStruct((ms, n), BF16),
                   jax.ShapeDtypeStruct((mp, LANES), F32),
                   jax.ShapeDtypeStruct((ms, LANES), F32)],
        scratch_shapes=[pltpu.VMEM((k, bn), BF16), pltpu.VMEM((k, bn), BF16),
                        pltpu.VMEM((mp, LANES), F32), pltpu.VMEM((ms, LANES), F32)],
        compiler_params=_cparams("arbitrary", "arbitrary"),
        name="out_proj",
    )(xp, xs, w, rp, rs, gain.reshape(1, n))


def _inv_rms(ss_ref):
    return lax.rsqrt(jnp.sum(ss_ref[...], axis=-1, keepdims=True) * (1.0 / D_MODEL) + EPS)


def _ffn_up_kernel(hp_ref, hs_ref, ssp_ref, sss_ref, wgf_ref, wuf_ref, cw_ref, cb_ref,
                   s0_ref, s1_ref, fp_ref, fs_ref, gs_ref, st_ref,
                   wg0_ref, wu0_ref, wg1_ref, wu1_ref, carry_ref, gbuf_ref, *, tiles_per_seq):
    i = pl.program_id(1)
    bm = hp_ref.shape[0]

    def compute(slot):
        wg_ref, wu_ref = slot
        w0 = cw_ref[0:1, :]
        w1 = cw_ref[1:2, :]
        w2 = cw_ref[2:3, :]
        bias = cb_ref[...]

        @pl.when(i % tiles_per_seq == 0)
        def _():
            carry_ref[...] = jnp.zeros(carry_ref.shape, F32)

        r = _inv_rms(ssp_ref)
        g = r * _dot(hp_ref[...], wg_ref[...])
        gbuf_ref[0:SUBLANES, :] = carry_ref[...]
        gbuf_ref[SUBLANES:SUBLANES + bm, :] = g
        g1 = gbuf_ref[SUBLANES - 1:SUBLANES - 1 + bm, :]
        g2 = gbuf_ref[SUBLANES - 2:SUBLANES - 2 + bm, :]
        gc = w0 * g2 + w1 * g1 + w2 * g + bias
        up = r * _dot(hp_ref[...], wu_ref[...])
        fp_ref[...] = (gc * _sigmoid(gc) * up).astype(fp_ref.dtype)
        tail = gbuf_ref[bm:SUBLANES + bm, :]
        carry_ref[...] = tail
        st_ref[0] = tail

        @pl.when(_is_last_row_tile())
        def _():
            rs = _inv_rms(sss_ref)
            gsm = rs * _dot(hs_ref[...], wg_ref[...])
            ups = rs * _dot(hs_ref[...], wu_ref[...])
            gcs = w0 * s0_ref[...] + w1 * s1_ref[...] + w2 * gsm + bias
            fs_ref[...] = (gcs * _sigmoid(gcs) * ups).astype(fs_ref.dtype)
            gs_ref[...] = gsm

    _ws_phases([wgf_ref, wuf_ref], [wg0_ref, wu0_ref], [wg1_ref, wu1_ref], compute)


def _ffn_up(hp, hs, ssp, sss, wg, wu, cw, cb, s0, s1, bm, bn):
    mp, k = hp.shape
    ms = hs.shape[0]
    n = wg.shape[1]
    grid = _ws_grid(n, bn, mp, bm)
    tiles_per_seq = SEQ // bm
    wchunk = pl.BlockSpec((k // grid[1], bn), _ws_wchunk(grid[0] - 1))
    wslot = pltpu.VMEM((k, bn), BF16)
    col = _ws_stile()

    def tail_map(j, i):
        return (jnp.where(j > 0, i, 0) // tiles_per_seq, 0, jnp.maximum(j - 1, 0))

    return pl.pallas_call(
        functools.partial(_ffn_up_kernel, tiles_per_seq=tiles_per_seq),
        grid=grid,
        in_specs=[pl.BlockSpec((bm, k), _ws_rows),
                  pl.BlockSpec((ms, k), lambda j, i: (0, 0)),
                  pl.BlockSpec((bm, LANES), _ws_rows),
                  pl.BlockSpec((ms, LANES), lambda j, i: (0, 0)),
                  wchunk, wchunk,
                  pl.BlockSpec((FFN_CONV_K, bn), col),
                  pl.BlockSpec((1, bn), col),
                  pl.BlockSpec((ms, bn), col),
                  pl.BlockSpec((ms, bn), col)],
        out_specs=[pl.BlockSpec((bm, bn), _ws_tile()),
                   pl.BlockSpec((ms, bn), col),
                   pl.BlockSpec((ms, bn), col),
                   pl.BlockSpec((1, SUBLANES, bn), tail_map)],
        out_shape=[jax.ShapeDtypeStruct((mp, n), BF16),
                   jax.ShapeDtypeStruct((ms, n), BF16),
                   jax.ShapeDtypeStruct((ms, n), F32),
                   jax.ShapeDtypeStruct((BATCH, SUBLANES, n), F32)],
        scratch_shapes=[wslot, wslot, wslot, wslot,
                        pltpu.VMEM((SUBLANES, bn), F32),
                        pltpu.VMEM((SUBLANES + bm, bn), F32)],
        compiler_params=_cparams("arbitrary", "arbitrary"),
        name="ffn_up",
    )(hp, hs, ssp, sss, wg, wu, cw, cb, s0, s1)


def kernel(x_prompt, x_sample, state_k_window, state_v_window, state_conv, state_ffn_conv,
           attn_norm_g, w_in, conv_w, w_branch_a, w_branch_b, sinks, w_out, ffn_norm_g,
           w_ffn_gate, w_ffn_up, ffn_conv_w, ffn_conv_b, w_ffn_down, rel_bias,
           final_norm_g):
    xp = x_prompt.reshape(M_PROMPT, D_MODEL)
    xs = x_sample.reshape(DEC_BATCH, D_MODEL)

    fs0 = state_ffn_conv[0][:, 0, :]
    fs1 = state_ffn_conv[0][:, 1, :]

    hp = _rmsnorm(xp, attn_norm_g[0], BF16, 256)
    hs = _rmsnorm(xs, attn_norm_g[0], BF16, DEC_BATCH)
    proj_p, proj_s = _in_proj(hp, hs, w_in[0], W_REST, PROJ_W, 1024, 1024)

    cs0 = state_conv[0][:, 0, :]
    cs1 = state_conv[0][:, 1, :]
    a_p, a_s, u_s, u_tail = _mixer_a(hp, hs, w_in[0], conv_w[0], cs0, cs1, 512, 512)

    bias_tab = _bias_table(rel_bias)
    o_p = _attn_prompt(proj_p, bias_tab, sinks[0])
    q3 = proj_s[:, OFF_Q:OFF_Q + ATTN_W].reshape(DEC_BATCH, N_HEADS, HEAD_DIM)
    k_new = proj_s[:, OFF_K:OFF_K + KV_W].astype(F32)
    v_new = proj_s[:, OFF_V:OFF_V + KV_W].astype(F32)
    bias_s = bias_tab[0, :, Q_BLOCK - 1:, Q_BLOCK - 1]
    o_s3, k_win_s, v_win_s = _attn_sample(
        q3, k_new, v_new,
        state_k_window[0].reshape(DEC_BATCH, WINDOW, KV_W),
        state_v_window[0].reshape(DEC_BATCH, WINDOW, KV_W), bias_s, sinks[0])
    o_s = o_s3.reshape(DEC_BATCH, ATTN_W)

    mg_p, mg_s = _branch_merge(a_p, o_p, a_s, o_s, w_branch_a[0], w_branch_b[0],
                               proj_p, proj_s, 1024, 512)
    x1p, x1s, xbp, xbs, ssp, sss = _out_proj(mg_p, mg_s, w_out[0], xp, xs, ffn_norm_g[0],
                                             1024, 512)

    f_p, f_s, g_s, g_tail = _ffn_up(xbp, xbs, ssp, sss, w_ffn_gate[0], w_ffn_up[0],
                                    ffn_conv_w[0], ffn_conv_b, fs0, fs1, 1024, 512)
    x2p, x2s = _resid_matmul(f_p, f_s, w_ffn_down[0], x1p, x1s, 512, 512, "ffn_down")
    y_p = _rmsnorm(x2p, final_norm_g, F32, 256)
    y_s = _rmsnorm(x2s, final_norm_g, F32, DEC_BATCH)

    kv_p = proj_p.reshape(BATCH, SEQ, PROJ_W)[:, SEQ - WINDOW:, OFF_K:OFF_GA].astype(F32)
    k_win_p = kv_p[:, :, :KV_W].reshape(1, BATCH, WINDOW, N_KV_HEADS, HEAD_DIM)
    v_win_p = kv_p[:, :, KV_W:].reshape(1, BATCH, WINDOW, N_KV_HEADS, HEAD_DIM)
    conv_p = u_tail[:, SUBLANES - (CONV_K - 1):, :]
    ffn_p = g_tail[:, SUBLANES - (FFN_CONV_K - 1):, :]
    conv_s = jnp.stack([cs1, u_s], axis=1)
    ffn_s = jnp.stack([fs1, g_s], axis=1)

    return (y_p.reshape(BATCH, SEQ, D_MODEL),
            y_s.reshape(DEC_BATCH, 1, D_MODEL),
            k_win_p, v_win_p, conv_p[None], ffn_p[None],
            k_win_s.reshape(1, DEC_BATCH, WINDOW, N_KV_HEADS, HEAD_DIM),
            v_win_s.reshape(1, DEC_BATCH, WINDOW, N_KV_HEADS, HEAD_DIM),
            conv_s[None], ffn_s[None])
```

```python
import functools
import math

import numpy as np
import jax
import jax.numpy as jnp
from jax import lax
from jax.experimental import pallas as pl
from jax.experimental.pallas import tpu as pltpu

F32 = jnp.float32
BF16 = jnp.bfloat16

D_MODEL = 4096
BATCH = 4
SEQ = 2048
DEC_BATCH = 128
N_HEADS = 32
N_KV_HEADS = 8
HEAD_DIM = 64
GROUP = N_HEADS // N_KV_HEADS
ATTN_W = N_HEADS * HEAD_DIM
KV_W = N_KV_HEADS * HEAD_DIM
CONV_W = D_MODEL // 2
CONV_K = 3
WINDOW = 128
Q_BLOCK = 128
N_BUCKETS = 32
MAX_DISTANCE = 128
D_FF = 11008
FFN_CONV_K = 3
EPS = 1e-5
NEG = -1e30
IN_W = 3 * CONV_W + ATTN_W + 2 * KV_W + 2 * D_MODEL
M_PROMPT = BATCH * SEQ

W_CB = 0
W_CC = CONV_W
W_CH = 2 * CONV_W
W_REST = 3 * CONV_W
OFF_Q = 0
OFF_K = OFF_Q + ATTN_W
OFF_V = OFF_K + KV_W
OFF_GA = OFF_V + KV_W
OFF_GB = OFF_GA + D_MODEL
PROJ_W = OFF_GB + D_MODEL

V7X_VMEM_BYTES = 64 * 1024 * 1024
VMEM_LIMIT = V7X_VMEM_BYTES - 2 * 1024 * 1024
SUBLANES = 8


def _cparams(*sem):
    return pltpu.CompilerParams(dimension_semantics=sem, vmem_limit_bytes=VMEM_LIMIT)


def _dot(a, b):
    return jnp.dot(a, b, preferred_element_type=F32)


def _sigmoid(x):
    return 1.0 / (1.0 + jnp.exp(-x))


def _ws_grid(n, bn, mp, bm):
    return (pl.cdiv(n, bn) + 1, mp // bm)


def _ws_rows(j, i):
    return (jnp.where(j > 0, i, 0), 0)


def _ws_wchunk(nj, col0=0):
    return lambda j, i: (i, col0 + jnp.minimum(j, nj - 1))


def _ws_tile(col0=0):
    return lambda j, i: (jnp.where(j > 0, i, 0), col0 + jnp.maximum(j - 1, 0))


def _ws_stile(col0=0):
    return lambda j, i: (0, col0 + jnp.maximum(j - 1, 0))


def _ws_cast(wf_refs, dst_refs):
    i = pl.program_id(1)
    for wf_ref, dst_ref in zip(wf_refs, dst_refs):
        rows = wf_ref.shape[0]
        r0 = pl.multiple_of(i * rows, rows)
        dst_ref[pl.ds(r0, rows), :] = wf_ref[...].astype(BF16)


def _ws_phases(wf_refs, slot0, slot1, compute):
    j = pl.program_id(0)

    @pl.when(j == 0)
    def _():
        _ws_cast(wf_refs, slot0)

    @pl.when((j > 0) & (j % 2 == 1))
    def _():
        _ws_cast(wf_refs, slot1)
        compute(slot0)

    @pl.when((j > 0) & (j % 2 == 0))
    def _():
        _ws_cast(wf_refs, slot0)
        compute(slot1)


def _is_last_row_tile():
    return pl.program_id(1) == pl.num_programs(1) - 1


def _rms_kernel(x_ref, g_ref, o_ref):
    x = x_ref[...]
    r = lax.rsqrt(jnp.mean(x * x, axis=-1, keepdims=True) + EPS)
    o_ref[...] = ((x * r) * g_ref[...]).astype(o_ref.dtype)


def _rmsnorm(x, g, out_dtype, block_rows):
    rows, d = x.shape
    return pl.pallas_call(
        _rms_kernel,
        grid=(rows // block_rows,),
        in_specs=[pl.BlockSpec((block_rows, d), lambda i: (i, 0)),
                  pl.BlockSpec((1, d), lambda i: (0, 0))],
        out_specs=pl.BlockSpec((block_rows, d), lambda i: (i, 0)),
        out_shape=jax.ShapeDtypeStruct((rows, d), out_dtype),
        compiler_params=_cparams("parallel"),
        name="rmsnorm",
    )(x, g.reshape(1, d))


def _proj_kernel(xp_ref, xs_ref, wf_ref, op_ref, os_ref, w0_ref, w1_ref):
    def compute(slot):
        (w_ref,) = slot
        op_ref[...] = _dot(xp_ref[...], w_ref[...]).astype(op_ref.dtype)

        @pl.when(_is_last_row_tile())
        def _():
            os_ref[...] = _dot(xs_ref[...], w_ref[...]).astype(os_ref.dtype)

    _ws_phases([wf_ref], [w0_ref], [w1_ref], compute)


def _in_proj(hp, hs, w, col0, n, bm, bn):
    mp, k = hp.shape
    ms = hs.shape[0]
    grid = _ws_grid(n, bn, mp, bm)
    return pl.pallas_call(
        _proj_kernel,
        grid=grid,
        in_specs=[pl.BlockSpec((bm, k), _ws_rows),
                  pl.BlockSpec((ms, k), lambda j, i: (0, 0)),
                  pl.BlockSpec((k // grid[1], bn), _ws_wchunk(grid[0] - 1, col0 // bn))],
        out_specs=[pl.BlockSpec((bm, bn), _ws_tile()),
                   pl.BlockSpec((ms, bn), _ws_stile())],
        out_shape=[jax.ShapeDtypeStruct((mp, n), BF16),
                   jax.ShapeDtypeStruct((ms, n), BF16)],
        scratch_shapes=[pltpu.VMEM((k, bn), BF16), pltpu.VMEM((k, bn), BF16)],
        compiler_params=_cparams("arbitrary", "arbitrary"),
        name="in_proj",
    )(hp, hs, w)


def _mixa_kernel(hp_ref, hs_ref, wbf_ref, wcf_ref, whf_ref, cw_ref, s0_ref, s1_ref,
                 ap_ref, as_ref, us_ref, st_ref,
                 wb0_ref, wc0_ref, wh0_ref, wb1_ref, wc1_ref, wh1_ref, carry_ref, ubuf_ref,
                 *, tiles_per_seq):
    i = pl.program_id(1)
    bm = hp_ref.shape[0]

    def compute(slot):
        wb_ref, wc_ref, wh_ref = slot
        w0 = cw_ref[0:1, :]
        w1 = cw_ref[1:2, :]
        w2 = cw_ref[2:3, :]

        @pl.when(i % tiles_per_seq == 0)
        def _():
            carry_ref[...] = jnp.zeros(carry_ref.shape, F32)

        u = _dot(hp_ref[...], wc_ref[...]) * _dot(hp_ref[...], wh_ref[...])
        ubuf_ref[0:SUBLANES, :] = carry_ref[...]
        ubuf_ref[SUBLANES:SUBLANES + bm, :] = u
        u1 = ubuf_ref[SUBLANES - 1:SUBLANES - 1 + bm, :]
        u2 = ubuf_ref[SUBLANES - 2:SUBLANES - 2 + bm, :]
        z = w0 * u2 + w1 * u1 + w2 * u
        ap_ref[...] = (_dot(hp_ref[...], wb_ref[...]) * z).astype(ap_ref.dtype)
        tail = ubuf_ref[bm:SUBLANES + bm, :]
        carry_ref[...] = tail
        st_ref[0] = tail

        @pl.when(_is_last_row_tile())
        def _():
            us = _dot(hs_ref[...], wc_ref[...]) * _dot(hs_ref[...], wh_ref[...])
            zs = w0 * s0_ref[...] + w1 * s1_ref[...] + w2 * us
            as_ref[...] = (_dot(hs_ref[...], wb_ref[...]) * zs).astype(as_ref.dtype)
            us_ref[...] = us

    _ws_phases([wbf_ref, wcf_ref, whf_ref], [wb0_ref, wc0_ref, wh0_ref],
               [wb1_ref, wc1_ref, wh1_ref], compute)


def _mixer_a(hp, hs, w_in, conv_w, s0, s1, bm, bn):
    mp, k = hp.shape
    ms = hs.shape[0]
    n = CONV_W
    grid = _ws_grid(n, bn, mp, bm)
    nj = grid[0] - 1
    tiles_per_seq = SEQ // bm
    wslot = pltpu.VMEM((k, bn), BF16)
    col = _ws_stile()

    def wchunk(col0):
        return pl.BlockSpec((k // grid[1], bn), _ws_wchunk(nj, col0 // bn))

    def tail_map(j, i):
        return (jnp.where(j > 0, i, 0) // tiles_per_seq, 0, jnp.maximum(j - 1, 0))

    return pl.pallas_call(
        functools.partial(_mixa_kernel, tiles_per_seq=tiles_per_seq),
        grid=grid,
        in_specs=[pl.BlockSpec((bm, k), _ws_rows),
                  pl.BlockSpec((ms, k), lambda j, i: (0, 0)),
                  wchunk(W_CB), wchunk(W_CC), wchunk(W_CH),
                  pl.BlockSpec((CONV_K, bn), col),
                  pl.BlockSpec((ms, bn), col),
                  pl.BlockSpec((ms, bn), col)],
        out_specs=[pl.BlockSpec((bm, bn), _ws_tile()),
                   pl.BlockSpec((ms, bn), col),
                   pl.BlockSpec((ms, bn), col),
                   pl.BlockSpec((1, SUBLANES, bn), tail_map)],
        out_shape=[jax.ShapeDtypeStruct((mp, n), BF16),
                   jax.ShapeDtypeStruct((ms, n), BF16),
                   jax.ShapeDtypeStruct((ms, n), F32),
                   jax.ShapeDtypeStruct((BATCH, SUBLANES, n), F32)],
        scratch_shapes=[wslot] * 6 + [pltpu.VMEM((SUBLANES, bn), F32),
                                      pltpu.VMEM((SUBLANES + bm, bn), F32)],
        compiler_params=_cparams("arbitrary", "arbitrary"),
        name="mixer_a",
    )(hp, hs, w_in, w_in, w_in, conv_w, s0, s1)


def _bucket_table():
    qi = np.arange(Q_BLOCK)[None, :]
    kj = np.arange(2 * Q_BLOCK)[:, None]
    dist = qi + Q_BLOCK - kj
    max_exact = N_BUCKETS // 2
    d = np.maximum(dist, 0)
    df = np.maximum(d, 1).astype(np.float32)
    large = max_exact + (np.log(df / np.float32(max_exact))
                         / np.float32(math.log(MAX_DISTANCE / max_exact))
                         * np.float32(N_BUCKETS - max_exact)).astype(np.int32)
    large = np.minimum(large, N_BUCKETS - 1)
    bucket = np.where(d < max_exact, d, large).astype(np.int32)
    valid = ((dist >= 0) & (dist <= WINDOW)).astype(np.int32)
    return bucket, valid


def _bias_kernel(rb_ref, bucket_ref, valid_ref, o_ref):
    bucket = bucket_ref[...]
    row = lax.broadcasted_iota(jnp.int32, bucket.shape, 0)
    keep = valid_ref[...] != 0
    keep_first = keep & (row >= Q_BLOCK)

    def head(h, carry):
        acc = jnp.zeros(bucket.shape, F32)
        for b in range(N_BUCKETS):
            acc = jnp.where(bucket == b, rb_ref[b, h], acc)
        o_ref[0, h] = jnp.where(keep, acc, NEG)
        o_ref[1, h] = jnp.where(keep_first, acc, NEG)
        return carry

    lax.fori_loop(0, N_HEADS, head, 0)


def _bias_table(rel_bias):
    bucket, valid = _bucket_table()
    shp = (2 * Q_BLOCK, Q_BLOCK)
    return pl.pallas_call(
        _bias_kernel,
        grid=(1,),
        in_specs=[pl.BlockSpec(memory_space=pltpu.SMEM),
                  pl.BlockSpec(shp, lambda v: (0, 0)),
                  pl.BlockSpec(shp, lambda v: (0, 0))],
        out_specs=pl.BlockSpec((2, N_HEADS) + shp, lambda v: (0, 0, 0, 0)),
        out_shape=jax.ShapeDtypeStruct((2, N_HEADS) + shp, F32),
        compiler_params=_cparams("arbitrary"),
        name="rel_bias_table",
    )(rel_bias, jnp.asarray(bucket), jnp.asarray(valid))


def _attn_prompt_kernel(sink_ref, q_ref, kp_ref, kc_ref, vp_ref, vc_ref, bias_ref, o_ref):
    scale = HEAD_DIM ** -0.5

    def scores(kv):
        ks = slice(kv * HEAD_DIM, (kv + 1) * HEAD_DIM)
        heads = range(kv * GROUP, (kv + 1) * GROUP)
        kband = jnp.concatenate([kp_ref[:, ks], kc_ref[:, ks]], axis=0)
        qg = jnp.concatenate(
            [q_ref[:, h * HEAD_DIM:(h + 1) * HEAD_DIM] for h in heads], axis=0) * scale
        st = lax.dot_general(kband, qg, (((1,), (1,)), ((), ())),
                             preferred_element_type=F32)
        return st + jnp.concatenate([bias_ref[0, h] for h in heads], axis=1)

    def finish(kv, st):
        ks = slice(kv * HEAD_DIM, (kv + 1) * HEAD_DIM)
        heads = range(kv * GROUP, (kv + 1) * GROUP)
        vband = jnp.concatenate([vp_ref[:, ks], vc_ref[:, ks]], axis=0)
        sk = jnp.concatenate([jnp.full((1, Q_BLOCK), sink_ref[h], F32) for h in heads], axis=1)
        m = jnp.maximum(jnp.max(st, axis=0, keepdims=True), sk)
        e = jnp.exp(st - m)
        denom = jnp.sum(e, axis=0, keepdims=True) + jnp.exp(sk - m)
        ot = lax.dot_general(vband, e.astype(BF16), (((0,), (0,)), ((), ())),
                             preferred_element_type=F32) * (1.0 / denom)
        o2 = jnp.concatenate(
            [ot[:, g * Q_BLOCK:(g + 1) * Q_BLOCK] for g in range(GROUP)], axis=0)
        gs = slice(kv * GROUP * HEAD_DIM, (kv + 1) * GROUP * HEAD_DIM)
        o_ref[:, gs] = o2.T.astype(o_ref.dtype)

    st = scores(0)
    for kv in range(N_KV_HEADS):
        st_next = scores(kv + 1) if kv + 1 < N_KV_HEADS else None
        finish(kv, st)
        st = st_next


def _attn_prompt(proj_p, bias_tab, sinks):
    nblk = SEQ // Q_BLOCK
    qcol = OFF_Q // ATTN_W
    kcol = OFF_K // KV_W
    vcol = OFF_V // KV_W

    def cur(b, n):
        return b * nblk + n

    def prev(b, n):
        return b * nblk + jnp.maximum(n - 1, 0)

    return pl.pallas_call(
        _attn_prompt_kernel,
        grid=(BATCH, nblk),
        in_specs=[pl.BlockSpec(memory_space=pltpu.SMEM),
                  pl.BlockSpec((Q_BLOCK, ATTN_W), lambda b, n: (cur(b, n), qcol)),
                  pl.BlockSpec((Q_BLOCK, KV_W), lambda b, n: (prev(b, n), kcol)),
                  pl.BlockSpec((Q_BLOCK, KV_W), lambda b, n: (cur(b, n), kcol)),
                  pl.BlockSpec((Q_BLOCK, KV_W), lambda b, n: (prev(b, n), vcol)),
                  pl.BlockSpec((Q_BLOCK, KV_W), lambda b, n: (cur(b, n), vcol)),
                  pl.BlockSpec((1, N_HEADS, 2 * Q_BLOCK, Q_BLOCK),
                               lambda b, n: (jnp.where(n == 0, 1, 0), 0, 0, 0))],
        out_specs=pl.BlockSpec((Q_BLOCK, ATTN_W), lambda b, n: (cur(b, n), 0)),
        out_shape=jax.ShapeDtypeStruct((M_PROMPT, ATTN_W), BF16),
        compiler_params=_cparams("parallel", "arbitrary"),
        name="attn_prompt",
    )(sinks, proj_p, proj_p, proj_p, proj_p, proj_p, bias_tab)


SAMPLES_PER_STEP = 8


def _attn_sample_kernel(q_ref, kn_ref, vn_ref, kb_ref, vb_ref, bias_ref, sink_ref,
                        o_ref, ko_ref, vo_ref):
    g = SAMPLES_PER_STEP
    scale = HEAD_DIM ** -0.5
    erow = lax.broadcasted_iota(jnp.int32, (HEAD_DIM, KV_W), 0)
    ecol = lax.broadcasted_iota(jnp.int32, (HEAD_DIM, KV_W), 1)
    expand = (ecol % HEAD_DIM == erow).astype(BF16)
    hrow = lax.broadcasted_iota(jnp.int32, (N_HEADS, KV_W), 0)
    hcol = lax.broadcasted_iota(jnp.int32, (N_HEADS, KV_W), 1)
    own = ((hrow // GROUP) == (hcol // HEAD_DIM))[None]
    bias_w = bias_ref[:, 0:WINDOW][None]
    bias_n = bias_ref[:, WINDOW:WINDOW + 1][None]
    sk = sink_ref[...][None]
    contract_last = (((1,), (1,)), ((), ()))

    q_all = q_ref[...].reshape(g * N_HEADS, HEAD_DIM) * scale
    qrow = jnp.where(own, _dot(q_all, expand).reshape(g, N_HEADS, KV_W), 0.0)
    qrow_b = qrow.astype(BF16)
    kn = kn_ref[...][:, None, :]
    vn = vn_ref[...][:, None, :]
    s_w = jnp.stack([lax.dot_general(qrow_b[b], kb_ref[b].astype(BF16), contract_last,
                                     preferred_element_type=F32) for b in range(g)])
    s_w = s_w + bias_w
    s_n = jnp.sum(qrow * kn, axis=-1, keepdims=True) + bias_n
    m = jnp.maximum(jnp.maximum(jnp.max(s_w, axis=-1, keepdims=True), s_n), sk)
    e_w = jnp.exp(s_w - m)
    e_n = jnp.exp(s_n - m)
    r = 1.0 / (jnp.sum(e_w, axis=-1, keepdims=True) + e_n + jnp.exp(sk - m))
    p_w = (e_w * r).astype(BF16)
    o_all = jnp.stack([_dot(p_w[b], vb_ref[b].astype(BF16)) for b in range(g)])
    o_own = jnp.where(own, o_all + (e_n * r) * vn, 0.0)
    o = o_own[:, :, 0:HEAD_DIM]
    for kv in range(1, N_KV_HEADS):
        o = o + o_own[:, :, kv * HEAD_DIM:(kv + 1) * HEAD_DIM]
    o_ref[...] = o.astype(o_ref.dtype)
    for b in range(g):
        ko_ref[b] = jnp.concatenate([kb_ref[b, 1:, :], kn_ref[b:b + 1, :]], axis=0)
        vo_ref[b] = jnp.concatenate([vb_ref[b, 1:, :], vn_ref[b:b + 1, :]], axis=0)


def _attn_sample(q3, k_new, v_new, k_buf, v_buf, bias_s, sinks):
    g = SAMPLES_PER_STEP
    nb = DEC_BATCH // g
    win = pl.BlockSpec((g, WINDOW, KV_W), lambda i: (i, 0, 0))
    row = pl.BlockSpec((g, KV_W), lambda i: (i, 0))
    return pl.pallas_call(
        _attn_sample_kernel,
        grid=(nb,),
        in_specs=[pl.BlockSpec((g, N_HEADS, HEAD_DIM), lambda i: (i, 0, 0)),
                  row, row, win, win,
                  pl.BlockSpec((N_HEADS, WINDOW + 1), lambda i: (0, 0)),
                  pl.BlockSpec((N_HEADS, 1), lambda i: (0, 0))],
        out_specs=[pl.BlockSpec((g, N_HEADS, HEAD_DIM), lambda i: (i, 0, 0)), win, win],
        out_shape=[jax.ShapeDtypeStruct((DEC_BATCH, N_HEADS, HEAD_DIM), BF16),
                   jax.ShapeDtypeStruct((DEC_BATCH, WINDOW, KV_W), F32),
                   jax.ShapeDtypeStruct((DEC_BATCH, WINDOW, KV_W), F32)],
        compiler_params=_cparams("parallel"),
        name="attn_sample",
    )(q3, k_new, v_new, k_buf, v_buf, bias_s, sinks.reshape(N_HEADS, 1))


def _merge_kernel(ap_ref, bp_ref, as_ref, bs_ref, waf_ref, wbf_ref,
                  gap_ref, gbp_ref, gas_ref, gbs_ref, op_ref, os_ref,
                  wa0_ref, wb0_ref, wa1_ref, wb1_ref):
    def compute(slot):
        wa_ref, wb_ref = slot

        def merged(a_ref, b_ref, ga_ref, gb_ref):
            return (_sigmoid(ga_ref[...].astype(F32)) * _dot(a_ref[...], wa_ref[...])
                    + _sigmoid(gb_ref[...].astype(F32)) * _dot(b_ref[...], wb_ref[...]))

        op_ref[...] = merged(ap_ref, bp_ref, gap_ref, gbp_ref).astype(op_ref.dtype)

        @pl.when(_is_last_row_tile())
        def _():
            os_ref[...] = merged(as_ref, bs_ref, gas_ref, gbs_ref).astype(os_ref.dtype)

    _ws_phases([waf_ref, wbf_ref], [wa0_ref, wb0_ref], [wa1_ref, wb1_ref], compute)


def _branch_merge(a_p, o_p, a_s, o_s, wa, wb, proj_p, proj_s, bm, bn):
    mp, k = a_p.shape
    ms = a_s.shape[0]
    n = wa.shape[1]
    grid = _ws_grid(n, bn, mp, bm)
    ga0 = OFF_GA // bn
    gb0 = OFF_GB // bn
    wchunk = pl.BlockSpec((k // grid[1], bn), _ws_wchunk(grid[0] - 1))
    wslot = pltpu.VMEM((k, bn), BF16)
    return pl.pallas_call(
        _merge_kernel,
        grid=grid,
        in_specs=[pl.BlockSpec((bm, k), _ws_rows),
                  pl.BlockSpec((bm, k), _ws_rows),
                  pl.BlockSpec((ms, k), lambda j, i: (0, 0)),
                  pl.BlockSpec((ms, k), lambda j, i: (0, 0)),
                  wchunk, wchunk,
                  pl.BlockSpec((bm, bn), _ws_tile(ga0)),
                  pl.BlockSpec((bm, bn), _ws_tile(gb0)),
                  pl.BlockSpec((ms, bn), _ws_stile(ga0)),
                  pl.BlockSpec((ms, bn), _ws_stile(gb0))],
        out_specs=[pl.BlockSpec((bm, bn), _ws_tile()),
                   pl.BlockSpec((ms, bn), _ws_stile())],
        out_shape=[jax.ShapeDtypeStruct((mp, n), BF16),
                   jax.ShapeDtypeStruct((ms, n), BF16)],
        scratch_shapes=[wslot, wslot, wslot, wslot],
        compiler_params=_cparams("arbitrary", "arbitrary"),
        name="branch_merge",
    )(a_p, o_p, a_s, o_s, wa, wb, proj_p, proj_p, proj_s, proj_s)


def _resid_kernel(xp_ref, xs_ref, wf_ref, rp_ref, rs_ref, op_ref, os_ref, w0_ref, w1_ref):
    def compute(slot):
        (w_ref,) = slot
        op_ref[...] = rp_ref[...] + _dot(xp_ref[...], w_ref[...])

        @pl.when(_is_last_row_tile())
        def _():
            os_ref[...] = rs_ref[...] + _dot(xs_ref[...], w_ref[...])

    _ws_phases([wf_ref], [w0_ref], [w1_ref], compute)


def _resid_matmul(xp, xs, w, rp, rs, bm, bn, name):
    mp, k = xp.shape
    ms = xs.shape[0]
    n = w.shape[1]
    grid = _ws_grid(n, bn, mp, bm)
    return pl.pallas_call(
        _resid_kernel,
        grid=grid,
        in_specs=[pl.BlockSpec((bm, k), _ws_rows),
                  pl.BlockSpec((ms, k), lambda j, i: (0, 0)),
                  pl.BlockSpec((k // grid[1], bn), _ws_wchunk(grid[0] - 1)),
                  pl.BlockSpec((bm, bn), _ws_tile()),
                  pl.BlockSpec((ms, bn), _ws_stile())],
        out_specs=[pl.BlockSpec((bm, bn), _ws_tile()),
                   pl.BlockSpec((ms, bn), _ws_stile())],
        out_shape=[jax.ShapeDtypeStruct((mp, n), F32),
                   jax.ShapeDtypeStruct((ms, n), F32)],
        scratch_shapes=[pltpu.VMEM((k, bn), BF16), pltpu.VMEM((k, bn), BF16)],
        compiler_params=_cparams("arbitrary", "arbitrary"),
        name=name,
    )(xp, xs, w, rp, rs)


LANES = 128


def _lane_partials(v):
    part = v[:, 0:LANES]
    for c in range(1, v.shape[1] // LANES):
        part = part + v[:, c * LANES:(c + 1) * LANES]
    return part


def _outproj_kernel(xp_ref, xs_ref, wf_ref, rp_ref, rs_ref, g_ref,
                    op_ref, os_ref, bp_ref, bs_ref, ssp_ref, sss_ref,
                    w0_ref, w1_ref, accp_ref, accs_ref):
    j = pl.program_id(0)
    i = pl.program_id(1)
    bm = xp_ref.shape[0]

    def compute(slot):
        (w_ref,) = slot

        def tile(x_ref, r_ref, o_ref, b_ref, acc_view, ss_ref):
            x1 = r_ref[...] + _dot(x_ref[...], w_ref[...])
            o_ref[...] = x1
            b_ref[...] = (x1 * g_ref[...]).astype(b_ref.dtype)
            total = jnp.where(j == 1, 0.0, acc_view[...]) + _lane_partials(x1 * x1)
            acc_view[...] = total
            ss_ref[...] = total

        rows = pl.ds(pl.multiple_of(i * bm, bm), bm)
        tile(xp_ref, rp_ref, op_ref, bp_ref, accp_ref.at[rows, :], ssp_ref)

        @pl.when(_is_last_row_tile())
        def _():
            tile(xs_ref, rs_ref, os_ref, bs_ref, accs_ref, sss_ref)

    _ws_phases([wf_ref], [w0_ref], [w1_ref], compute)


def _out_proj(xp, xs, w, rp, rs, gain, bm, bn):
    mp, k = xp.shape
    ms = xs.shape[0]
    n = w.shape[1]
    grid = _ws_grid(n, bn, mp, bm)
    last = grid[0] - 1
    ssp_map = lambda j, i: (jnp.where(j == last, i, 0), 0)
    return pl.pallas_call(
        _outproj_kernel,
        grid=grid,
        in_specs=[pl.BlockSpec((bm, k), _ws_rows),
                  pl.BlockSpec((ms, k), lambda j, i: (0, 0)),
                  pl.BlockSpec((k // grid[1], bn), _ws_wchunk(grid[0] - 1)),
                  pl.BlockSpec((bm, bn), _ws_tile()),
                  pl.BlockSpec((ms, bn), _ws_stile()),
                  pl.BlockSpec((1, bn), _ws_stile())],
        out_specs=[pl.BlockSpec((bm, bn), _ws_tile()),
                   pl.BlockSpec((ms, bn), _ws_stile()),
                   pl.BlockSpec((bm, bn), _ws_tile()),
                   pl.BlockSpec((ms, bn), _ws_stile()),
                   pl.BlockSpec((bm, LANES), ssp_map),
                   pl.BlockSpec((ms, LANES), lambda j, i: (0, 0))],
        out_shape=[jax.ShapeDtypeStruct((mp, n), F32),
                   jax.ShapeDtypeStruct((ms, n), F32),
                   jax.ShapeDtypeStruct((mp, n), BF16),
                   jax.ShapeDtypeStruct((ms, n), BF16),
                   jax.ShapeDtypeStruct((mp, LANES), F32),
                   jax.ShapeDtypeStruct((ms, LANES), F32)],
        scratch_shapes=[pltpu.VMEM((k, bn), BF16), pltpu.VMEM((k, bn), BF16),
                        pltpu.VMEM((mp, LANES), F32), pltpu.VMEM((ms, LANES), F32)],
        compiler_params=_cparams("arbitrary", "arbitrary"),
        name="out_proj",
    )(xp, xs, w, rp, rs, gain.reshape(1, n))


def _inv_rms(ss_ref):
    return lax.rsqrt(jnp.sum(ss_ref[...], axis=-1, keepdims=True) * (1.0 / D_MODEL) + EPS)


def _ffn_up_kernel(hp_ref, hs_ref, ssp_ref, sss_ref, wgf_ref, wuf_ref, cw_ref, cb_ref,
                   s0_ref, s1_ref, fp_ref, fs_ref, gs_ref, st_ref,
                   wg0_ref, wu0_ref, wg1_ref, wu1_ref, carry_ref, gbuf_ref, *, tiles_per_seq):
    i = pl.program_id(1)
    bm = hp_ref.shape[0]

    def compute(slot):
        wg_ref, wu_ref = slot
        w0 = cw_ref[0:1, :]
        w1 = cw_ref[1:2, :]
        w2 = cw_ref[2:3, :]
        bias = cb_ref[...]

        @pl.when(i % tiles_per_seq == 0)
        def _():
            carry_ref[...] = jnp.zeros(carry_ref.shape, F32)

        r = _inv_rms(ssp_ref)
        g = r * _dot(hp_ref[...], wg_ref[...])
        gbuf_ref[0:SUBLANES, :] = carry_ref[...]
        gbuf_ref[SUBLANES:SUBLANES + bm, :] = g
        g1 = gbuf_ref[SUBLANES - 1:SUBLANES - 1 + bm, :]
        g2 = gbuf_ref[SUBLANES - 2:SUBLANES - 2 + bm, :]
        gc = w0 * g2 + w1 * g1 + w2 * g + bias
        up = r * _dot(hp_ref[...], wu_ref[...])
        fp_ref[...] = (gc * _sigmoid(gc) * up).astype(fp_ref.dtype)
        tail = gbuf_ref[bm:SUBLANES + bm, :]
        carry_ref[...] = tail
        st_ref[0] = tail

        @pl.when(_is_last_row_tile())
        def _():
            rs = _inv_rms(sss_ref)
            gsm = rs * _dot(hs_ref[...], wg_ref[...])
            ups = rs * _dot(hs_ref[...], wu_ref[...])
            gcs = w0 * s0_ref[...] + w1 * s1_ref[...] + w2 * gsm + bias
            fs_ref[...] = (gcs * _sigmoid(gcs) * ups).astype(fs_ref.dtype)
            gs_ref[...] = gsm

    _ws_phases([wgf_ref, wuf_ref], [wg0_ref, wu0_ref], [wg1_ref, wu1_ref], compute)


def _ffn_up(hp, hs, ssp, sss, wg, wu, cw, cb, s0, s1, bm, bn):
    mp, k = hp.shape
    ms = hs.shape[0]
    n = wg.shape[1]
    grid = _ws_grid(n, bn, mp, bm)
    tiles_per_seq = SEQ // bm
    wchunk = pl.BlockSpec((k // grid[1], bn), _ws_wchunk(grid[0] - 1))
    wslot = pltpu.VMEM((k, bn), BF16)
    col = _ws_stile()

    def tail_map(j, i):
        return (jnp.where(j > 0, i, 0) // tiles_per_seq, 0, jnp.maximum(j - 1, 0))

    return pl.pallas_call(
        functools.partial(_ffn_up_kernel, tiles_per_seq=tiles_per_seq),
        grid=grid,
        in_specs=[pl.BlockSpec((bm, k), _ws_rows),
                  pl.BlockSpec((ms, k), lambda j, i: (0, 0)),
                  pl.BlockSpec((bm, LANES), _ws_rows),
                  pl.BlockSpec((ms, LANES), lambda j, i: (0, 0)),
                  wchunk, wchunk,
                  pl.BlockSpec((FFN_CONV_K, bn), col),
                  pl.BlockSpec((1, bn), col),
                  pl.BlockSpec((ms, bn), col),
                  pl.BlockSpec((ms, bn), col)],
        out_specs=[pl.BlockSpec((bm, bn), _ws_tile()),
                   pl.BlockSpec((ms, bn), col),
                   pl.BlockSpec((ms, bn), col),
                   pl.BlockSpec((1, SUBLANES, bn), tail_map)],
        out_shape=[jax.ShapeDtypeStruct((mp, n), BF16),
                   jax.ShapeDtypeStruct((ms, n), BF16),
                   jax.ShapeDtypeStruct((ms, n), F32),
                   jax.ShapeDtypeStruct((BATCH, SUBLANES, n), F32)],
        scratch_shapes=[wslot, wslot, wslot, wslot,
                        pltpu.VMEM((SUBLANES, bn), F32),
                        pltpu.VMEM((SUBLANES + bm, bn), F32)],
        compiler_params=_cparams("arbitrary", "arbitrary"),
        name="ffn_up",
    )(hp, hs, ssp, sss, wg, wu, cw, cb, s0, s1)


def kernel(x_prompt, x_sample, state_k_window, state_v_window, state_conv, state_ffn_conv,
           attn_norm_g, w_in, conv_w, w_branch_a, w_branch_b, sinks, w_out, ffn_norm_g,
           w_ffn_gate, w_ffn_up, ffn_conv_w, ffn_conv_b, w_ffn_down, rel_bias,
           final_norm_g):
    xp = x_prompt.reshape(M_PROMPT, D_MODEL)
    xs = x_sample.reshape(DEC_BATCH, D_MODEL)

    fs0 = state_ffn_conv[0][:, 0, :]
    fs1 = state_ffn_conv[0][:, 1, :]

    hp = _rmsnorm(xp, attn_norm_g[0], BF16, 256)
    hs = _rmsnorm(xs, attn_norm_g[0], BF16, DEC_BATCH)
    proj_p, proj_s = _in_proj(hp, hs, w_in[0], W_REST, PROJ_W, 1024, 1024)

    cs0 = state_conv[0][:, 0, :]
    cs1 = state_conv[0][:, 1, :]
    a_p, a_s, u_s, u_tail = _mixer_a(hp, hs, w_in[0], conv_w[0], cs0, cs1, 512, 512)

    bias_tab = _bias_table(rel_bias)
    o_p = _attn_prompt(proj_p, bias_tab, sinks[0])
    q3 = proj_s[:, OFF_Q:OFF_Q + ATTN_W].reshape(DEC_BATCH, N_HEADS, HEAD_DIM)
    k_new = proj_s[:, OFF_K:OFF_K + KV_W].astype(F32)
    v_new = proj_s[:, OFF_V:OFF_V + KV_W].astype(F32)
    bias_s = bias_tab[0, :, Q_BLOCK - 1:, Q_BLOCK - 1]
    o_s3, k_win_s, v_win_s = _attn_sample(
        q3, k_new, v_new,
        state_k_window[0].reshape(DEC_BATCH, WINDOW, KV_W),
        state_v_window[0].reshape(DEC_BATCH, WINDOW, KV_W), bias_s, sinks[0])
    o_s = o_s3.reshape(DEC_BATCH, ATTN_W)

    mg_p, mg_s = _branch_merge(a_p, o_p, a_s, o_s, w_branch_a[0], w_branch_b[0],
                               proj_p, proj_s, 512, 1024)
    x1p, x1s, xbp, xbs, ssp, sss = _out_proj(mg_p, mg_s, w_out[0], xp, xs, ffn_norm_g[0],
                                             512, 1024)

    f_p, f_s, g_s, g_tail = _ffn_up(xbp, xbs, ssp, sss, w_ffn_gate[0], w_ffn_up[0],
                                    ffn_conv_w[0], ffn_conv_b, fs0, fs1, 1024, 512)
    x2p, x2s = _resid_matmul(f_p, f_s, w_ffn_down[0], x1p, x1s, 512, 512, "ffn_down")
    y_p = _rmsnorm(x2p, final_norm_g, F32, 256)
    y_s = _rmsnorm(x2s, final_norm_g, F32, DEC_BATCH)

    kv_p = proj_p.reshape(BATCH, SEQ, PROJ_W)[:, SEQ - WINDOW:, OFF_K:OFF_GA].astype(F32)
    k_win_p = kv_p[:, :, :KV_W].reshape(1, BATCH, WINDOW, N_KV_HEADS, HEAD_DIM)
    v_win_p = kv_p[:, :, KV_W:].reshape(1, BATCH, WINDOW, N_KV_HEADS, HEAD_DIM)
    conv_p = u_tail[:, SUBLANES - (CONV_K - 1):, :]
    ffn_p = g_tail[:, SUBLANES - (FFN_CONV_K - 1):, :]
    conv_s = jnp.stack([cs1, u_s], axis=1)
    ffn_s = jnp.stack([fs1, g_s], axis=1)

    return (y_p.reshape(BATCH, SEQ, D_MODEL),
            y_s.reshape(DEC_BATCH, 1, D_MODEL),
            k_win_p, v_win_p, conv_p[None], ffn_p[None],
            k_win_s.reshape(1, DEC_BATCH, WINDOW, N_KV_HEADS, HEAD_DIM),
            v_win_s.reshape(1, DEC_BATCH, WINDOW, N_KV_HEADS, HEAD_DIM),
            conv_s[None], ffn_s[None])
```

```python
import functools
import math

import numpy as np
import jax
import jax.numpy as jnp
from jax import lax
from jax.experimental import pallas as pl
from jax.experimental.pallas import tpu as pltpu

F32 = jnp.float32
BF16 = jnp.bfloat16

D_MODEL = 4096
BATCH = 4
SEQ = 2048
DEC_BATCH = 128
N_HEADS = 32
N_KV_HEADS = 8
HEAD_DIM = 64
GROUP = N_HEADS // N_KV_HEADS
ATTN_W = N_HEADS * HEAD_DIM
KV_W = N_KV_HEADS * HEAD_DIM
CONV_W = D_MODEL // 2
CONV_K = 3
WINDOW = 128
Q_BLOCK = 128
N_BUCKETS = 32
MAX_DISTANCE = 128
D_FF = 11008
FFN_CONV_K = 3
EPS = 1e-5
NEG = -1e30
IN_W = 3 * CONV_W + ATTN_W + 2 * KV_W + 2 * D_MODEL
M_PROMPT = BATCH * SEQ

W_CB = 0
W_CC = CONV_W
W_CH = 2 * CONV_W
W_REST = 3 * CONV_W
OFF_Q = 0
OFF_K = OFF_Q + ATTN_W
OFF_V = OFF_K + KV_W
OFF_GA = OFF_V + KV_W
OFF_GB = OFF_GA + D_MODEL
PROJ_W = OFF_GB + D_MODEL

V7X_VMEM_BYTES = 64 * 1024 * 1024
VMEM_LIMIT = V7X_VMEM_BYTES - 1024 * 1024
SUBLANES = 8


def _cparams(*sem):
    return pltpu.CompilerParams(dimension_semantics=sem, vmem_limit_bytes=VMEM_LIMIT)


def _dot(a, b):
    return jnp.dot(a, b, preferred_element_type=F32)


def _sigmoid(x):
    return 1.0 / (1.0 + jnp.exp(-x))


def _ws_grid(n, bn, mp, bm):
    return (pl.cdiv(n, bn) + 1, mp // bm)


def _ws_rows(j, i):
    return (jnp.where(j > 0, i, 0), 0)


def _ws_wchunk(nj, col0=0):
    return lambda j, i: (i, col0 + jnp.minimum(j, nj - 1))


def _ws_tile(col0=0):
    return lambda j, i: (jnp.where(j > 0, i, 0), col0 + jnp.maximum(j - 1, 0))


def _ws_stile(col0=0):
    return lambda j, i: (0, col0 + jnp.maximum(j - 1, 0))


def _ws_cast(wf_refs, dst_refs):
    i = pl.program_id(1)
    for wf_ref, dst_ref in zip(wf_refs, dst_refs):
        rows = wf_ref.shape[0]
        r0 = pl.multiple_of(i * rows, rows)
        dst_ref[pl.ds(r0, rows), :] = wf_ref[...].astype(BF16)


def _ws_phases(wf_refs, slot0, slot1, compute):
    j = pl.program_id(0)

    @pl.when(j == 0)
    def _():
        _ws_cast(wf_refs, slot0)

    @pl.when((j > 0) & (j % 2 == 1))
    def _():
        _ws_cast(wf_refs, slot1)
        compute(slot0)

    @pl.when((j > 0) & (j % 2 == 0))
    def _():
        _ws_cast(wf_refs, slot0)
        compute(slot1)


def _is_last_row_tile():
    return pl.program_id(1) == pl.num_programs(1) - 1


def _rms_kernel(x_ref, g_ref, o_ref):
    x = x_ref[...]
    r = lax.rsqrt(jnp.mean(x * x, axis=-1, keepdims=True) + EPS)
    o_ref[...] = ((x * r) * g_ref[...]).astype(o_ref.dtype)


def _rmsnorm(x, g, out_dtype, block_rows):
    rows, d = x.shape
    return pl.pallas_call(
        _rms_kernel,
        grid=(rows // block_rows,),
        in_specs=[pl.BlockSpec((block_rows, d), lambda i: (i, 0)),
                  pl.BlockSpec((1, d), lambda i: (0, 0))],
        out_specs=pl.BlockSpec((block_rows, d), lambda i: (i, 0)),
        out_shape=jax.ShapeDtypeStruct((rows, d), out_dtype),
        compiler_params=_cparams("parallel"),
        name="rmsnorm",
    )(x, g.reshape(1, d))


def _proj_kernel(xp_ref, xs_ref, ssp_ref, wf_ref, op_ref, os_ref, w0_ref, w1_ref):
    def compute(slot):
        (w_ref,) = slot
        op_ref[...] = (_inv_rms(ssp_ref) * _dot(xp_ref[...], w_ref[...])).astype(op_ref.dtype)

        @pl.when(_is_last_row_tile())
        def _():
            os_ref[...] = _dot(xs_ref[...], w_ref[...]).astype(os_ref.dtype)

    _ws_phases([wf_ref], [w0_ref], [w1_ref], compute)


def _in_proj(hp, hs, ssp, w, col0, n, bm, bn):
    mp, k = hp.shape
    ms = hs.shape[0]
    grid = _ws_grid(n, bn, mp, bm)
    return pl.pallas_call(
        _proj_kernel,
        grid=grid,
        in_specs=[pl.BlockSpec((bm, k), _ws_rows),
                  pl.BlockSpec((ms, k), lambda j, i: (0, 0)),
                  pl.BlockSpec((bm, LANES), _ws_rows),
                  pl.BlockSpec((k // grid[1], bn), _ws_wchunk(grid[0] - 1, col0 // bn))],
        out_specs=[pl.BlockSpec((bm, bn), _ws_tile()),
                   pl.BlockSpec((ms, bn), _ws_stile())],
        out_shape=[jax.ShapeDtypeStruct((mp, n), BF16),
                   jax.ShapeDtypeStruct((ms, n), BF16)],
        scratch_shapes=[pltpu.VMEM((k, bn), BF16), pltpu.VMEM((k, bn), BF16)],
        compiler_params=_cparams("arbitrary", "arbitrary"),
        name="in_proj",
    )(hp, hs, ssp, w)


def _mixa_kernel(xp_ref, hs_ref, g_ref, wbf_ref, wcf_ref, whf_ref, cw_ref, s0_ref, s1_ref,
                 ap_ref, as_ref, us_ref, st_ref, hp_ref, ssp_ref,
                 wb0_ref, wc0_ref, wh0_ref, wb1_ref, wc1_ref, wh1_ref, carry_ref, ubuf_ref,
                 *, tiles_per_seq):
    i = pl.program_id(1)
    bm = xp_ref.shape[0]

    def compute(slot):
        wb_ref, wc_ref, wh_ref = slot
        w0 = cw_ref[0:1, :]
        w1 = cw_ref[1:2, :]
        w2 = cw_ref[2:3, :]

        @pl.when(i % tiles_per_seq == 0)
        def _():
            carry_ref[...] = jnp.zeros(carry_ref.shape, F32)

        hp_ref[...] = (xp_ref[...] * g_ref[...]).astype(hp_ref.dtype)
        ssp_ref[...] = _lane_partials(xp_ref[...] * xp_ref[...])
        r = _inv_rms(ssp_ref)
        u = (r * _dot(hp_ref[...], wc_ref[...])) * (r * _dot(hp_ref[...], wh_ref[...]))
        ubuf_ref[0:SUBLANES, :] = carry_ref[...]
        ubuf_ref[SUBLANES:SUBLANES + bm, :] = u
        u1 = ubuf_ref[SUBLANES - 1:SUBLANES - 1 + bm, :]
        u2 = ubuf_ref[SUBLANES - 2:SUBLANES - 2 + bm, :]
        z = w0 * u2 + w1 * u1 + w2 * u
        ap_ref[...] = ((r * _dot(hp_ref[...], wb_ref[...])) * z).astype(ap_ref.dtype)
        tail = ubuf_ref[bm:SUBLANES + bm, :]
        carry_ref[...] = tail
        st_ref[0] = tail

        @pl.when(_is_last_row_tile())
        def _():
            us = _dot(hs_ref[...], wc_ref[...]) * _dot(hs_ref[...], wh_ref[...])
            zs = w0 * s0_ref[...] + w1 * s1_ref[...] + w2 * us
            as_ref[...] = (_dot(hs_ref[...], wb_ref[...]) * zs).astype(as_ref.dtype)
            us_ref[...] = us

    _ws_phases([wbf_ref, wcf_ref, whf_ref], [wb0_ref, wc0_ref, wh0_ref],
               [wb1_ref, wc1_ref, wh1_ref], compute)


def _mixer_a(xp, hs, gain, w_in, conv_w, s0, s1, bm, bn):
    mp, k = xp.shape
    ms = hs.shape[0]
    n = CONV_W
    grid = _ws_grid(n, bn, mp, bm)
    nj = grid[0] - 1
    tiles_per_seq = SEQ // bm
    wslot = pltpu.VMEM((k, bn), BF16)
    col = _ws_stile()

    def wchunk(col0):
        return pl.BlockSpec((k // grid[1], bn), _ws_wchunk(nj, col0 // bn))

    def tail_map(j, i):
        return (jnp.where(j > 0, i, 0) // tiles_per_seq, 0, jnp.maximum(j - 1, 0))

    return pl.pallas_call(
        functools.partial(_mixa_kernel, tiles_per_seq=tiles_per_seq),
        grid=grid,
        in_specs=[pl.BlockSpec((bm, k), _ws_rows),
                  pl.BlockSpec((ms, k), lambda j, i: (0, 0)),
                  pl.BlockSpec((1, k), lambda j, i: (0, 0)),
                  wchunk(W_CB), wchunk(W_CC), wchunk(W_CH),
                  pl.BlockSpec((CONV_K, bn), col),
                  pl.BlockSpec((ms, bn), col),
                  pl.BlockSpec((ms, bn), col)],
        out_specs=[pl.BlockSpec((bm, bn), _ws_tile()),
                   pl.BlockSpec((ms, bn), col),
                   pl.BlockSpec((ms, bn), col),
                   pl.BlockSpec((1, SUBLANES, bn), tail_map),
                   pl.BlockSpec((bm, k), _ws_rows),
                   pl.BlockSpec((bm, LANES), _ws_rows)],
        out_shape=[jax.ShapeDtypeStruct((mp, n), BF16),
                   jax.ShapeDtypeStruct((ms, n), BF16),
                   jax.ShapeDtypeStruct((ms, n), F32),
                   jax.ShapeDtypeStruct((BATCH, SUBLANES, n), F32),
                   jax.ShapeDtypeStruct((mp, k), BF16),
                   jax.ShapeDtypeStruct((mp, LANES), F32)],
        scratch_shapes=[wslot] * 6 + [pltpu.VMEM((SUBLANES, bn), F32),
                                      pltpu.VMEM((SUBLANES + bm, bn), F32)],
        compiler_params=_cparams("arbitrary", "arbitrary"),
        name="mixer_a",
    )(xp, hs, gain.reshape(1, k), w_in, w_in, w_in, conv_w, s0, s1)


def _bucket_table():
    qi = np.arange(Q_BLOCK)[None, :]
    kj = np.arange(2 * Q_BLOCK)[:, None]
    dist = qi + Q_BLOCK - kj
    max_exact = N_BUCKETS // 2
    d = np.maximum(dist, 0)
    df = np.maximum(d, 1).astype(np.float32)
    large = max_exact + (np.log(df / np.float32(max_exact))
                         / np.float32(math.log(MAX_DISTANCE / max_exact))
                         * np.float32(N_BUCKETS - max_exact)).astype(np.int32)
    large = np.minimum(large, N_BUCKETS - 1)
    bucket = np.where(d < max_exact, d, large).astype(np.int32)
    valid = ((dist >= 0) & (dist <= WINDOW)).astype(np.int32)
    return bucket, valid


def _bias_kernel(rb_ref, bucket_ref, valid_ref, o_ref):
    bucket = bucket_ref[...]
    row = lax.broadcasted_iota(jnp.int32, bucket.shape, 0)
    keep = valid_ref[...] != 0
    keep_first = keep & (row >= Q_BLOCK)

    def head(h, carry):
        acc = jnp.zeros(bucket.shape, F32)
        for b in range(N_BUCKETS):
            acc = jnp.where(bucket == b, rb_ref[b, h], acc)
        o_ref[0, h] = jnp.where(keep, acc, NEG)
        o_ref[1, h] = jnp.where(keep_first, acc, NEG)
        return carry

    lax.fori_loop(0, N_HEADS, head, 0)


def _bias_table(rel_bias):
    bucket, valid = _bucket_table()
    shp = (2 * Q_BLOCK, Q_BLOCK)
    return pl.pallas_call(
        _bias_kernel,
        grid=(1,),
        in_specs=[pl.BlockSpec(memory_space=pltpu.SMEM),
                  pl.BlockSpec(shp, lambda v: (0, 0)),
                  pl.BlockSpec(shp, lambda v: (0, 0))],
        out_specs=pl.BlockSpec((2, N_HEADS) + shp, lambda v: (0, 0, 0, 0)),
        out_shape=jax.ShapeDtypeStruct((2, N_HEADS) + shp, F32),
        compiler_params=_cparams("arbitrary"),
        name="rel_bias_table",
    )(rel_bias, jnp.asarray(bucket), jnp.asarray(valid))


def _attn_prompt_kernel(sink_ref, q_ref, kp_ref, kc_ref, vp_ref, vc_ref, bias_ref, o_ref):
    scale = HEAD_DIM ** -0.5

    def scores(kv):
        ks = slice(kv * HEAD_DIM, (kv + 1) * HEAD_DIM)
        heads = range(kv * GROUP, (kv + 1) * GROUP)
        kband = jnp.concatenate([kp_ref[:, ks], kc_ref[:, ks]], axis=0)
        qg = jnp.concatenate(
            [q_ref[:, h * HEAD_DIM:(h + 1) * HEAD_DIM] for h in heads], axis=0) * scale
        st = lax.dot_general(kband, qg, (((1,), (1,)), ((), ())),
                             preferred_element_type=F32)
        return st + jnp.concatenate([bias_ref[0, h] for h in heads], axis=1)

    def finish(kv, st):
        ks = slice(kv * HEAD_DIM, (kv + 1) * HEAD_DIM)
        heads = range(kv * GROUP, (kv + 1) * GROUP)
        vband = jnp.concatenate([vp_ref[:, ks], vc_ref[:, ks]], axis=0)
        sk = jnp.concatenate([jnp.full((1, Q_BLOCK), sink_ref[h], F32) for h in heads], axis=1)
        m = jnp.maximum(jnp.max(st, axis=0, keepdims=True), sk)
        e = jnp.exp(st - m)
        denom = jnp.sum(e, axis=0, keepdims=True) + jnp.exp(sk - m)
        ot = lax.dot_general(vband, e.astype(BF16), (((0,), (0,)), ((), ())),
                             preferred_element_type=F32) * (1.0 / denom)
        o2 = jnp.concatenate(
            [ot[:, g * Q_BLOCK:(g + 1) * Q_BLOCK] for g in range(GROUP)], axis=0)
        gs = slice(kv * GROUP * HEAD_DIM, (kv + 1) * GROUP * HEAD_DIM)
        o_ref[:, gs] = o2.T.astype(o_ref.dtype)

    st = scores(0)
    for kv in range(N_KV_HEADS):
        st_next = scores(kv + 1) if kv + 1 < N_KV_HEADS else None
        finish(kv, st)
        st = st_next


def _attn_prompt(proj_p, bias_tab, sinks):
    nblk = SEQ // Q_BLOCK
    qcol = OFF_Q // ATTN_W
    kcol = OFF_K // KV_W
    vcol = OFF_V // KV_W

    def cur(b, n):
        return b * nblk + n

    def prev(b, n):
        return b * nblk + jnp.maximum(n - 1, 0)

    return pl.pallas_call(
        _attn_prompt_kernel,
        grid=(BATCH, nblk),
        in_specs=[pl.BlockSpec(memory_space=pltpu.SMEM),
                  pl.BlockSpec((Q_BLOCK, ATTN_W), lambda b, n: (cur(b, n), qcol)),
                  pl.BlockSpec((Q_BLOCK, KV_W), lambda b, n: (prev(b, n), kcol)),
                  pl.BlockSpec((Q_BLOCK, KV_W), lambda b, n: (cur(b, n), kcol)),
                  pl.BlockSpec((Q_BLOCK, KV_W), lambda b, n: (prev(b, n), vcol)),
                  pl.BlockSpec((Q_BLOCK, KV_W), lambda b, n: (cur(b, n), vcol)),
                  pl.BlockSpec((1, N_HEADS, 2 * Q_BLOCK, Q_BLOCK),
                               lambda b, n: (jnp.where(n == 0, 1, 0), 0, 0, 0))],
        out_specs=pl.BlockSpec((Q_BLOCK, ATTN_W), lambda b, n: (cur(b, n), 0)),
        out_shape=jax.ShapeDtypeStruct((M_PROMPT, ATTN_W), BF16),
        compiler_params=_cparams("parallel", "arbitrary"),
        name="attn_prompt",
    )(sinks, proj_p, proj_p, proj_p, proj_p, proj_p, bias_tab)


SAMPLES_PER_STEP = 8


def _attn_sample_kernel(q_ref, kn_ref, vn_ref, kb_ref, vb_ref, bias_ref, sink_ref,
                        o_ref, ko_ref, vo_ref):
    g = SAMPLES_PER_STEP
    scale = HEAD_DIM ** -0.5
    erow = lax.broadcasted_iota(jnp.int32, (HEAD_DIM, KV_W), 0)
    ecol = lax.broadcasted_iota(jnp.int32, (HEAD_DIM, KV_W), 1)
    expand = (ecol % HEAD_DIM == erow).astype(BF16)
    hrow = lax.broadcasted_iota(jnp.int32, (N_HEADS, KV_W), 0)
    hcol = lax.broadcasted_iota(jnp.int32, (N_HEADS, KV_W), 1)
    own = ((hrow // GROUP) == (hcol // HEAD_DIM))[None]
    bias_w = bias_ref[:, 0:WINDOW][None]
    bias_n = bias_ref[:, WINDOW:WINDOW + 1][None]
    sk = sink_ref[...][None]
    contract_last = (((1,), (1,)), ((), ()))

    q_all = q_ref[...].reshape(g * N_HEADS, HEAD_DIM) * scale
    qrow = jnp.where(own, _dot(q_all, expand).reshape(g, N_HEADS, KV_W), 0.0)
    qrow_b = qrow.astype(BF16)
    kn = kn_ref[...][:, None, :]
    vn = vn_ref[...][:, None, :]
    s_w = jnp.stack([lax.dot_general(qrow_b[b], kb_ref[b].astype(BF16), contract_last,
                                     preferred_element_type=F32) for b in range(g)])
    s_w = s_w + bias_w
    s_n = jnp.sum(qrow * kn, axis=-1, keepdims=True) + bias_n
    m = jnp.maximum(jnp.maximum(jnp.max(s_w, axis=-1, keepdims=True), s_n), sk)
    e_w = jnp.exp(s_w - m)
    e_n = jnp.exp(s_n - m)
    r = 1.0 / (jnp.sum(e_w, axis=-1, keepdims=True) + e_n + jnp.exp(sk - m))
    p_w = (e_w * r).astype(BF16)
    o_all = jnp.stack([_dot(p_w[b], vb_ref[b].astype(BF16)) for b in range(g)])
    o_own = jnp.where(own, o_all + (e_n * r) * vn, 0.0)
    o = o_own[:, :, 0:HEAD_DIM]
    for kv in range(1, N_KV_HEADS):
        o = o + o_own[:, :, kv * HEAD_DIM:(kv + 1) * HEAD_DIM]
    o_ref[...] = o.astype(o_ref.dtype)
    for b in range(g):
        ko_ref[b] = jnp.concatenate([kb_ref[b, 1:, :], kn_ref[b:b + 1, :]], axis=0)
        vo_ref[b] = jnp.concatenate([vb_ref[b, 1:, :], vn_ref[b:b + 1, :]], axis=0)


def _attn_sample(q3, k_new, v_new, k_buf, v_buf, bias_s, sinks):
    g = SAMPLES_PER_STEP
    nb = DEC_BATCH // g
    win = pl.BlockSpec((g, WINDOW, KV_W), lambda i: (i, 0, 0))
    row = pl.BlockSpec((g, KV_W), lambda i: (i, 0))
    return pl.pallas_call(
        _attn_sample_kernel,
        grid=(nb,),
        in_specs=[pl.BlockSpec((g, N_HEADS, HEAD_DIM), lambda i: (i, 0, 0)),
                  row, row, win, win,
                  pl.BlockSpec((N_HEADS, WINDOW + 1), lambda i: (0, 0)),
                  pl.BlockSpec((N_HEADS, 1), lambda i: (0, 0))],
        out_specs=[pl.BlockSpec((g, N_HEADS, HEAD_DIM), lambda i: (i, 0, 0)), win, win],
        out_shape=[jax.ShapeDtypeStruct((DEC_BATCH, N_HEADS, HEAD_DIM), BF16),
                   jax.ShapeDtypeStruct((DEC_BATCH, WINDOW, KV_W), F32),
                   jax.ShapeDtypeStruct((DEC_BATCH, WINDOW, KV_W), F32)],
        compiler_params=_cparams("parallel"),
        name="attn_sample",
    )(q3, k_new, v_new, k_buf, v_buf, bias_s, sinks.reshape(N_HEADS, 1))


def _merge_kernel(ap_ref, bp_ref, as_ref, bs_ref, waf_ref, wbf_ref,
                  gap_ref, gbp_ref, gas_ref, gbs_ref, op_ref, os_ref,
                  wa0_ref, wb0_ref, wa1_ref, wb1_ref):
    def compute(slot):
        wa_ref, wb_ref = slot

        def merged(a_ref, b_ref, ga_ref, gb_ref):
            return (_sigmoid(ga_ref[...].astype(F32)) * _dot(a_ref[...], wa_ref[...])
                    + _sigmoid(gb_ref[...].astype(F32)) * _dot(b_ref[...], wb_ref[...]))

        op_ref[...] = merged(ap_ref, bp_ref, gap_ref, gbp_ref).astype(op_ref.dtype)

        @pl.when(_is_last_row_tile())
        def _():
            os_ref[...] = merged(as_ref, bs_ref, gas_ref, gbs_ref).astype(os_ref.dtype)

    _ws_phases([waf_ref, wbf_ref], [wa0_ref, wb0_ref], [wa1_ref, wb1_ref], compute)


def _branch_merge(a_p, o_p, a_s, o_s, wa, wb, proj_p, proj_s, bm, bn):
    mp, k = a_p.shape
    ms = a_s.shape[0]
    n = wa.shape[1]
    grid = _ws_grid(n, bn, mp, bm)
    ga0 = OFF_GA // bn
    gb0 = OFF_GB // bn
    wchunk = pl.BlockSpec((k // grid[1], bn), _ws_wchunk(grid[0] - 1))
    wslot = pltpu.VMEM((k, bn), BF16)
    return pl.pallas_call(
        _merge_kernel,
        grid=grid,
        in_specs=[pl.BlockSpec((bm, k), _ws_rows),
                  pl.BlockSpec((bm, k), _ws_rows),
                  pl.BlockSpec((ms, k), lambda j, i: (0, 0)),
                  pl.BlockSpec((ms, k), lambda j, i: (0, 0)),
                  wchunk, wchunk,
                  pl.BlockSpec((bm, bn), _ws_tile(ga0)),
                  pl.BlockSpec((bm, bn), _ws_tile(gb0)),
                  pl.BlockSpec((ms, bn), _ws_stile(ga0)),
                  pl.BlockSpec((ms, bn), _ws_stile(gb0))],
        out_specs=[pl.BlockSpec((bm, bn), _ws_tile()),
                   pl.BlockSpec((ms, bn), _ws_stile())],
        out_shape=[jax.ShapeDtypeStruct((mp, n), BF16),
                   jax.ShapeDtypeStruct((ms, n), BF16)],
        scratch_shapes=[wslot, wslot, wslot, wslot],
        compiler_params=_cparams("arbitrary", "arbitrary"),
        name="branch_merge",
    )(a_p, o_p, a_s, o_s, wa, wb, proj_p, proj_p, proj_s, proj_s)


def _resid_kernel(xp_ref, xs_ref, wf_ref, rp_ref, rs_ref, op_ref, os_ref, w0_ref, w1_ref):
    def compute(slot):
        (w_ref,) = slot
        op_ref[...] = rp_ref[...] + _dot(xp_ref[...], w_ref[...])

        @pl.when(_is_last_row_tile())
        def _():
            os_ref[...] = rs_ref[...] + _dot(xs_ref[...], w_ref[...])

    _ws_phases([wf_ref], [w0_ref], [w1_ref], compute)


def _resid_matmul(xp, xs, w, rp, rs, bm, bn, name):
    mp, k = xp.shape
    ms = xs.shape[0]
    n = w.shape[1]
    grid = _ws_grid(n, bn, mp, bm)
    return pl.pallas_call(
        _resid_kernel,
        grid=grid,
        in_specs=[pl.BlockSpec((bm, k), _ws_rows),
                  pl.BlockSpec((ms, k), lambda j, i: (0, 0)),
                  pl.BlockSpec((k // grid[1], bn), _ws_wchunk(grid[0] - 1)),
                  pl.BlockSpec((bm, bn), _ws_tile()),
                  pl.BlockSpec((ms, bn), _ws_stile())],
        out_specs=[pl.BlockSpec((bm, bn), _ws_tile()),
                   pl.BlockSpec((ms, bn), _ws_stile())],
        out_shape=[jax.ShapeDtypeStruct((mp, n), F32),
                   jax.ShapeDtypeStruct((ms, n), F32)],
        scratch_shapes=[pltpu.VMEM((k, bn), BF16), pltpu.VMEM((k, bn), BF16)],
        compiler_params=_cparams("arbitrary", "arbitrary"),
        name=name,
    )(xp, xs, w, rp, rs)


LANES = 128


def _lane_partials(v):
    part = v[:, 0:LANES]
    for c in range(1, v.shape[1] // LANES):
        part = part + v[:, c * LANES:(c + 1) * LANES]
    return part


def _outproj_kernel(xp_ref, xs_ref, wf_ref, rp_ref, rs_ref, g_ref,
                    op_ref, os_ref, bp_ref, bs_ref, ssp_ref, sss_ref,
                    w0_ref, w1_ref, accp_ref, accs_ref):
    j = pl.program_id(0)
    i = pl.program_id(1)
    bm = xp_ref.shape[0]

    def compute(slot):
        (w_ref,) = slot

        def tile(x_ref, r_ref, o_ref, b_ref, acc_view, ss_ref):
            x1 = r_ref[...] + _dot(x_ref[...], w_ref[...])
            o_ref[...] = x1
            b_ref[...] = (x1 * g_ref[...]).astype(b_ref.dtype)
            total = jnp.where(j == 1, 0.0, acc_view[...]) + _lane_partials(x1 * x1)
            acc_view[...] = total
            ss_ref[...] = total

        rows = pl.ds(pl.multiple_of(i * bm, bm), bm)
        tile(xp_ref, rp_ref, op_ref, bp_ref, accp_ref.at[rows, :], ssp_ref)

        @pl.when(_is_last_row_tile())
        def _():
            tile(xs_ref, rs_ref, os_ref, bs_ref, accs_ref, sss_ref)

    _ws_phases([wf_ref], [w0_ref], [w1_ref], compute)


def _out_proj(xp, xs, w, rp, rs, gain, bm, bn):
    mp, k = xp.shape
    ms = xs.shape[0]
    n = w.shape[1]
    grid = _ws_grid(n, bn, mp, bm)
    last = grid[0] - 1
    ssp_map = lambda j, i: (jnp.where(j == last, i, 0), 0)
    return pl.pallas_call(
        _outproj_kernel,
        grid=grid,
        in_specs=[pl.BlockSpec((bm, k), _ws_rows),
                  pl.BlockSpec((ms, k), lambda j, i: (0, 0)),
                  pl.BlockSpec((k // grid[1], bn), _ws_wchunk(grid[0] - 1)),
                  pl.BlockSpec((bm, bn), _ws_tile()),
                  pl.BlockSpec((ms, bn), _ws_stile()),
                  pl.BlockSpec((1, bn), _ws_stile())],
        out_specs=[pl.BlockSpec((bm, bn), _ws_tile()),
                   pl.BlockSpec((ms, bn), _ws_stile()),
                   pl.BlockSpec((bm, bn), _ws_tile()),
                   pl.BlockSpec((ms, bn), _ws_stile()),
                   pl.BlockSpec((bm, LANES), ssp_map),
                   pl.BlockSpec((ms, LANES), lambda j, i: (0, 0))],
        out_shape=[jax.ShapeDtypeStruct((mp, n), F32),
                   jax.ShapeDtypeStruct((ms, n), F32),
                   jax.ShapeDtypeStruct((mp, n), BF16),
                   jax.ShapeDtypeStruct((ms, n), BF16),
                   jax.ShapeDtypeStruct((mp, LANES), F32),
                   jax.ShapeDtypeStruct((ms, LANES), F32)],
        scratch_shapes=[pltpu.VMEM((k, bn), BF16), pltpu.VMEM((k, bn), BF16),
                        pltpu.VMEM((mp, LANES), F32), pltpu.VMEM((ms, LANES), F32)],
        compiler_params=_cparams("arbitrary", "arbitrary"),
        name="out_proj",
    )(xp, xs, w, rp, rs, gain.reshape(1, n))


def _inv_rms(ss_ref):
    return lax.rsqrt(jnp.sum(ss_ref[...], axis=-1, keepdims=True) * (1.0 / D_MODEL) + EPS)


def _ffn_up_kernel(hp_ref, hs_ref, ssp_ref, sss_ref, wgf_ref, wuf_ref, cw_ref, cb_ref,
                   s0_ref, s1_ref, fp_ref, fs_ref, gs_ref, st_ref,
                   wg0_ref, wu0_ref, wg1_ref, wu1_ref, carry_ref, gbuf_ref, *, tiles_per_seq):
    i = pl.program_id(1)
    bm = hp_ref.shape[0]

    def compute(slot):
        wg_ref, wu_ref = slot
        w0 = cw_ref[0:1, :]
        w1 = cw_ref[1:2, :]
        w2 = cw_ref[2:3, :]
        bias = cb_ref[...]

        @pl.when(i % tiles_per_seq == 0)
        def _():
            carry_ref[...] = jnp.zeros(carry_ref.shape, F32)

        r = _inv_rms(ssp_ref)
        g = r * _dot(hp_ref[...], wg_ref[...])
        gbuf_ref[0:SUBLANES, :] = carry_ref[...]
        gbuf_ref[SUBLANES:SUBLANES + bm, :] = g
        g1 = gbuf_ref[SUBLANES - 1:SUBLANES - 1 + bm, :]
        g2 = gbuf_ref[SUBLANES - 2:SUBLANES - 2 + bm, :]
        gc = w0 * g2 + w1 * g1 + w2 * g + bias
        up = r * _dot(hp_ref[...], wu_ref[...])
        fp_ref[...] = (gc * _sigmoid(gc) * up).astype(fp_ref.dtype)
        tail = gbuf_ref[bm:SUBLANES + bm, :]
        carry_ref[...] = tail
        st_ref[0] = tail

        @pl.when(_is_last_row_tile())
        def _():
            rs = _inv_rms(sss_ref)
            gsm = rs * _dot(hs_ref[...], wg_ref[...])
            ups = rs * _dot(hs_ref[...], wu_ref[...])
            gcs = w0 * s0_ref[...] + w1 * s1_ref[...] + w2 * gsm + bias
            fs_ref[...] = (gcs * _sigmoid(gcs) * ups).astype(fs_ref.dtype)
            gs_ref[...] = gsm

    _ws_phases([wgf_ref, wuf_ref], [wg0_ref, wu0_ref], [wg1_ref, wu1_ref], compute)


def _ffn_up(hp, hs, ssp, sss, wg, wu, cw, cb, s0, s1, bm, bn):
    mp, k = hp.shape
    ms = hs.shape[0]
    n = wg.shape[1]
    grid = _ws_grid(n, bn, mp, bm)
    tiles_per_seq = SEQ // bm
    wchunk = pl.BlockSpec((k // grid[1], bn), _ws_wchunk(grid[0] - 1))
    wslot = pltpu.VMEM((k, bn), BF16)
    col = _ws_stile()

    def tail_map(j, i):
        return (jnp.where(j > 0, i, 0) // tiles_per_seq, 0, jnp.maximum(j - 1, 0))

    return pl.pallas_call(
        functools.partial(_ffn_up_kernel, tiles_per_seq=tiles_per_seq),
        grid=grid,
        in_specs=[pl.BlockSpec((bm, k), _ws_rows),
                  pl.BlockSpec((ms, k), lambda j, i: (0, 0)),
                  pl.BlockSpec((bm, LANES), _ws_rows),
                  pl.BlockSpec((ms, LANES), lambda j, i: (0, 0)),
                  wchunk, wchunk,
                  pl.BlockSpec((FFN_CONV_K, bn), col),
                  pl.BlockSpec((1, bn), col),
                  pl.BlockSpec((ms, bn), col),
                  pl.BlockSpec((ms, bn), col)],
        out_specs=[pl.BlockSpec((bm, bn), _ws_tile()),
                   pl.BlockSpec((ms, bn), col),
                   pl.BlockSpec((ms, bn), col),
                   pl.BlockSpec((1, SUBLANES, bn), tail_map)],
        out_shape=[jax.ShapeDtypeStruct((mp, n), BF16),
                   jax.ShapeDtypeStruct((ms, n), BF16),
                   jax.ShapeDtypeStruct((ms, n), F32),
                   jax.ShapeDtypeStruct((BATCH, SUBLANES, n), F32)],
        scratch_shapes=[wslot, wslot, wslot, wslot,
                        pltpu.VMEM((SUBLANES, bn), F32),
                        pltpu.VMEM((SUBLANES + bm, bn), F32)],
        compiler_params=_cparams("arbitrary", "arbitrary"),
        name="ffn_up",
    )(hp, hs, ssp, sss, wg, wu, cw, cb, s0, s1)


def kernel(x_prompt, x_sample, state_k_window, state_v_window, state_conv, state_ffn_conv,
           attn_norm_g, w_in, conv_w, w_branch_a, w_branch_b, sinks, w_out, ffn_norm_g,
           w_ffn_gate, w_ffn_up, ffn_conv_w, ffn_conv_b, w_ffn_down, rel_bias,
           final_norm_g):
    xp = x_prompt.reshape(M_PROMPT, D_MODEL)
    xs = x_sample.reshape(DEC_BATCH, D_MODEL)

    fs0 = state_ffn_conv[0][:, 0, :]
    fs1 = state_ffn_conv[0][:, 1, :]

    cs0 = state_conv[0][:, 0, :]
    cs1 = state_conv[0][:, 1, :]
    hs = _rmsnorm(xs, attn_norm_g[0], BF16, DEC_BATCH)
    a_p, a_s, u_s, u_tail, hp, hssp = _mixer_a(
        xp, hs, attn_norm_g[0], w_in[0], conv_w[0], cs0, cs1, 512, 512)
    proj_p, proj_s = _in_proj(hp, hs, hssp, w_in[0], W_REST, PROJ_W, 1024, 1024)

    bias_tab = _bias_table(rel_bias)
    o_p = _attn_prompt(proj_p, bias_tab, sinks[0])
    q3 = proj_s[:, OFF_Q:OFF_Q + ATTN_W].reshape(DEC_BATCH, N_HEADS, HEAD_DIM)
    k_new = proj_s[:, OFF_K:OFF_K + KV_W].astype(F32)
    v_new = proj_s[:, OFF_V:OFF_V + KV_W].astype(F32)
    bias_s = bias_tab[0, :, Q_BLOCK - 1:, Q_BLOCK - 1]
    o_s3, k_win_s, v_win_s = _attn_sample(
        q3, k_new, v_new,
        state_k_window[0].reshape(DEC_BATCH, WINDOW, KV_W),
        state_v_window[0].reshape(DEC_BATCH, WINDOW, KV_W), bias_s, sinks[0])
    o_s = o_s3.reshape(DEC_BATCH, ATTN_W)

    mg_p, mg_s = _branch_merge(a_p, o_p, a_s, o_s, w_branch_a[0], w_branch_b[0],
                               proj_p, proj_s, 1024, 1024)
    x1p, x1s, xbp, xbs, ssp, sss = _out_proj(mg_p, mg_s, w_out[0], xp, xs, ffn_norm_g[0],
                                             512, 1024)

    f_p, f_s, g_s, g_tail = _ffn_up(xbp, xbs, ssp, sss, w_ffn_gate[0], w_ffn_up[0],
                                    ffn_conv_w[0], ffn_conv_b, fs0, fs1, 1024, 512)
    x2p, x2s = _resid_matmul(f_p, f_s, w_ffn_down[0], x1p, x1s, 512, 512, "ffn_down")
    y_p = _rmsnorm(x2p, final_norm_g, F32, 256)
    y_s = _rmsnorm(x2s, final_norm_g, F32, DEC_BATCH)

    kv_p = proj_p.reshape(BATCH, SEQ, PROJ_W)[:, SEQ - WINDOW:, OFF_K:OFF_GA].astype(F32)
    k_win_p = kv_p[:, :, :KV_W].reshape(1, BATCH, WINDOW, N_KV_HEADS, HEAD_DIM)
    v_win_p = kv_p[:, :, KV_W:].reshape(1, BATCH, WINDOW, N_KV_HEADS, HEAD_DIM)
    conv_p = u_tail[:, SUBLANES - (CONV_K - 1):, :]
    ffn_p = g_tail[:, SUBLANES - (FFN_CONV_K - 1):, :]
    conv_s = jnp.stack([cs1, u_s], axis=1)
    ffn_s = jnp.stack([fs1, g_s], axis=1)

    return (y_p.reshape(BATCH, SEQ, D_MODEL),
            y_s.reshape(DEC_BATCH, 1, D_MODEL),
            k_win_p, v_win_p, conv_p[None], ffn_p[None],
            k_win_s.reshape(1, DEC_BATCH, WINDOW, N_KV_HEADS, HEAD_DIM),
            v_win_s.reshape(1, DEC_BATCH, WINDOW, N_KV_HEADS, HEAD_DIM),
            conv_s[None], ffn_s[None])
```

```python
import functools
import math

import numpy as np
import jax
import jax.numpy as jnp
from jax import lax
from jax.experimental import pallas as pl
from jax.experimental.pallas import tpu as pltpu

F32 = jnp.float32
BF16 = jnp.bfloat16

D_MODEL = 4096
BATCH = 4
SEQ = 2048
DEC_BATCH = 128
N_HEADS = 32
N_KV_HEADS = 8
HEAD_DIM = 64
GROUP = N_HEADS // N_KV_HEADS
ATTN_W = N_HEADS * HEAD_DIM
KV_W = N_KV_HEADS * HEAD_DIM
CONV_W = D_MODEL // 2
CONV_K = 3
WINDOW = 128
Q_BLOCK = 128
N_BUCKETS = 32
MAX_DISTANCE = 128
D_FF = 11008
FFN_CONV_K = 3
EPS = 1e-5
NEG = -1e30
IN_W = 3 * CONV_W + ATTN_W + 2 * KV_W + 2 * D_MODEL
M_PROMPT = BATCH * SEQ

W_CB = 0
W_CC = CONV_W
W_CH = 2 * CONV_W
W_REST = 3 * CONV_W
OFF_Q = 0
OFF_K = OFF_Q + ATTN_W
OFF_V = OFF_K + KV_W
OFF_GA = OFF_V + KV_W
OFF_GB = OFF_GA + D_MODEL
PROJ_W = OFF_GB + D_MODEL

V7X_VMEM_BYTES = 64 * 1024 * 1024
VMEM_LIMIT = V7X_VMEM_BYTES - 1024 * 1024
SUBLANES = 8


def _cparams(*sem):
    return pltpu.CompilerParams(dimension_semantics=sem, vmem_limit_bytes=VMEM_LIMIT)


def _dot(a, b):
    return jnp.dot(a, b, preferred_element_type=F32)


def _sigmoid(x):
    return 1.0 / (1.0 + jnp.exp(-x))


def _ws_grid(n, bn, mp, bm):
    return (pl.cdiv(n, bn) + 1, mp // bm)


def _ws_rows(j, i):
    return (jnp.where(j > 0, i, 0), 0)


def _ws_wchunk(nj, col0=0):
    return lambda j, i: (i, col0 + jnp.minimum(j, nj - 1))


def _ws_tile(col0=0):
    return lambda j, i: (jnp.where(j > 0, i, 0), col0 + jnp.maximum(j - 1, 0))


def _ws_stile(col0=0):
    return lambda j, i: (0, col0 + jnp.maximum(j - 1, 0))


def _ws_cast(wf_refs, dst_refs):
    i = pl.program_id(1)
    for wf_ref, dst_ref in zip(wf_refs, dst_refs):
        rows = wf_ref.shape[0]
        r0 = pl.multiple_of(i * rows, rows)
        dst_ref[pl.ds(r0, rows), :] = wf_ref[...].astype(BF16)


def _ws_phases(wf_refs, slot0, slot1, compute):
    j = pl.program_id(0)

    @pl.when(j == 0)
    def _():
        _ws_cast(wf_refs, slot0)

    @pl.when((j > 0) & (j % 2 == 1))
    def _():
        _ws_cast(wf_refs, slot1)
        compute(slot0)

    @pl.when((j > 0) & (j % 2 == 0))
    def _():
        _ws_cast(wf_refs, slot0)
        compute(slot1)


def _is_last_row_tile():
    return pl.program_id(1) == pl.num_programs(1) - 1


def _rms_kernel(x_ref, g_ref, o_ref):
    x = x_ref[...]
    r = lax.rsqrt(jnp.mean(x * x, axis=-1, keepdims=True) + EPS)
    o_ref[...] = ((x * r) * g_ref[...]).astype(o_ref.dtype)


def _rmsnorm(x, g, out_dtype, block_rows):
    rows, d = x.shape
    return pl.pallas_call(
        _rms_kernel,
        grid=(rows // block_rows,),
        in_specs=[pl.BlockSpec((block_rows, d), lambda i: (i, 0)),
                  pl.BlockSpec((1, d), lambda i: (0, 0))],
        out_specs=pl.BlockSpec((block_rows, d), lambda i: (i, 0)),
        out_shape=jax.ShapeDtypeStruct((rows, d), out_dtype),
        compiler_params=_cparams("parallel"),
        name="rmsnorm",
    )(x, g.reshape(1, d))


def _proj_kernel(xp_ref, xs_ref, ssp_ref, wf_ref, op_ref, os_ref, w0_ref, w1_ref):
    def compute(slot):
        (w_ref,) = slot
        op_ref[...] = (_inv_rms(ssp_ref) * _dot(xp_ref[...], w_ref[...])).astype(op_ref.dtype)

        @pl.when(_is_last_row_tile())
        def _():
            os_ref[...] = _dot(xs_ref[...], w_ref[...]).astype(os_ref.dtype)

    _ws_phases([wf_ref], [w0_ref], [w1_ref], compute)


def _in_proj(hp, hs, ssp, w, mp, col0, n, bm, bn):
    k = hp.shape[1]
    ms = hs.shape[0]
    grid = _ws_grid(n, bn, mp, bm)
    return pl.pallas_call(
        _proj_kernel,
        grid=grid,
        in_specs=[pl.BlockSpec((bm, k), _ws_rows),
                  pl.BlockSpec((ms, k), lambda j, i: (0, 0)),
                  pl.BlockSpec((bm, LANES), _ws_rows),
                  pl.BlockSpec((k // grid[1], bn), _ws_wchunk(grid[0] - 1, col0 // bn))],
        out_specs=[pl.BlockSpec((bm, bn), _ws_tile()),
                   pl.BlockSpec((ms, bn), _ws_stile())],
        out_shape=[jax.ShapeDtypeStruct((mp, n), BF16),
                   jax.ShapeDtypeStruct((ms, n), BF16)],
        scratch_shapes=[pltpu.VMEM((k, bn), BF16), pltpu.VMEM((k, bn), BF16)],
        compiler_params=_cparams("arbitrary", "arbitrary"),
        name="in_proj",
    )(hp, hs, ssp, w)


def _mixa_kernel(xp_ref, hs_ref, g_ref, wbf_ref, wcf_ref, whf_ref, cw_ref, s0_ref, s1_ref,
                 ap_ref, as_ref, us_ref, st_ref, hp_ref, ssp_ref,
                 wb0_ref, wc0_ref, wh0_ref, wb1_ref, wc1_ref, wh1_ref, carry_ref, ubuf_ref,
                 *, tiles_per_seq):
    i = pl.program_id(1)
    bm = xp_ref.shape[0]

    def compute(slot):
        wb_ref, wc_ref, wh_ref = slot
        w0 = cw_ref[0:1, :]
        w1 = cw_ref[1:2, :]
        w2 = cw_ref[2:3, :]

        @pl.when(i % tiles_per_seq == 0)
        def _():
            carry_ref[...] = jnp.zeros(carry_ref.shape, F32)

        hp_ref[...] = (xp_ref[...] * g_ref[...]).astype(hp_ref.dtype)
        ssp_ref[...] = _lane_partials(xp_ref[...] * xp_ref[...])
        r = _inv_rms(ssp_ref)
        u = (r * _dot(hp_ref[...], wc_ref[...])) * (r * _dot(hp_ref[...], wh_ref[...]))
        ubuf_ref[0:SUBLANES, :] = carry_ref[...]
        ubuf_ref[SUBLANES:SUBLANES + bm, :] = u
        u1 = ubuf_ref[SUBLANES - 1:SUBLANES - 1 + bm, :]
        u2 = ubuf_ref[SUBLANES - 2:SUBLANES - 2 + bm, :]
        z = w0 * u2 + w1 * u1 + w2 * u
        ap_ref[...] = ((r * _dot(hp_ref[...], wb_ref[...])) * z).astype(ap_ref.dtype)
        tail = ubuf_ref[bm:SUBLANES + bm, :]
        carry_ref[...] = tail
        st_ref[0] = tail

        @pl.when(_is_last_row_tile())
        def _():
            us = _dot(hs_ref[...], wc_ref[...]) * _dot(hs_ref[...], wh_ref[...])
            zs = w0 * s0_ref[...] + w1 * s1_ref[...] + w2 * us
            as_ref[...] = (_dot(hs_ref[...], wb_ref[...]) * zs).astype(as_ref.dtype)
            us_ref[...] = us

    _ws_phases([wbf_ref, wcf_ref, whf_ref], [wb0_ref, wc0_ref, wh0_ref],
               [wb1_ref, wc1_ref, wh1_ref], compute)


def _mixer_a(xp, hs, gain, w_in, conv_w, s0, s1, bm, bn):
    mp, k = xp.shape
    ms = hs.shape[0]
    n = CONV_W
    grid = _ws_grid(n, bn, mp, bm)
    nj = grid[0] - 1
    tiles_per_seq = SEQ // bm
    wslot = pltpu.VMEM((k, bn), BF16)
    col = _ws_stile()

    def wchunk(col0):
        return pl.BlockSpec((k // grid[1], bn), _ws_wchunk(nj, col0 // bn))

    def tail_map(j, i):
        return (jnp.where(j > 0, i, 0) // tiles_per_seq, 0, jnp.maximum(j - 1, 0))

    spare = grid[1]

    def operand_map(j, i):
        return (jnp.where(j == 1, i, jnp.where(j == 0, 0, spare)), 0)

    return pl.pallas_call(
        functools.partial(_mixa_kernel, tiles_per_seq=tiles_per_seq),
        grid=grid,
        in_specs=[pl.BlockSpec((bm, k), _ws_rows),
                  pl.BlockSpec((ms, k), lambda j, i: (0, 0)),
                  pl.BlockSpec((1, k), lambda j, i: (0, 0)),
                  wchunk(W_CB), wchunk(W_CC), wchunk(W_CH),
                  pl.BlockSpec((CONV_K, bn), col),
                  pl.BlockSpec((ms, bn), col),
                  pl.BlockSpec((ms, bn), col)],
        out_specs=[pl.BlockSpec((bm, bn), _ws_tile()),
                   pl.BlockSpec((ms, bn), col),
                   pl.BlockSpec((ms, bn), col),
                   pl.BlockSpec((1, SUBLANES, bn), tail_map),
                   pl.BlockSpec((bm, k), operand_map),
                   pl.BlockSpec((bm, LANES), operand_map)],
        out_shape=[jax.ShapeDtypeStruct((mp, n), BF16),
                   jax.ShapeDtypeStruct((ms, n), BF16),
                   jax.ShapeDtypeStruct((ms, n), F32),
                   jax.ShapeDtypeStruct((BATCH, SUBLANES, n), F32),
                   jax.ShapeDtypeStruct((mp + bm, k), BF16),
                   jax.ShapeDtypeStruct((mp + bm, LANES), F32)],
        scratch_shapes=[wslot] * 6 + [pltpu.VMEM((SUBLANES, bn), F32),
                                      pltpu.VMEM((SUBLANES + bm, bn), F32)],
        compiler_params=_cparams("arbitrary", "arbitrary"),
        name="mixer_a",
    )(xp, hs, gain.reshape(1, k), w_in, w_in, w_in, conv_w, s0, s1)


def _bucket_table():
    qi = np.arange(Q_BLOCK)[None, :]
    kj = np.arange(2 * Q_BLOCK)[:, None]
    dist = qi + Q_BLOCK - kj
    max_exact = N_BUCKETS // 2
    d = np.maximum(dist, 0)
    df = np.maximum(d, 1).astype(np.float32)
    large = max_exact + (np.log(df / np.float32(max_exact))
                         / np.float32(math.log(MAX_DISTANCE / max_exact))
                         * np.float32(N_BUCKETS - max_exact)).astype(np.int32)
    large = np.minimum(large, N_BUCKETS - 1)
    bucket = np.where(d < max_exact, d, large).astype(np.int32)
    valid = ((dist >= 0) & (dist <= WINDOW)).astype(np.int32)
    return bucket, valid


def _bias_kernel(rb_ref, bucket_ref, valid_ref, o_ref):
    bucket = bucket_ref[...]
    row = lax.broadcasted_iota(jnp.int32, bucket.shape, 0)
    keep = valid_ref[...] != 0
    keep_first = keep & (row >= Q_BLOCK)

    def head(h, carry):
        acc = jnp.zeros(bucket.shape, F32)
        for b in range(N_BUCKETS):
            acc = jnp.where(bucket == b, rb_ref[b, h], acc)
        o_ref[0, h] = jnp.where(keep, acc, NEG)
        o_ref[1, h] = jnp.where(keep_first, acc, NEG)
        return carry

    lax.fori_loop(0, N_HEADS, head, 0)


def _bias_table(rel_bias):
    bucket, valid = _bucket_table()
    shp = (2 * Q_BLOCK, Q_BLOCK)
    return pl.pallas_call(
        _bias_kernel,
        grid=(1,),
        in_specs=[pl.BlockSpec(memory_space=pltpu.SMEM),
                  pl.BlockSpec(shp, lambda v: (0, 0)),
                  pl.BlockSpec(shp, lambda v: (0, 0))],
        out_specs=pl.BlockSpec((2, N_HEADS) + shp, lambda v: (0, 0, 0, 0)),
        out_shape=jax.ShapeDtypeStruct((2, N_HEADS) + shp, F32),
        compiler_params=_cparams("arbitrary"),
        name="rel_bias_table",
    )(rel_bias, jnp.asarray(bucket), jnp.asarray(valid))


def _attn_prompt_kernel(sink_ref, q_ref, kp_ref, kc_ref, vp_ref, vc_ref, bias_ref, o_ref):
    scale = HEAD_DIM ** -0.5

    def scores(kv):
        ks = slice(kv * HEAD_DIM, (kv + 1) * HEAD_DIM)
        heads = range(kv * GROUP, (kv + 1) * GROUP)
        kband = jnp.concatenate([kp_ref[:, ks], kc_ref[:, ks]], axis=0)
        qg = jnp.concatenate(
            [q_ref[:, h * HEAD_DIM:(h + 1) * HEAD_DIM] for h in heads], axis=0) * scale
        st = lax.dot_general(kband, qg, (((1,), (1,)), ((), ())),
                             preferred_element_type=F32)
        return st + jnp.concatenate([bias_ref[0, h] for h in heads], axis=1)

    def finish(kv, st):
        ks = slice(kv * HEAD_DIM, (kv + 1) * HEAD_DIM)
        heads = range(kv * GROUP, (kv + 1) * GROUP)
        vband = jnp.concatenate([vp_ref[:, ks], vc_ref[:, ks]], axis=0)
        sk = jnp.concatenate([jnp.full((1, Q_BLOCK), sink_ref[h], F32) for h in heads], axis=1)
        m = jnp.maximum(jnp.max(st, axis=0, keepdims=True), sk)
        e = jnp.exp(st - m)
        denom = jnp.sum(e, axis=0, keepdims=True) + jnp.exp(sk - m)
        ot = lax.dot_general(vband, e.astype(BF16), (((0,), (0,)), ((), ())),
                             preferred_element_type=F32) * (1.0 / denom)
        o2 = jnp.concatenate(
            [ot[:, g * Q_BLOCK:(g + 1) * Q_BLOCK] for g in range(GROUP)], axis=0)
        gs = slice(kv * GROUP * HEAD_DIM, (kv + 1) * GROUP * HEAD_DIM)
        o_ref[:, gs] = o2.T.astype(o_ref.dtype)

    st = scores(0)
    for kv in range(N_KV_HEADS):
        st_next = scores(kv + 1) if kv + 1 < N_KV_HEADS else None
        finish(kv, st)
        st = st_next


def _attn_prompt(proj_p, bias_tab, sinks):
    nblk = SEQ // Q_BLOCK
    qcol = OFF_Q // ATTN_W
    kcol = OFF_K // KV_W
    vcol = OFF_V // KV_W

    def cur(b, n):
        return b * nblk + n

    def prev(b, n):
        return b * nblk + jnp.maximum(n - 1, 0)

    return pl.pallas_call(
        _attn_prompt_kernel,
        grid=(BATCH, nblk),
        in_specs=[pl.BlockSpec(memory_space=pltpu.SMEM),
                  pl.BlockSpec((Q_BLOCK, ATTN_W), lambda b, n: (cur(b, n), qcol)),
                  pl.BlockSpec((Q_BLOCK, KV_W), lambda b, n: (prev(b, n), kcol)),
                  pl.BlockSpec((Q_BLOCK, KV_W), lambda b, n: (cur(b, n), kcol)),
                  pl.BlockSpec((Q_BLOCK, KV_W), lambda b, n: (prev(b, n), vcol)),
                  pl.BlockSpec((Q_BLOCK, KV_W), lambda b, n: (cur(b, n), vcol)),
                  pl.BlockSpec((1, N_HEADS, 2 * Q_BLOCK, Q_BLOCK),
                               lambda b, n: (jnp.where(n == 0, 1, 0), 0, 0, 0))],
        out_specs=pl.BlockSpec((Q_BLOCK, ATTN_W), lambda b, n: (cur(b, n), 0)),
        out_shape=jax.ShapeDtypeStruct((M_PROMPT, ATTN_W), BF16),
        compiler_params=_cparams("parallel", "arbitrary"),
        name="attn_prompt",
    )(sinks, proj_p, proj_p, proj_p, proj_p, proj_p, bias_tab)


SAMPLES_PER_STEP = 8


def _attn_sample_kernel(q_ref, kn_ref, vn_ref, kb_ref, vb_ref, bias_ref, sink_ref,
                        o_ref, ko_ref, vo_ref):
    g = SAMPLES_PER_STEP
    scale = HEAD_DIM ** -0.5
    erow = lax.broadcasted_iota(jnp.int32, (HEAD_DIM, KV_W), 0)
    ecol = lax.broadcasted_iota(jnp.int32, (HEAD_DIM, KV_W), 1)
    expand = (ecol % HEAD_DIM == erow).astype(BF16)
    hrow = lax.broadcasted_iota(jnp.int32, (N_HEADS, KV_W), 0)
    hcol = lax.broadcasted_iota(jnp.int32, (N_HEADS, KV_W), 1)
    own = ((hrow // GROUP) == (hcol // HEAD_DIM))[None]
    bias_w = bias_ref[:, 0:WINDOW][None]
    bias_n = bias_ref[:, WINDOW:WINDOW + 1][None]
    sk = sink_ref[...][None]
    contract_last = (((1,), (1,)), ((), ()))

    q_all = q_ref[...].reshape(g * N_HEADS, HEAD_DIM) * scale
    qrow = jnp.where(own, _dot(q_all, expand).reshape(g, N_HEADS, KV_W), 0.0)
    qrow_b = qrow.astype(BF16)
    kn = kn_ref[...][:, None, :]
    vn = vn_ref[...][:, None, :]
    s_w = jnp.stack([lax.dot_general(qrow_b[b], kb_ref[b].astype(BF16), contract_last,
                                     preferred_element_type=F32) for b in range(g)])
    s_w = s_w + bias_w
    s_n = jnp.sum(qrow * kn, axis=-1, keepdims=True) + bias_n
    m = jnp.maximum(jnp.maximum(jnp.max(s_w, axis=-1, keepdims=True), s_n), sk)
    e_w = jnp.exp(s_w - m)
    e_n = jnp.exp(s_n - m)
    r = 1.0 / (jnp.sum(e_w, axis=-1, keepdims=True) + e_n + jnp.exp(sk - m))
    p_w = (e_w * r).astype(BF16)
    o_all = jnp.stack([_dot(p_w[b], vb_ref[b].astype(BF16)) for b in range(g)])
    o_own = jnp.where(own, o_all + (e_n * r) * vn, 0.0)
    o = o_own[:, :, 0:HEAD_DIM]
    for kv in range(1, N_KV_HEADS):
        o = o + o_own[:, :, kv * HEAD_DIM:(kv + 1) * HEAD_DIM]
    o_ref[...] = o.astype(o_ref.dtype)
    for b in range(g):
        ko_ref[b] = jnp.concatenate([kb_ref[b, 1:, :], kn_ref[b:b + 1, :]], axis=0)
        vo_ref[b] = jnp.concatenate([vb_ref[b, 1:, :], vn_ref[b:b + 1, :]], axis=0)


def _attn_sample(q3, k_new, v_new, k_buf, v_buf, bias_s, sinks):
    g = SAMPLES_PER_STEP
    nb = DEC_BATCH // g
    win = pl.BlockSpec((g, WINDOW, KV_W), lambda i: (i, 0, 0))
    row = pl.BlockSpec((g, KV_W), lambda i: (i, 0))
    return pl.pallas_call(
        _attn_sample_kernel,
        grid=(nb,),
        in_specs=[pl.BlockSpec((g, N_HEADS, HEAD_DIM), lambda i: (i, 0, 0)),
                  row, row, win, win,
                  pl.BlockSpec((N_HEADS, WINDOW + 1), lambda i: (0, 0)),
                  pl.BlockSpec((N_HEADS, 1), lambda i: (0, 0))],
        out_specs=[pl.BlockSpec((g, N_HEADS, HEAD_DIM), lambda i: (i, 0, 0)), win, win],
        out_shape=[jax.ShapeDtypeStruct((DEC_BATCH, N_HEADS, HEAD_DIM), BF16),
                   jax.ShapeDtypeStruct((DEC_BATCH, WINDOW, KV_W), F32),
                   jax.ShapeDtypeStruct((DEC_BATCH, WINDOW, KV_W), F32)],
        compiler_params=_cparams("parallel"),
        name="attn_sample",
    )(q3, k_new, v_new, k_buf, v_buf, bias_s, sinks.reshape(N_HEADS, 1))


def _merge_kernel(ap_ref, bp_ref, as_ref, bs_ref, waf_ref, wbf_ref,
                  gap_ref, gbp_ref, gas_ref, gbs_ref, op_ref, os_ref,
                  wa0_ref, wb0_ref, wa1_ref, wb1_ref):
    def compute(slot):
        wa_ref, wb_ref = slot

        def merged(a_ref, b_ref, ga_ref, gb_ref):
            return (_sigmoid(ga_ref[...].astype(F32)) * _dot(a_ref[...], wa_ref[...])
                    + _sigmoid(gb_ref[...].astype(F32)) * _dot(b_ref[...], wb_ref[...]))

        op_ref[...] = merged(ap_ref, bp_ref, gap_ref, gbp_ref).astype(op_ref.dtype)

        @pl.when(_is_last_row_tile())
        def _():
            os_ref[...] = merged(as_ref, bs_ref, gas_ref, gbs_ref).astype(os_ref.dtype)

    _ws_phases([waf_ref, wbf_ref], [wa0_ref, wb0_ref], [wa1_ref, wb1_ref], compute)


def _branch_merge(a_p, o_p, a_s, o_s, wa, wb, proj_p, proj_s, bm, bn):
    mp, k = a_p.shape
    ms = a_s.shape[0]
    n = wa.shape[1]
    grid = _ws_grid(n, bn, mp, bm)
    ga0 = OFF_GA // bn
    gb0 = OFF_GB // bn
    wchunk = pl.BlockSpec((k // grid[1], bn), _ws_wchunk(grid[0] - 1))
    wslot = pltpu.VMEM((k, bn), BF16)
    return pl.pallas_call(
        _merge_kernel,
        grid=grid,
        in_specs=[pl.BlockSpec((bm, k), _ws_rows),
                  pl.BlockSpec((bm, k), _ws_rows),
                  pl.BlockSpec((ms, k), lambda j, i: (0, 0)),
                  pl.BlockSpec((ms, k), lambda j, i: (0, 0)),
                  wchunk, wchunk,
                  pl.BlockSpec((bm, bn), _ws_tile(ga0)),
                  pl.BlockSpec((bm, bn), _ws_tile(gb0)),
                  pl.BlockSpec((ms, bn), _ws_stile(ga0)),
                  pl.BlockSpec((ms, bn), _ws_stile(gb0))],
        out_specs=[pl.BlockSpec((bm, bn), _ws_tile()),
                   pl.BlockSpec((ms, bn), _ws_stile())],
        out_shape=[jax.ShapeDtypeStruct((mp, n), BF16),
                   jax.ShapeDtypeStruct((ms, n), BF16)],
        scratch_shapes=[wslot, wslot, wslot, wslot],
        compiler_params=_cparams("arbitrary", "arbitrary"),
        name="branch_merge",
    )(a_p, o_p, a_s, o_s, wa, wb, proj_p, proj_p, proj_s, proj_s)


def _resid_kernel(xp_ref, xs_ref, wf_ref, rp_ref, rs_ref, op_ref, os_ref, w0_ref, w1_ref):
    def compute(slot):
        (w_ref,) = slot
        op_ref[...] = rp_ref[...] + _dot(xp_ref[...], w_ref[...])

        @pl.when(_is_last_row_tile())
        def _():
            os_ref[...] = rs_ref[...] + _dot(xs_ref[...], w_ref[...])

    _ws_phases([wf_ref], [w0_ref], [w1_ref], compute)


def _resid_matmul(xp, xs, w, rp, rs, bm, bn, name):
    mp, k = xp.shape
    ms = xs.shape[0]
    n = w.shape[1]
    grid = _ws_grid(n, bn, mp, bm)
    return pl.pallas_call(
        _resid_kernel,
        grid=grid,
        in_specs=[pl.BlockSpec((bm, k), _ws_rows),
                  pl.BlockSpec((ms, k), lambda j, i: (0, 0)),
                  pl.BlockSpec((k // grid[1], bn), _ws_wchunk(grid[0] - 1)),
                  pl.BlockSpec((bm, bn), _ws_tile()),
                  pl.BlockSpec((ms, bn), _ws_stile())],
        out_specs=[pl.BlockSpec((bm, bn), _ws_tile()),
                   pl.BlockSpec((ms, bn), _ws_stile())],
        out_shape=[jax.ShapeDtypeStruct((mp, n), F32),
                   jax.ShapeDtypeStruct((ms, n), F32)],
        scratch_shapes=[pltpu.VMEM((k, bn), BF16), pltpu.VMEM((k, bn), BF16)],
        compiler_params=_cparams("arbitrary", "arbitrary"),
        name=name,
    )(xp, xs, w, rp, rs)


LANES = 128


def _lane_partials(v):
    part = v[:, 0:LANES]
    for c in range(1, v.shape[1] // LANES):
        part = part + v[:, c * LANES:(c + 1) * LANES]
    return part


def _outproj_kernel(xp_ref, xs_ref, wf_ref, rp_ref, rs_ref, g_ref,
                    op_ref, os_ref, bp_ref, bs_ref, ssp_ref, sss_ref,
                    w0_ref, w1_ref, accp_ref, accs_ref):
    j = pl.program_id(0)
    i = pl.program_id(1)
    bm = xp_ref.shape[0]

    def compute(slot):
        (w_ref,) = slot

        def tile(x_ref, r_ref, o_ref, b_ref, acc_view, ss_ref):
            x1 = r_ref[...] + _dot(x_ref[...], w_ref[...])
            o_ref[...] = x1
            b_ref[...] = (x1 * g_ref[...]).astype(b_ref.dtype)
            total = jnp.where(j == 1, 0.0, acc_view[...]) + _lane_partials(x1 * x1)
            acc_view[...] = total
            ss_ref[...] = total

        rows = pl.ds(pl.multiple_of(i * bm, bm), bm)
        tile(xp_ref, rp_ref, op_ref, bp_ref, accp_ref.at[rows, :], ssp_ref)

        @pl.when(_is_last_row_tile())
        def _():
            tile(xs_ref, rs_ref, os_ref, bs_ref, accs_ref, sss_ref)

    _ws_phases([wf_ref], [w0_ref], [w1_ref], compute)


def _out_proj(xp, xs, w, rp, rs, gain, bm, bn):
    mp, k = xp.shape
    ms = xs.shape[0]
    n = w.shape[1]
    grid = _ws_grid(n, bn, mp, bm)
    last = grid[0] - 1
    ssp_map = lambda j, i: (jnp.where(j == last, i, 0), 0)
    return pl.pallas_call(
        _outproj_kernel,
        grid=grid,
        in_specs=[pl.BlockSpec((bm, k), _ws_rows),
                  pl.BlockSpec((ms, k), lambda j, i: (0, 0)),
                  pl.BlockSpec((k // grid[1], bn), _ws_wchunk(grid[0] - 1)),
                  pl.BlockSpec((bm, bn), _ws_tile()),
                  pl.BlockSpec((ms, bn), _ws_stile()),
                  pl.BlockSpec((1, bn), _ws_stile())],
        out_specs=[pl.BlockSpec((bm, bn), _ws_tile()),
                   pl.BlockSpec((ms, bn), _ws_stile()),
                   pl.BlockSpec((bm, bn), _ws_tile()),
                   pl.BlockSpec((ms, bn), _ws_stile()),
                   pl.BlockSpec((bm, LANES), ssp_map),
                   pl.BlockSpec((ms, LANES), lambda j, i: (0, 0))],
        out_shape=[jax.ShapeDtypeStruct((mp, n), F32),
                   jax.ShapeDtypeStruct((ms, n), F32),
                   jax.ShapeDtypeStruct((mp, n), BF16),
                   jax.ShapeDtypeStruct((ms, n), BF16),
                   jax.ShapeDtypeStruct((mp, LANES), F32),
                   jax.ShapeDtypeStruct((ms, LANES), F32)],
        scratch_shapes=[pltpu.VMEM((k, bn), BF16), pltpu.VMEM((k, bn), BF16),
                        pltpu.VMEM((mp, LANES), F32), pltpu.VMEM((ms, LANES), F32)],
        compiler_params=_cparams("arbitrary", "arbitrary"),
        name="out_proj",
    )(xp, xs, w, rp, rs, gain.reshape(1, n))


def _inv_rms(ss_ref):
    return lax.rsqrt(jnp.sum(ss_ref[...], axis=-1, keepdims=True) * (1.0 / D_MODEL) + EPS)


def _ffn_up_kernel(hp_ref, hs_ref, ssp_ref, sss_ref, wgf_ref, wuf_ref, cw_ref, cb_ref,
                   s0_ref, s1_ref, fp_ref, fs_ref, gs_ref, st_ref,
                   wg0_ref, wu0_ref, wg1_ref, wu1_ref, carry_ref, gbuf_ref, *, tiles_per_seq):
    i = pl.program_id(1)
    bm = hp_ref.shape[0]

    def compute(slot):
        wg_ref, wu_ref = slot
        w0 = cw_ref[0:1, :]
        w1 = cw_ref[1:2, :]
        w2 = cw_ref[2:3, :]
        bias = cb_ref[...]

        @pl.when(i % tiles_per_seq == 0)
        def _():
            carry_ref[...] = jnp.zeros(carry_ref.shape, F32)

        r = _inv_rms(ssp_ref)
        g = r * _dot(hp_ref[...], wg_ref[...])
        gbuf_ref[0:SUBLANES, :] = carry_ref[...]
        gbuf_ref[SUBLANES:SUBLANES + bm, :] = g
        g1 = gbuf_ref[SUBLANES - 1:SUBLANES - 1 + bm, :]
        g2 = gbuf_ref[SUBLANES - 2:SUBLANES - 2 + bm, :]
        gc = w0 * g2 + w1 * g1 + w2 * g + bias
        up = r * _dot(hp_ref[...], wu_ref[...])
        fp_ref[...] = (gc * _sigmoid(gc) * up).astype(fp_ref.dtype)
        tail = gbuf_ref[bm:SUBLANES + bm, :]
        carry_ref[...] = tail
        st_ref[0] = tail

        @pl.when(_is_last_row_tile())
        def _():
            rs = _inv_rms(sss_ref)
            gsm = rs * _dot(hs_ref[...], wg_ref[...])
            ups = rs * _dot(hs_ref[...], wu_ref[...])
            gcs = w0 * s0_ref[...] + w1 * s1_ref[...] + w2 * gsm + bias
            fs_ref[...] = (gcs * _sigmoid(gcs) * ups).astype(fs_ref.dtype)
            gs_ref[...] = gsm

    _ws_phases([wgf_ref, wuf_ref], [wg0_ref, wu0_ref], [wg1_ref, wu1_ref], compute)


def _ffn_up(hp, hs, ssp, sss, wg, wu, cw, cb, s0, s1, bm, bn):
    mp, k = hp.shape
    ms = hs.shape[0]
    n = wg.shape[1]
    grid = _ws_grid(n, bn, mp, bm)
    tiles_per_seq = SEQ // bm
    wchunk = pl.BlockSpec((k // grid[1], bn), _ws_wchunk(grid[0] - 1))
    wslot = pltpu.VMEM((k, bn), BF16)
    col = _ws_stile()

    def tail_map(j, i):
        return (jnp.where(j > 0, i, 0) // tiles_per_seq, 0, jnp.maximum(j - 1, 0))

    return pl.pallas_call(
        functools.partial(_ffn_up_kernel, tiles_per_seq=tiles_per_seq),
        grid=grid,
        in_specs=[pl.BlockSpec((bm, k), _ws_rows),
                  pl.BlockSpec((ms, k), lambda j, i: (0, 0)),
                  pl.BlockSpec((bm, LANES), _ws_rows),
                  pl.BlockSpec((ms, LANES), lambda j, i: (0, 0)),
                  wchunk, wchunk,
                  pl.BlockSpec((FFN_CONV_K, bn), col),
                  pl.BlockSpec((1, bn), col),
                  pl.BlockSpec((ms, bn), col),
                  pl.BlockSpec((ms, bn), col)],
        out_specs=[pl.BlockSpec((bm, bn), _ws_tile()),
                   pl.BlockSpec((ms, bn), col),
                   pl.BlockSpec((ms, bn), col),
                   pl.BlockSpec((1, SUBLANES, bn), tail_map)],
        out_shape=[jax.ShapeDtypeStruct((mp, n), BF16),
                   jax.ShapeDtypeStruct((ms, n), BF16),
                   jax.ShapeDtypeStruct((ms, n), F32),
                   jax.ShapeDtypeStruct((BATCH, SUBLANES, n), F32)],
        scratch_shapes=[wslot, wslot, wslot, wslot,
                        pltpu.VMEM((SUBLANES, bn), F32),
                        pltpu.VMEM((SUBLANES + bm, bn), F32)],
        compiler_params=_cparams("arbitrary", "arbitrary"),
        name="ffn_up",
    )(hp, hs, ssp, sss, wg, wu, cw, cb, s0, s1)


def kernel(x_prompt, x_sample, state_k_window, state_v_window, state_conv, state_ffn_conv,
           attn_norm_g, w_in, conv_w, w_branch_a, w_branch_b, sinks, w_out, ffn_norm_g,
           w_ffn_gate, w_ffn_up, ffn_conv_w, ffn_conv_b, w_ffn_down, rel_bias,
           final_norm_g):
    xp = x_prompt.reshape(M_PROMPT, D_MODEL)
    xs = x_sample.reshape(DEC_BATCH, D_MODEL)

    fs0 = state_ffn_conv[0][:, 0, :]
    fs1 = state_ffn_conv[0][:, 1, :]

    cs0 = state_conv[0][:, 0, :]
    cs1 = state_conv[0][:, 1, :]
    hs = _rmsnorm(xs, attn_norm_g[0], BF16, DEC_BATCH)
    a_p, a_s, u_s, u_tail, hp, hssp = _mixer_a(
        xp, hs, attn_norm_g[0], w_in[0], conv_w[0], cs0, cs1, 512, 512)
    proj_p, proj_s = _in_proj(hp, hs, hssp, w_in[0], M_PROMPT, W_REST, PROJ_W, 1024, 1024)

    bias_tab = _bias_table(rel_bias)
    o_p = _attn_prompt(proj_p, bias_tab, sinks[0])
    q3 = proj_s[:, OFF_Q:OFF_Q + ATTN_W].reshape(DEC_BATCH, N_HEADS, HEAD_DIM)
    k_new = proj_s[:, OFF_K:OFF_K + KV_W].astype(F32)
    v_new = proj_s[:, OFF_V:OFF_V + KV_W].astype(F32)
    bias_s = bias_tab[0, :, Q_BLOCK - 1:, Q_BLOCK - 1]
    o_s3, k_win_s, v_win_s = _attn_sample(
        q3, k_new, v_new,
        state_k_window[0].reshape(DEC_BATCH, WINDOW, KV_W),
        state_v_window[0].reshape(DEC_BATCH, WINDOW, KV_W), bias_s, sinks[0])
    o_s = o_s3.reshape(DEC_BATCH, ATTN_W)

    mg_p, mg_s = _branch_merge(a_p, o_p, a_s, o_s, w_branch_a[0], w_branch_b[0],
                               proj_p, proj_s, 1024, 1024)
    x1p, x1s, xbp, xbs, ssp, sss = _out_proj(mg_p, mg_s, w_out[0], xp, xs, ffn_norm_g[0],
                                             512, 1024)

    f_p, f_s, g_s, g_tail = _ffn_up(xbp, xbs, ssp, sss, w_ffn_gate[0], w_ffn_up[0],
                                    ffn_conv_w[0], ffn_conv_b, fs0, fs1, 1024, 512)
    x2p, x2s = _resid_matmul(f_p, f_s, w_ffn_down[0], x1p, x1s, 512, 512, "ffn_down")
    y_p = _rmsnorm(x2p, final_norm_g, F32, 256)
    y_s = _rmsnorm(x2s, final_norm_g, F32, DEC_BATCH)

    kv_p = proj_p.reshape(BATCH, SEQ, PROJ_W)[:, SEQ - WINDOW:, OFF_K:OFF_GA].astype(F32)
    k_win_p = kv_p[:, :, :KV_W].reshape(1, BATCH, WINDOW, N_KV_HEADS, HEAD_DIM)
    v_win_p = kv_p[:, :, KV_W:].reshape(1, BATCH, WINDOW, N_KV_HEADS, HEAD_DIM)
    conv_p = u_tail[:, SUBLANES - (CONV_K - 1):, :]
    ffn_p = g_tail[:, SUBLANES - (FFN_CONV_K - 1):, :]
    conv_s = jnp.stack([cs1, u_s], axis=1)
    ffn_s = jnp.stack([fs1, g_s], axis=1)

    return (y_p.reshape(BATCH, SEQ, D_MODEL),
            y_s.reshape(DEC_BATCH, 1, D_MODEL),
            k_win_p, v_win_p, conv_p[None], ffn_p[None],
            k_win_s.reshape(1, DEC_BATCH, WINDOW, N_KV_HEADS, HEAD_DIM),
            v_win_s.reshape(1, DEC_BATCH, WINDOW, N_KV_HEADS, HEAD_DIM),
            conv_s[None], ffn_s[None])
```

```python
import functools
import math

import numpy as np
import jax
import jax.numpy as jnp
from jax import lax
from jax.experimental import pallas as pl
from jax.experimental.pallas import tpu as pltpu

F32 = jnp.float32
BF16 = jnp.bfloat16

D_MODEL = 4096
BATCH = 4
SEQ = 2048
DEC_BATCH = 128
N_HEADS = 32
N_KV_HEADS = 8
HEAD_DIM = 64
GROUP = N_HEADS // N_KV_HEADS
ATTN_W = N_HEADS * HEAD_DIM
KV_W = N_KV_HEADS * HEAD_DIM
CONV_W = D_MODEL // 2
CONV_K = 3
WINDOW = 128
Q_BLOCK = 128
N_BUCKETS = 32
MAX_DISTANCE = 128
D_FF = 11008
FFN_CONV_K = 3
EPS = 1e-5
NEG = -1e30
IN_W = 3 * CONV_W + ATTN_W + 2 * KV_W + 2 * D_MODEL
M_PROMPT = BATCH * SEQ

W_CB = 0
W_CC = CONV_W
W_CH = 2 * CONV_W
W_REST = 3 * CONV_W
OFF_Q = 0
OFF_K = OFF_Q + ATTN_W
OFF_V = OFF_K + KV_W
OFF_GA = OFF_V + KV_W
OFF_GB = OFF_GA + D_MODEL
PROJ_W = OFF_GB + D_MODEL

V7X_VMEM_BYTES = 64 * 1024 * 1024
VMEM_LIMIT = V7X_VMEM_BYTES - 1024 * 1024
SUBLANES = 8


def _cparams(*sem):
    return pltpu.CompilerParams(dimension_semantics=sem, vmem_limit_bytes=VMEM_LIMIT)


def _dot(a, b):
    return jnp.dot(a, b, preferred_element_type=F32)


def _sigmoid(x):
    return 1.0 / (1.0 + jnp.exp(-x))


def _ws_grid(n, bn, mp, bm):
    return (pl.cdiv(n, bn) + 1, mp // bm)


def _ws_rows(j, i):
    return (jnp.where(j > 0, i, 0), 0)


def _ws_wchunk(nj, col0=0):
    return lambda j, i: (i, col0 + jnp.minimum(j, nj - 1))


def _ws_tile(col0=0):
    return lambda j, i: (jnp.where(j > 0, i, 0), col0 + jnp.maximum(j - 1, 0))


def _ws_stile(col0=0):
    return lambda j, i: (0, col0 + jnp.maximum(j - 1, 0))


def _ws_cast(wf_refs, dst_refs):
    i = pl.program_id(1)
    for wf_ref, dst_ref in zip(wf_refs, dst_refs):
        rows = wf_ref.shape[0]
        r0 = pl.multiple_of(i * rows, rows)
        dst_ref[pl.ds(r0, rows), :] = wf_ref[...].astype(BF16)


def _ws_phases(wf_refs, slot0, slot1, compute):
    j = pl.program_id(0)

    @pl.when(j == 0)
    def _():
        _ws_cast(wf_refs, slot0)

    @pl.when((j > 0) & (j % 2 == 1))
    def _():
        _ws_cast(wf_refs, slot1)
        compute(slot0)

    @pl.when((j > 0) & (j % 2 == 0))
    def _():
        _ws_cast(wf_refs, slot0)
        compute(slot1)


def _is_last_row_tile():
    return pl.program_id(1) == pl.num_programs(1) - 1


def _rms_kernel(x_ref, g_ref, o_ref):
    x = x_ref[...]
    r = lax.rsqrt(jnp.mean(x * x, axis=-1, keepdims=True) + EPS)
    o_ref[...] = ((x * r) * g_ref[...]).astype(o_ref.dtype)


def _rmsnorm(x, g, out_dtype, block_rows):
    rows, d = x.shape
    return pl.pallas_call(
        _rms_kernel,
        grid=(rows // block_rows,),
        in_specs=[pl.BlockSpec((block_rows, d), lambda i: (i, 0)),
                  pl.BlockSpec((1, d), lambda i: (0, 0))],
        out_specs=pl.BlockSpec((block_rows, d), lambda i: (i, 0)),
        out_shape=jax.ShapeDtypeStruct((rows, d), out_dtype),
        compiler_params=_cparams("parallel"),
        name="rmsnorm",
    )(x, g.reshape(1, d))


def _proj_kernel(xp_ref, xs_ref, ssp_ref, wf_ref, op_ref, os_ref, w0_ref, w1_ref):
    def compute(slot):
        (w_ref,) = slot
        op_ref[...] = (_inv_rms(ssp_ref) * _dot(xp_ref[...], w_ref[...])).astype(op_ref.dtype)

        @pl.when(_is_last_row_tile())
        def _():
            os_ref[...] = _dot(xs_ref[...], w_ref[...]).astype(os_ref.dtype)

    _ws_phases([wf_ref], [w0_ref], [w1_ref], compute)


def _in_proj(hp, hs, ssp, w, mp, col0, n, bm, bn):
    k = hp.shape[1]
    ms = hs.shape[0]
    grid = _ws_grid(n, bn, mp, bm)
    return pl.pallas_call(
        _proj_kernel,
        grid=grid,
        in_specs=[pl.BlockSpec((bm, k), _ws_rows),
                  pl.BlockSpec((ms, k), lambda j, i: (0, 0)),
                  pl.BlockSpec((bm, LANES), _ws_rows),
                  pl.BlockSpec((k // grid[1], bn), _ws_wchunk(grid[0] - 1, col0 // bn))],
        out_specs=[pl.BlockSpec((bm, bn), _ws_tile()),
                   pl.BlockSpec((ms, bn), _ws_stile())],
        out_shape=[jax.ShapeDtypeStruct((mp, n), BF16),
                   jax.ShapeDtypeStruct((ms, n), BF16)],
        scratch_shapes=[pltpu.VMEM((k, bn), BF16), pltpu.VMEM((k, bn), BF16)],
        compiler_params=_cparams("arbitrary", "arbitrary"),
        name="in_proj",
    )(hp, hs, ssp, w)


def _mixa_kernel(xp_ref, hs_ref, g_ref, wbf_ref, wcf_ref, whf_ref, cw_ref, s0_ref, s1_ref,
                 ap_ref, as_ref, us_ref, st_ref, hp_ref, ssp_ref,
                 wb0_ref, wc0_ref, wh0_ref, wb1_ref, wc1_ref, wh1_ref, carry_ref, ubuf_ref,
                 *, tiles_per_seq):
    i = pl.program_id(1)
    bm = xp_ref.shape[0]

    def compute(slot):
        wb_ref, wc_ref, wh_ref = slot
        w0 = cw_ref[0:1, :]
        w1 = cw_ref[1:2, :]
        w2 = cw_ref[2:3, :]

        @pl.when(i % tiles_per_seq == 0)
        def _():
            carry_ref[...] = jnp.zeros(carry_ref.shape, F32)

        hp_ref[...] = (xp_ref[...] * g_ref[...]).astype(hp_ref.dtype)
        ssp_ref[...] = _lane_partials(xp_ref[...] * xp_ref[...])
        r = _inv_rms(ssp_ref)
        u = (r * _dot(hp_ref[...], wc_ref[...])) * (r * _dot(hp_ref[...], wh_ref[...]))
        ubuf_ref[0:SUBLANES, :] = carry_ref[...]
        ubuf_ref[SUBLANES:SUBLANES + bm, :] = u
        u1 = ubuf_ref[SUBLANES - 1:SUBLANES - 1 + bm, :]
        u2 = ubuf_ref[SUBLANES - 2:SUBLANES - 2 + bm, :]
        z = w0 * u2 + w1 * u1 + w2 * u
        ap_ref[...] = ((r * _dot(hp_ref[...], wb_ref[...])) * z).astype(ap_ref.dtype)
        tail = ubuf_ref[bm:SUBLANES + bm, :]
        carry_ref[...] = tail
        st_ref[0] = tail

        @pl.when(_is_last_row_tile())
        def _():
            us = _dot(hs_ref[...], wc_ref[...]) * _dot(hs_ref[...], wh_ref[...])
            zs = w0 * s0_ref[...] + w1 * s1_ref[...] + w2 * us
            as_ref[...] = (_dot(hs_ref[...], wb_ref[...]) * zs).astype(as_ref.dtype)
            us_ref[...] = us

    _ws_phases([wbf_ref, wcf_ref, whf_ref], [wb0_ref, wc0_ref, wh0_ref],
               [wb1_ref, wc1_ref, wh1_ref], compute)


def _mixer_a(xp, hs, gain, w_in, conv_w, s0, s1, bm, bn):
    mp, k = xp.shape
    ms = hs.shape[0]
    n = CONV_W
    grid = _ws_grid(n, bn, mp, bm)
    nj = grid[0] - 1
    tiles_per_seq = SEQ // bm
    wslot = pltpu.VMEM((k, bn), BF16)
    col = _ws_stile()

    def wchunk(col0):
        return pl.BlockSpec((k // grid[1], bn), _ws_wchunk(nj, col0 // bn))

    def tail_map(j, i):
        return (jnp.where(j > 0, i, 0) // tiles_per_seq, 0, jnp.maximum(j - 1, 0))

    spare = grid[1]

    def operand_map(j, i):
        return (jnp.where(j == 1, i, jnp.where(j == 0, 0, spare)), 0)

    return pl.pallas_call(
        functools.partial(_mixa_kernel, tiles_per_seq=tiles_per_seq),
        grid=grid,
        in_specs=[pl.BlockSpec((bm, k), _ws_rows),
                  pl.BlockSpec((ms, k), lambda j, i: (0, 0)),
                  pl.BlockSpec((1, k), lambda j, i: (0, 0)),
                  wchunk(W_CB), wchunk(W_CC), wchunk(W_CH),
                  pl.BlockSpec((CONV_K, bn), col),
                  pl.BlockSpec((ms, bn), col),
                  pl.BlockSpec((ms, bn), col)],
        out_specs=[pl.BlockSpec((bm, bn), _ws_tile()),
                   pl.BlockSpec((ms, bn), col),
                   pl.BlockSpec((ms, bn), col),
                   pl.BlockSpec((1, SUBLANES, bn), tail_map),
                   pl.BlockSpec((bm, k), operand_map),
                   pl.BlockSpec((bm, LANES), operand_map)],
        out_shape=[jax.ShapeDtypeStruct((mp, n), BF16),
                   jax.ShapeDtypeStruct((ms, n), BF16),
                   jax.ShapeDtypeStruct((ms, n), F32),
                   jax.ShapeDtypeStruct((BATCH, SUBLANES, n), F32),
                   jax.ShapeDtypeStruct((mp + bm, k), BF16),
                   jax.ShapeDtypeStruct((mp + bm, LANES), F32)],
        scratch_shapes=[wslot] * 6 + [pltpu.VMEM((SUBLANES, bn), F32),
                                      pltpu.VMEM((SUBLANES + bm, bn), F32)],
        compiler_params=_cparams("arbitrary", "arbitrary"),
        name="mixer_a",
    )(xp, hs, gain.reshape(1, k), w_in, w_in, w_in, conv_w, s0, s1)


def _bucket_table():
    qi = np.arange(Q_BLOCK)[None, :]
    kj = np.arange(2 * Q_BLOCK)[:, None]
    dist = qi + Q_BLOCK - kj
    max_exact = N_BUCKETS // 2
    d = np.maximum(dist, 0)
    df = np.maximum(d, 1).astype(np.float32)
    large = max_exact + (np.log(df / np.float32(max_exact))
                         / np.float32(math.log(MAX_DISTANCE / max_exact))
                         * np.float32(N_BUCKETS - max_exact)).astype(np.int32)
    large = np.minimum(large, N_BUCKETS - 1)
    bucket = np.where(d < max_exact, d, large).astype(np.int32)
    valid = ((dist >= 0) & (dist <= WINDOW)).astype(np.int32)
    return bucket, valid


def _bias_kernel(rb_ref, bucket_ref, valid_ref, o_ref):
    bucket = bucket_ref[...]
    row = lax.broadcasted_iota(jnp.int32, bucket.shape, 0)
    keep = valid_ref[...] != 0
    keep_first = keep & (row >= Q_BLOCK)

    def head(h, carry):
        acc = jnp.zeros(bucket.shape, F32)
        for b in range(N_BUCKETS):
            acc = jnp.where(bucket == b, rb_ref[b, h], acc)
        o_ref[0, h] = jnp.where(keep, acc, NEG)
        o_ref[1, h] = jnp.where(keep_first, acc, NEG)
        return carry

    lax.fori_loop(0, N_HEADS, head, 0)


def _bias_table(rel_bias):
    bucket, valid = _bucket_table()
    shp = (2 * Q_BLOCK, Q_BLOCK)
    return pl.pallas_call(
        _bias_kernel,
        grid=(1,),
        in_specs=[pl.BlockSpec(memory_space=pltpu.SMEM),
                  pl.BlockSpec(shp, lambda v: (0, 0)),
                  pl.BlockSpec(shp, lambda v: (0, 0))],
        out_specs=pl.BlockSpec((2, N_HEADS) + shp, lambda v: (0, 0, 0, 0)),
        out_shape=jax.ShapeDtypeStruct((2, N_HEADS) + shp, F32),
        compiler_params=_cparams("arbitrary"),
        name="rel_bias_table",
    )(rel_bias, jnp.asarray(bucket), jnp.asarray(valid))


def _attn_prompt_kernel(sink_ref, q_ref, kp_ref, kc_ref, vp_ref, vc_ref, bias_ref, o_ref):
    scale = HEAD_DIM ** -0.5

    def scores(kv):
        ks = slice(kv * HEAD_DIM, (kv + 1) * HEAD_DIM)
        heads = range(kv * GROUP, (kv + 1) * GROUP)
        kband = jnp.concatenate([kp_ref[:, ks], kc_ref[:, ks]], axis=0)
        qg = jnp.concatenate(
            [q_ref[:, h * HEAD_DIM:(h + 1) * HEAD_DIM] for h in heads], axis=0) * scale
        st = lax.dot_general(kband, qg, (((1,), (1,)), ((), ())),
                             preferred_element_type=F32)
        return st + jnp.concatenate([bias_ref[0, h] for h in heads], axis=1)

    def finish(kv, st):
        ks = slice(kv * HEAD_DIM, (kv + 1) * HEAD_DIM)
        heads = range(kv * GROUP, (kv + 1) * GROUP)
        vband = jnp.concatenate([vp_ref[:, ks], vc_ref[:, ks]], axis=0)
        sk = jnp.concatenate([jnp.full((1, Q_BLOCK), sink_ref[h], F32) for h in heads], axis=1)
        m = jnp.maximum(jnp.max(st, axis=0, keepdims=True), sk)
        e = jnp.exp(st - m)
        denom = jnp.sum(e, axis=0, keepdims=True) + jnp.exp(sk - m)
        ot = lax.dot_general(vband, e.astype(BF16), (((0,), (0,)), ((), ())),
                             preferred_element_type=F32) * (1.0 / denom)
        o2 = jnp.concatenate(
            [ot[:, g * Q_BLOCK:(g + 1) * Q_BLOCK] for g in range(GROUP)], axis=0)
        gs = slice(kv * GROUP * HEAD_DIM, (kv + 1) * GROUP * HEAD_DIM)
        o_ref[:, gs] = o2.T.astype(o_ref.dtype)

    st = scores(0)
    for kv in range(N_KV_HEADS):
        st_next = scores(kv + 1) if kv + 1 < N_KV_HEADS else None
        finish(kv, st)
        st = st_next


def _attn_prompt(proj_p, bias_tab, sinks):
    nblk = SEQ // Q_BLOCK
    qcol = OFF_Q // ATTN_W
    kcol = OFF_K // KV_W
    vcol = OFF_V // KV_W

    def cur(b, n):
        return b * nblk + n

    def prev(b, n):
        return b * nblk + jnp.maximum(n - 1, 0)

    return pl.pallas_call(
        _attn_prompt_kernel,
        grid=(BATCH, nblk),
        in_specs=[pl.BlockSpec(memory_space=pltpu.SMEM),
                  pl.BlockSpec((Q_BLOCK, ATTN_W), lambda b, n: (cur(b, n), qcol)),
                  pl.BlockSpec((Q_BLOCK, KV_W), lambda b, n: (prev(b, n), kcol)),
                  pl.BlockSpec((Q_BLOCK, KV_W), lambda b, n: (cur(b, n), kcol)),
                  pl.BlockSpec((Q_BLOCK, KV_W), lambda b, n: (prev(b, n), vcol)),
                  pl.BlockSpec((Q_BLOCK, KV_W), lambda b, n: (cur(b, n), vcol)),
                  pl.BlockSpec((1, N_HEADS, 2 * Q_BLOCK, Q_BLOCK),
                               lambda b, n: (jnp.where(n == 0, 1, 0), 0, 0, 0))],
        out_specs=pl.BlockSpec((Q_BLOCK, ATTN_W), lambda b, n: (cur(b, n), 0)),
        out_shape=jax.ShapeDtypeStruct((M_PROMPT, ATTN_W), BF16),
        compiler_params=_cparams("parallel", "arbitrary"),
        name="attn_prompt",
    )(sinks, proj_p, proj_p, proj_p, proj_p, proj_p, bias_tab)


SAMPLES_PER_STEP = 8


def _attn_sample_kernel(q_ref, kn_ref, vn_ref, kt_ref, vt_ref, bias_ref, sink_ref,
                        o_ref, kto_ref, vto_ref):
    g = SAMPLES_PER_STEP
    scale = HEAD_DIM ** -0.5
    erow = lax.broadcasted_iota(jnp.int32, (HEAD_DIM, KV_W), 0)
    ecol = lax.broadcasted_iota(jnp.int32, (HEAD_DIM, KV_W), 1)
    expand = (ecol % HEAD_DIM == erow).astype(BF16)
    hrow = lax.broadcasted_iota(jnp.int32, (N_HEADS, KV_W), 0)
    hcol = lax.broadcasted_iota(jnp.int32, (N_HEADS, KV_W), 1)
    own = ((hrow // GROUP) == (hcol // HEAD_DIM))[None]
    bias_w = bias_ref[:, 0:WINDOW][None]
    bias_n = bias_ref[:, WINDOW:WINDOW + 1][None]
    sk = sink_ref[...][None]
    contract_last = (((1,), (1,)), ((), ()))

    q_all = q_ref[...].reshape(g * N_HEADS, HEAD_DIM) * scale
    qrow = jnp.where(own, _dot(q_all, expand).reshape(g, N_HEADS, KV_W), 0.0)
    qrow_b = qrow.astype(BF16)
    kn = kn_ref[...][:, None, :]
    vn = vn_ref[...][:, None, :]
    s_w = jnp.stack([_dot(qrow_b[b], kt_ref[b].astype(BF16)) for b in range(g)])
    s_w = s_w + bias_w
    s_n = jnp.sum(qrow * kn, axis=-1, keepdims=True) + bias_n
    m = jnp.maximum(jnp.maximum(jnp.max(s_w, axis=-1, keepdims=True), s_n), sk)
    e_w = jnp.exp(s_w - m)
    e_n = jnp.exp(s_n - m)
    r = 1.0 / (jnp.sum(e_w, axis=-1, keepdims=True) + e_n + jnp.exp(sk - m))
    p_w = (e_w * r).astype(BF16)
    o_all = jnp.stack([lax.dot_general(p_w[b], vt_ref[b].astype(BF16), contract_last,
                                       preferred_element_type=F32) for b in range(g)])
    o_own = jnp.where(own, o_all + (e_n * r) * vn, 0.0)
    o = o_own[:, :, 0:HEAD_DIM]
    for kv in range(1, N_KV_HEADS):
        o = o + o_own[:, :, kv * HEAD_DIM:(kv + 1) * HEAD_DIM]
    o_ref[...] = o.astype(o_ref.dtype)

    rows = jnp.concatenate([kn_ref[...], vn_ref[...]], axis=0).astype(BF16)
    place = (lax.broadcasted_iota(jnp.int32, (2 * g, 2 * g * WINDOW), 1)
             == WINDOW * lax.broadcasted_iota(jnp.int32, (2 * g, 2 * g * WINDOW), 0)
             ).astype(BF16)
    newcols = lax.dot_general(rows, place, (((0,), (0,)), ((), ())),
                              preferred_element_type=F32)
    first = lax.broadcasted_iota(jnp.int32, (KV_W, WINDOW), 1) == 0
    for b in range(g):
        kcol = newcols[:, b * WINDOW:(b + 1) * WINDOW]
        vcol = newcols[:, (g + b) * WINDOW:(g + b + 1) * WINDOW]
        kto_ref[b] = pltpu.roll(jnp.where(first, kcol, kt_ref[b]), WINDOW - 1, 1)
        vto_ref[b] = pltpu.roll(jnp.where(first, vcol, vt_ref[b]), WINDOW - 1, 1)


def _attn_sample(q3, k_new, v_new, k_buf, v_buf, bias_s, sinks):
    g = SAMPLES_PER_STEP
    nb = DEC_BATCH // g
    win = pl.BlockSpec((g, KV_W, WINDOW), lambda i: (i, 0, 0))
    row = pl.BlockSpec((g, KV_W), lambda i: (i, 0))
    return pl.pallas_call(
        _attn_sample_kernel,
        grid=(nb,),
        in_specs=[pl.BlockSpec((g, N_HEADS, HEAD_DIM), lambda i: (i, 0, 0)),
                  row, row, win, win,
                  pl.BlockSpec((N_HEADS, WINDOW + 1), lambda i: (0, 0)),
                  pl.BlockSpec((N_HEADS, 1), lambda i: (0, 0))],
        out_specs=[pl.BlockSpec((g, N_HEADS, HEAD_DIM), lambda i: (i, 0, 0)), win, win],
        out_shape=[jax.ShapeDtypeStruct((DEC_BATCH, N_HEADS, HEAD_DIM), BF16),
                   jax.ShapeDtypeStruct((DEC_BATCH, KV_W, WINDOW), F32),
                   jax.ShapeDtypeStruct((DEC_BATCH, KV_W, WINDOW), F32)],
        compiler_params=_cparams("parallel"),
        name="attn_sample",
    )(q3, k_new, v_new, k_buf, v_buf, bias_s, sinks.reshape(N_HEADS, 1))


def _merge_kernel(ap_ref, bp_ref, as_ref, bs_ref, waf_ref, wbf_ref,
                  gap_ref, gbp_ref, gas_ref, gbs_ref, op_ref, os_ref,
                  wa0_ref, wb0_ref, wa1_ref, wb1_ref):
    def compute(slot):
        wa_ref, wb_ref = slot

        def merged(a_ref, b_ref, ga_ref, gb_ref):
            return (_sigmoid(ga_ref[...].astype(F32)) * _dot(a_ref[...], wa_ref[...])
                    + _sigmoid(gb_ref[...].astype(F32)) * _dot(b_ref[...], wb_ref[...]))

        op_ref[...] = merged(ap_ref, bp_ref, gap_ref, gbp_ref).astype(op_ref.dtype)

        @pl.when(_is_last_row_tile())
        def _():
            os_ref[...] = merged(as_ref, bs_ref, gas_ref, gbs_ref).astype(os_ref.dtype)

    _ws_phases([waf_ref, wbf_ref], [wa0_ref, wb0_ref], [wa1_ref, wb1_ref], compute)


def _branch_merge(a_p, o_p, a_s, o_s, wa, wb, proj_p, proj_s, bm, bn):
    mp, k = a_p.shape
    ms = a_s.shape[0]
    n = wa.shape[1]
    grid = _ws_grid(n, bn, mp, bm)
    ga0 = OFF_GA // bn
    gb0 = OFF_GB // bn
    wchunk = pl.BlockSpec((k // grid[1], bn), _ws_wchunk(grid[0] - 1))
    wslot = pltpu.VMEM((k, bn), BF16)
    return pl.pallas_call(
        _merge_kernel,
        grid=grid,
        in_specs=[pl.BlockSpec((bm, k), _ws_rows),
                  pl.BlockSpec((bm, k), _ws_rows),
                  pl.BlockSpec((ms, k), lambda j, i: (0, 0)),
                  pl.BlockSpec((ms, k), lambda j, i: (0, 0)),
                  wchunk, wchunk,
                  pl.BlockSpec((bm, bn), _ws_tile(ga0)),
                  pl.BlockSpec((bm, bn), _ws_tile(gb0)),
                  pl.BlockSpec((ms, bn), _ws_stile(ga0)),
                  pl.BlockSpec((ms, bn), _ws_stile(gb0))],
        out_specs=[pl.BlockSpec((bm, bn), _ws_tile()),
                   pl.BlockSpec((ms, bn), _ws_stile())],
        out_shape=[jax.ShapeDtypeStruct((mp, n), BF16),
                   jax.ShapeDtypeStruct((ms, n), BF16)],
        scratch_shapes=[wslot, wslot, wslot, wslot],
        compiler_params=_cparams("arbitrary", "arbitrary"),
        name="branch_merge",
    )(a_p, o_p, a_s, o_s, wa, wb, proj_p, proj_p, proj_s, proj_s)


def _resid_kernel(xp_ref, xs_ref, wf_ref, rp_ref, rs_ref, op_ref, os_ref, w0_ref, w1_ref):
    def compute(slot):
        (w_ref,) = slot
        op_ref[...] = rp_ref[...] + _dot(xp_ref[...], w_ref[...])

        @pl.when(_is_last_row_tile())
        def _():
            os_ref[...] = rs_ref[...] + _dot(xs_ref[...], w_ref[...])

    _ws_phases([wf_ref], [w0_ref], [w1_ref], compute)


def _resid_matmul(xp, xs, w, rp, rs, bm, bn, name):
    mp, k = xp.shape
    ms = xs.shape[0]
    n = w.shape[1]
    grid = _ws_grid(n, bn, mp, bm)
    return pl.pallas_call(
        _resid_kernel,
        grid=grid,
        in_specs=[pl.BlockSpec((bm, k), _ws_rows),
                  pl.BlockSpec((ms, k), lambda j, i: (0, 0)),
                  pl.BlockSpec((k // grid[1], bn), _ws_wchunk(grid[0] - 1)),
                  pl.BlockSpec((bm, bn), _ws_tile()),
                  pl.BlockSpec((ms, bn), _ws_stile())],
        out_specs=[pl.BlockSpec((bm, bn), _ws_tile()),
                   pl.BlockSpec((ms, bn), _ws_stile())],
        out_shape=[jax.ShapeDtypeStruct((mp, n), F32),
                   jax.ShapeDtypeStruct((ms, n), F32)],
        scratch_shapes=[pltpu.VMEM((k, bn), BF16), pltpu.VMEM((k, bn), BF16)],
        compiler_params=_cparams("arbitrary", "arbitrary"),
        name=name,
    )(xp, xs, w, rp, rs)


LANES = 128


def _lane_partials(v):
    part = v[:, 0:LANES]
    for c in range(1, v.shape[1] // LANES):
        part = part + v[:, c * LANES:(c + 1) * LANES]
    return part


def _outproj_kernel(xp_ref, xs_ref, wf_ref, rp_ref, rs_ref, g_ref,
                    op_ref, os_ref, bp_ref, bs_ref, ssp_ref, sss_ref,
                    w0_ref, w1_ref, accp_ref, accs_ref):
    j = pl.program_id(0)
    i = pl.program_id(1)
    bm = xp_ref.shape[0]

    def compute(slot):
        (w_ref,) = slot

        def tile(x_ref, r_ref, o_ref, b_ref, acc_view, ss_ref):
            x1 = r_ref[...] + _dot(x_ref[...], w_ref[...])
            o_ref[...] = x1
            b_ref[...] = (x1 * g_ref[...]).astype(b_ref.dtype)
            total = jnp.where(j == 1, 0.0, acc_view[...]) + _lane_partials(x1 * x1)
            acc_view[...] = total
            ss_ref[...] = total

        rows = pl.ds(pl.multiple_of(i * bm, bm), bm)
        tile(xp_ref, rp_ref, op_ref, bp_ref, accp_ref.at[rows, :], ssp_ref)

        @pl.when(_is_last_row_tile())
        def _():
            tile(xs_ref, rs_ref, os_ref, bs_ref, accs_ref, sss_ref)

    _ws_phases([wf_ref], [w0_ref], [w1_ref], compute)


def _out_proj(xp, xs, w, rp, rs, gain, bm, bn):
    mp, k = xp.shape
    ms = xs.shape[0]
    n = w.shape[1]
    grid = _ws_grid(n, bn, mp, bm)
    last = grid[0] - 1
    ssp_map = lambda j, i: (jnp.where(j == last, i, 0), 0)
    return pl.pallas_call(
        _outproj_kernel,
        grid=grid,
        in_specs=[pl.BlockSpec((bm, k), _ws_rows),
                  pl.BlockSpec((ms, k), lambda j, i: (0, 0)),
                  pl.BlockSpec((k // grid[1], bn), _ws_wchunk(grid[0] - 1)),
                  pl.BlockSpec((bm, bn), _ws_tile()),
                  pl.BlockSpec((ms, bn), _ws_stile()),
                  pl.BlockSpec((1, bn), _ws_stile())],
        out_specs=[pl.BlockSpec((bm, bn), _ws_tile()),
                   pl.BlockSpec((ms, bn), _ws_stile()),
                   pl.BlockSpec((bm, bn), _ws_tile()),
                   pl.BlockSpec((ms, bn), _ws_stile()),
                   pl.BlockSpec((bm, LANES), ssp_map),
                   pl.BlockSpec((ms, LANES), lambda j, i: (0, 0))],
        out_shape=[jax.ShapeDtypeStruct((mp, n), F32),
                   jax.ShapeDtypeStruct((ms, n), F32),
                   jax.ShapeDtypeStruct((mp, n), BF16),
                   jax.ShapeDtypeStruct((ms, n), BF16),
                   jax.ShapeDtypeStruct((mp, LANES), F32),
                   jax.ShapeDtypeStruct((ms, LANES), F32)],
        scratch_shapes=[pltpu.VMEM((k, bn), BF16), pltpu.VMEM((k, bn), BF16),
                        pltpu.VMEM((mp, LANES), F32), pltpu.VMEM((ms, LANES), F32)],
        compiler_params=_cparams("arbitrary", "arbitrary"),
        name="out_proj",
    )(xp, xs, w, rp, rs, gain.reshape(1, n))


def _inv_rms(ss_ref):
    return lax.rsqrt(jnp.sum(ss_ref[...], axis=-1, keepdims=True) * (1.0 / D_MODEL) + EPS)


def _ffn_up_kernel(hp_ref, hs_ref, ssp_ref, sss_ref, wgf_ref, wuf_ref, cw_ref, cb_ref,
                   s0_ref, s1_ref, fp_ref, fs_ref, gs_ref, st_ref,
                   wg0_ref, wu0_ref, wg1_ref, wu1_ref, carry_ref, gbuf_ref, *, tiles_per_seq):
    i = pl.program_id(1)
    bm = hp_ref.shape[0]

    def compute(slot):
        wg_ref, wu_ref = slot
        w0 = cw_ref[0:1, :]
        w1 = cw_ref[1:2, :]
        w2 = cw_ref[2:3, :]
        bias = cb_ref[...]

        @pl.when(i % tiles_per_seq == 0)
        def _():
            carry_ref[...] = jnp.zeros(carry_ref.shape, F32)

        r = _inv_rms(ssp_ref)
        g = r * _dot(hp_ref[...], wg_ref[...])
        gbuf_ref[0:SUBLANES, :] = carry_ref[...]
        gbuf_ref[SUBLANES:SUBLANES + bm, :] = g
        g1 = gbuf_ref[SUBLANES - 1:SUBLANES - 1 + bm, :]
        g2 = gbuf_ref[SUBLANES - 2:SUBLANES - 2 + bm, :]
        gc = w0 * g2 + w1 * g1 + w2 * g + bias
        up = r * _dot(hp_ref[...], wu_ref[...])
        fp_ref[...] = (gc * _sigmoid(gc) * up).astype(fp_ref.dtype)
        tail = gbuf_ref[bm:SUBLANES + bm, :]
        carry_ref[...] = tail
        st_ref[0] = tail

        @pl.when(_is_last_row_tile())
        def _():
            rs = _inv_rms(sss_ref)
            gsm = rs * _dot(hs_ref[...], wg_ref[...])
            ups = rs * _dot(hs_ref[...], wu_ref[...])
            gcs = w0 * s0_ref[...] + w1 * s1_ref[...] + w2 * gsm + bias
            fs_ref[...] = (gcs * _sigmoid(gcs) * ups).astype(fs_ref.dtype)
            gs_ref[...] = gsm

    _ws_phases([wgf_ref, wuf_ref], [wg0_ref, wu0_ref], [wg1_ref, wu1_ref], compute)


def _ffn_up(hp, hs, ssp, sss, wg, wu, cw, cb, s0, s1, bm, bn):
    mp, k = hp.shape
    ms = hs.shape[0]
    n = wg.shape[1]
    grid = _ws_grid(n, bn, mp, bm)
    tiles_per_seq = SEQ // bm
    wchunk = pl.BlockSpec((k // grid[1], bn), _ws_wchunk(grid[0] - 1))
    wslot = pltpu.VMEM((k, bn), BF16)
    col = _ws_stile()

    def tail_map(j, i):
        return (jnp.where(j > 0, i, 0) // tiles_per_seq, 0, jnp.maximum(j - 1, 0))

    return pl.pallas_call(
        functools.partial(_ffn_up_kernel, tiles_per_seq=tiles_per_seq),
        grid=grid,
        in_specs=[pl.BlockSpec((bm, k), _ws_rows),
                  pl.BlockSpec((ms, k), lambda j, i: (0, 0)),
                  pl.BlockSpec((bm, LANES), _ws_rows),
                  pl.BlockSpec((ms, LANES), lambda j, i: (0, 0)),
                  wchunk, wchunk,
                  pl.BlockSpec((FFN_CONV_K, bn), col),
                  pl.BlockSpec((1, bn), col),
                  pl.BlockSpec((ms, bn), col),
                  pl.BlockSpec((ms, bn), col)],
        out_specs=[pl.BlockSpec((bm, bn), _ws_tile()),
                   pl.BlockSpec((ms, bn), col),
                   pl.BlockSpec((ms, bn), col),
                   pl.BlockSpec((1, SUBLANES, bn), tail_map)],
        out_shape=[jax.ShapeDtypeStruct((mp, n), BF16),
                   jax.ShapeDtypeStruct((ms, n), BF16),
                   jax.ShapeDtypeStruct((ms, n), F32),
                   jax.ShapeDtypeStruct((BATCH, SUBLANES, n), F32)],
        scratch_shapes=[wslot, wslot, wslot, wslot,
                        pltpu.VMEM((SUBLANES, bn), F32),
                        pltpu.VMEM((SUBLANES + bm, bn), F32)],
        compiler_params=_cparams("arbitrary", "arbitrary"),
        name="ffn_up",
    )(hp, hs, ssp, sss, wg, wu, cw, cb, s0, s1)


def _window_to_kernel(state):
    return jnp.transpose(state[0], (0, 2, 3, 1)).reshape(DEC_BATCH, KV_W, WINDOW)


def _window_from_kernel(win):
    win = win.reshape(DEC_BATCH, N_KV_HEADS, HEAD_DIM, WINDOW)
    return jnp.transpose(win, (0, 3, 1, 2))[None]


def kernel(x_prompt, x_sample, state_k_window, state_v_window, state_conv, state_ffn_conv,
           attn_norm_g, w_in, conv_w, w_branch_a, w_branch_b, sinks, w_out, ffn_norm_g,
           w_ffn_gate, w_ffn_up, ffn_conv_w, ffn_conv_b, w_ffn_down, rel_bias,
           final_norm_g):
    xp = x_prompt.reshape(M_PROMPT, D_MODEL)
    xs = x_sample.reshape(DEC_BATCH, D_MODEL)

    fs0 = state_ffn_conv[0][:, 0, :]
    fs1 = state_ffn_conv[0][:, 1, :]

    cs0 = state_conv[0][:, 0, :]
    cs1 = state_conv[0][:, 1, :]
    hs = _rmsnorm(xs, attn_norm_g[0], BF16, DEC_BATCH)
    a_p, a_s, u_s, u_tail, hp, hssp = _mixer_a(
        xp, hs, attn_norm_g[0], w_in[0], conv_w[0], cs0, cs1, 512, 512)
    proj_p, proj_s = _in_proj(hp, hs, hssp, w_in[0], M_PROMPT, W_REST, PROJ_W, 1024, 1024)

    bias_tab = _bias_table(rel_bias)
    o_p = _attn_prompt(proj_p, bias_tab, sinks[0])
    q3 = proj_s[:, OFF_Q:OFF_Q + ATTN_W].reshape(DEC_BATCH, N_HEADS, HEAD_DIM)
    k_new = proj_s[:, OFF_K:OFF_K + KV_W].astype(F32)
    v_new = proj_s[:, OFF_V:OFF_V + KV_W].astype(F32)
    bias_s = bias_tab[0, :, Q_BLOCK - 1:, Q_BLOCK - 1]
    o_s3, k_win_s, v_win_s = _attn_sample(
        q3, k_new, v_new,
        _window_to_kernel(state_k_window), _window_to_kernel(state_v_window),
        bias_s, sinks[0])
    o_s = o_s3.reshape(DEC_BATCH, ATTN_W)

    mg_p, mg_s = _branch_merge(a_p, o_p, a_s, o_s, w_branch_a[0], w_branch_b[0],
                               proj_p, proj_s, 1024, 1024)
    x1p, x1s, xbp, xbs, ssp, sss = _out_proj(mg_p, mg_s, w_out[0], xp, xs, ffn_norm_g[0],
                                             512, 1024)

    f_p, f_s, g_s, g_tail = _ffn_up(xbp, xbs, ssp, sss, w_ffn_gate[0], w_ffn_up[0],
                                    ffn_conv_w[0], ffn_conv_b, fs0, fs1, 1024, 512)
    x2p, x2s = _resid_matmul(f_p, f_s, w_ffn_down[0], x1p, x1s, 512, 512, "ffn_down")
    y_p = _rmsnorm(x2p, final_norm_g, F32, 256)
    y_s = _rmsnorm(x2s, final_norm_g, F32, DEC_BATCH)

    kv_p = proj_p.reshape(BATCH, SEQ, PROJ_W)[:, SEQ - WINDOW:, OFF_K:OFF_GA].astype(F32)
    k_win_p = kv_p[:, :, :KV_W].reshape(1, BATCH, WINDOW, N_KV_HEADS, HEAD_DIM)
    v_win_p = kv_p[:, :, KV_W:].reshape(1, BATCH, WINDOW, N_KV_HEADS, HEAD_DIM)
    conv_p = u_tail[:, SUBLANES - (CONV_K - 1):, :]
    ffn_p = g_tail[:, SUBLANES - (FFN_CONV_K - 1):, :]
    conv_s = jnp.stack([cs1, u_s], axis=1)
    ffn_s = jnp.stack([fs1, g_s], axis=1)

    return (y_p.reshape(BATCH, SEQ, D_MODEL),
            y_s.reshape(DEC_BATCH, 1, D_MODEL),
            k_win_p, v_win_p, conv_p[None], ffn_p[None],
            _window_from_kernel(k_win_s), _window_from_kernel(v_win_s),
            conv_s[None], ffn_s[None])
```

```python
import functools
import math

import numpy as np
import jax
import jax.numpy as jnp
from jax import lax
from jax.experimental import pallas as pl
from jax.experimental.pallas import tpu as pltpu

F32 = jnp.float32
BF16 = jnp.bfloat16

D_MODEL = 4096
BATCH = 4
SEQ = 2048
DEC_BATCH = 128
N_HEADS = 32
N_KV_HEADS = 8
HEAD_DIM = 64
GROUP = N_HEADS // N_KV_HEADS
ATTN_W = N_HEADS * HEAD_DIM
KV_W = N_KV_HEADS * HEAD_DIM
CONV_W = D_MODEL // 2
CONV_K = 3
WINDOW = 128
Q_BLOCK = 128
N_BUCKETS = 32
MAX_DISTANCE = 128
D_FF = 11008
FFN_CONV_K = 3
EPS = 1e-5
NEG = -1e30
IN_W = 3 * CONV_W + ATTN_W + 2 * KV_W + 2 * D_MODEL
M_PROMPT = BATCH * SEQ

W_CB = 0
W_CC = CONV_W
W_CH = 2 * CONV_W
W_REST = 3 * CONV_W
OFF_Q = 0
OFF_K = OFF_Q + ATTN_W
OFF_V = OFF_K + KV_W
OFF_GA = OFF_V + KV_W
OFF_GB = OFF_GA + D_MODEL
PROJ_W = OFF_GB + D_MODEL

V7X_VMEM_BYTES = 64 * 1024 * 1024
VMEM_LIMIT = V7X_VMEM_BYTES - 1024 * 1024
SUBLANES = 8


def _cparams(*sem):
    return pltpu.CompilerParams(dimension_semantics=sem, vmem_limit_bytes=VMEM_LIMIT)


def _dot(a, b):
    return jnp.dot(a, b, preferred_element_type=F32)


def _sigmoid(x):
    return 1.0 / (1.0 + jnp.exp(-x))


def _ws_grid(n, bn, mp, bm):
    return (pl.cdiv(n, bn) + 1, mp // bm)


def _ws_rows(j, i):
    return (jnp.where(j > 0, i, 0), 0)


def _ws_wchunk(nj, col0=0):
    return lambda j, i: (i, col0 + jnp.minimum(j, nj - 1))


def _ws_tile(col0=0):
    return lambda j, i: (jnp.where(j > 0, i, 0), col0 + jnp.maximum(j - 1, 0))


def _ws_stile(col0=0):
    return lambda j, i: (0, col0 + jnp.maximum(j - 1, 0))


def _ws_cast(wf_refs, dst_refs):
    i = pl.program_id(1)
    for wf_ref, dst_ref in zip(wf_refs, dst_refs):
        rows = wf_ref.shape[0]
        r0 = pl.multiple_of(i * rows, rows)
        dst_ref[pl.ds(r0, rows), :] = wf_ref[...].astype(BF16)


def _ws_phases(wf_refs, slot0, slot1, compute):
    j = pl.program_id(0)

    @pl.when(j == 0)
    def _():
        _ws_cast(wf_refs, slot0)

    @pl.when((j > 0) & (j % 2 == 1))
    def _():
        _ws_cast(wf_refs, slot1)
        compute(slot0)

    @pl.when((j > 0) & (j % 2 == 0))
    def _():
        _ws_cast(wf_refs, slot0)
        compute(slot1)


def _is_last_row_tile():
    return pl.program_id(1) == pl.num_programs(1) - 1


def _rms_kernel(x_ref, g_ref, o_ref):
    x = x_ref[...]
    r = lax.rsqrt(jnp.mean(x * x, axis=-1, keepdims=True) + EPS)
    o_ref[...] = ((x * r) * g_ref[...]).astype(o_ref.dtype)


def _rmsnorm(x, g, out_dtype, block_rows):
    rows, d = x.shape
    return pl.pallas_call(
        _rms_kernel,
        grid=(rows // block_rows,),
        in_specs=[pl.BlockSpec((block_rows, d), lambda i: (i, 0)),
                  pl.BlockSpec((1, d), lambda i: (0, 0))],
        out_specs=pl.BlockSpec((block_rows, d), lambda i: (i, 0)),
        out_shape=jax.ShapeDtypeStruct((rows, d), out_dtype),
        compiler_params=_cparams("parallel"),
        name="rmsnorm",
    )(x, g.reshape(1, d))


def _proj_kernel(xp_ref, xs_ref, ssp_ref, wf_ref, op_ref, os_ref, w0_ref, w1_ref):
    def compute(slot):
        (w_ref,) = slot
        op_ref[...] = (_inv_rms(ssp_ref) * _dot(xp_ref[...], w_ref[...])).astype(op_ref.dtype)

        @pl.when(_is_last_row_tile())
        def _():
            os_ref[...] = _dot(xs_ref[...], w_ref[...]).astype(os_ref.dtype)

    _ws_phases([wf_ref], [w0_ref], [w1_ref], compute)


def _in_proj(hp, hs, ssp, w, mp, col0, n, bm, bn):
    k = hp.shape[1]
    ms = hs.shape[0]
    grid = _ws_grid(n, bn, mp, bm)
    return pl.pallas_call(
        _proj_kernel,
        grid=grid,
        in_specs=[pl.BlockSpec((bm, k), _ws_rows),
                  pl.BlockSpec((ms, k), lambda j, i: (0, 0)),
                  pl.BlockSpec((bm, LANES), _ws_rows),
                  pl.BlockSpec((k // grid[1], bn), _ws_wchunk(grid[0] - 1, col0 // bn))],
        out_specs=[pl.BlockSpec((bm, bn), _ws_tile()),
                   pl.BlockSpec((ms, bn), _ws_stile())],
        out_shape=[jax.ShapeDtypeStruct((mp, n), BF16),
                   jax.ShapeDtypeStruct((ms, n), BF16)],
        scratch_shapes=[pltpu.VMEM((k, bn), BF16), pltpu.VMEM((k, bn), BF16)],
        compiler_params=_cparams("arbitrary", "arbitrary"),
        name="in_proj",
    )(hp, hs, ssp, w)


def _mixa_kernel(xp_ref, hs_ref, g_ref, wbf_ref, wcf_ref, whf_ref, cw_ref, s0_ref, s1_ref,
                 ap_ref, as_ref, us_ref, st_ref, hp_ref, ssp_ref,
                 wb0_ref, wc0_ref, wh0_ref, wb1_ref, wc1_ref, wh1_ref, carry_ref, ubuf_ref,
                 *, tiles_per_seq):
    i = pl.program_id(1)
    bm = xp_ref.shape[0]

    def compute(slot):
        wb_ref, wc_ref, wh_ref = slot
        w0 = cw_ref[0:1, :]
        w1 = cw_ref[1:2, :]
        w2 = cw_ref[2:3, :]

        @pl.when(i % tiles_per_seq == 0)
        def _():
            carry_ref[...] = jnp.zeros(carry_ref.shape, F32)

        hp_ref[...] = (xp_ref[...] * g_ref[...]).astype(hp_ref.dtype)
        ssp_ref[...] = _lane_partials(xp_ref[...] * xp_ref[...])
        r = _inv_rms(ssp_ref)
        u = (r * _dot(hp_ref[...], wc_ref[...])) * (r * _dot(hp_ref[...], wh_ref[...]))
        ubuf_ref[0:SUBLANES, :] = carry_ref[...]
        ubuf_ref[SUBLANES:SUBLANES + bm, :] = u
        u1 = ubuf_ref[SUBLANES - 1:SUBLANES - 1 + bm, :]
        u2 = ubuf_ref[SUBLANES - 2:SUBLANES - 2 + bm, :]
        z = w0 * u2 + w1 * u1 + w2 * u
        ap_ref[...] = ((r * _dot(hp_ref[...], wb_ref[...])) * z).astype(ap_ref.dtype)
        tail = ubuf_ref[bm:SUBLANES + bm, :]
        carry_ref[...] = tail
        st_ref[0] = tail

        @pl.when(_is_last_row_tile())
        def _():
            us = _dot(hs_ref[...], wc_ref[...]) * _dot(hs_ref[...], wh_ref[...])
            zs = w0 * s0_ref[...] + w1 * s1_ref[...] + w2 * us
            as_ref[...] = (_dot(hs_ref[...], wb_ref[...]) * zs).astype(as_ref.dtype)
            us_ref[...] = us

    _ws_phases([wbf_ref, wcf_ref, whf_ref], [wb0_ref, wc0_ref, wh0_ref],
               [wb1_ref, wc1_ref, wh1_ref], compute)


def _mixer_a(xp, hs, gain, w_in, conv_w, s0, s1, bm, bn):
    mp, k = xp.shape
    ms = hs.shape[0]
    n = CONV_W
    grid = _ws_grid(n, bn, mp, bm)
    nj = grid[0] - 1
    tiles_per_seq = SEQ // bm
    wslot = pltpu.VMEM((k, bn), BF16)
    col = _ws_stile()

    def wchunk(col0):
        return pl.BlockSpec((k // grid[1], bn), _ws_wchunk(nj, col0 // bn))

    def tail_map(j, i):
        return (jnp.where(j > 0, i, 0) // tiles_per_seq, 0, jnp.maximum(j - 1, 0))

    spare = grid[1]

    def operand_map(j, i):
        return (jnp.where(j == 1, i, jnp.where(j == 0, 0, spare)), 0)

    return pl.pallas_call(
        functools.partial(_mixa_kernel, tiles_per_seq=tiles_per_seq),
        grid=grid,
        in_specs=[pl.BlockSpec((bm, k), _ws_rows),
                  pl.BlockSpec((ms, k), lambda j, i: (0, 0)),
                  pl.BlockSpec((1, k), lambda j, i: (0, 0)),
                  wchunk(W_CB), wchunk(W_CC), wchunk(W_CH),
                  pl.BlockSpec((CONV_K, bn), col),
                  pl.BlockSpec((ms, bn), col),
                  pl.BlockSpec((ms, bn), col)],
        out_specs=[pl.BlockSpec((bm, bn), _ws_tile()),
                   pl.BlockSpec((ms, bn), col),
                   pl.BlockSpec((ms, bn), col),
                   pl.BlockSpec((1, SUBLANES, bn), tail_map),
                   pl.BlockSpec((bm, k), operand_map),
                   pl.BlockSpec((bm, LANES), operand_map)],
        out_shape=[jax.ShapeDtypeStruct((mp, n), BF16),
                   jax.ShapeDtypeStruct((ms, n), BF16),
                   jax.ShapeDtypeStruct((ms, n), F32),
                   jax.ShapeDtypeStruct((BATCH, SUBLANES, n), F32),
                   jax.ShapeDtypeStruct((mp + bm, k), BF16),
                   jax.ShapeDtypeStruct((mp + bm, LANES), F32)],
        scratch_shapes=[wslot] * 6 + [pltpu.VMEM((SUBLANES, bn), F32),
                                      pltpu.VMEM((SUBLANES + bm, bn), F32)],
        compiler_params=_cparams("arbitrary", "arbitrary"),
        name="mixer_a",
    )(xp, hs, gain.reshape(1, k), w_in, w_in, w_in, conv_w, s0, s1)


def _bucket_table():
    qi = np.arange(Q_BLOCK)[None, :]
    kj = np.arange(2 * Q_BLOCK)[:, None]
    dist = qi + Q_BLOCK - kj
    max_exact = N_BUCKETS // 2
    d = np.maximum(dist, 0)
    df = np.maximum(d, 1).astype(np.float32)
    large = max_exact + (np.log(df / np.float32(max_exact))
                         / np.float32(math.log(MAX_DISTANCE / max_exact))
                         * np.float32(N_BUCKETS - max_exact)).astype(np.int32)
    large = np.minimum(large, N_BUCKETS - 1)
    bucket = np.where(d < max_exact, d, large).astype(np.int32)
    valid = ((dist >= 0) & (dist <= WINDOW)).astype(np.int32)
    return bucket, valid


def _bias_kernel(rb_ref, bucket_ref, valid_ref, o_ref):
    bucket = bucket_ref[...]
    row = lax.broadcasted_iota(jnp.int32, bucket.shape, 0)
    keep = valid_ref[...] != 0
    keep_first = keep & (row >= Q_BLOCK)

    def head(h, carry):
        acc = jnp.zeros(bucket.shape, F32)
        for b in range(N_BUCKETS):
            acc = jnp.where(bucket == b, rb_ref[b, h], acc)
        o_ref[0, h] = jnp.where(keep, acc, NEG)
        o_ref[1, h] = jnp.where(keep_first, acc, NEG)
        return carry

    lax.fori_loop(0, N_HEADS, head, 0)


def _bias_table(rel_bias):
    bucket, valid = _bucket_table()
    shp = (2 * Q_BLOCK, Q_BLOCK)
    return pl.pallas_call(
        _bias_kernel,
        grid=(1,),
        in_specs=[pl.BlockSpec(memory_space=pltpu.SMEM),
                  pl.BlockSpec(shp, lambda v: (0, 0)),
                  pl.BlockSpec(shp, lambda v: (0, 0))],
        out_specs=pl.BlockSpec((2, N_HEADS) + shp, lambda v: (0, 0, 0, 0)),
        out_shape=jax.ShapeDtypeStruct((2, N_HEADS) + shp, F32),
        compiler_params=_cparams("arbitrary"),
        name="rel_bias_table",
    )(rel_bias, jnp.asarray(bucket), jnp.asarray(valid))


BLOCKS_PER_STEP = 2


def _attn_prompt_kernel(sink_ref, q_ref, kvp_ref, kvc_ref, bias_ref, o_ref):
    scale = HEAD_DIM ** -0.5
    first_variant = jnp.where(pl.program_id(1) == 0, 1, 0)

    def band(blk, col):
        cols = slice(col, col + HEAD_DIM)
        cur = kvc_ref[blk * Q_BLOCK:(blk + 1) * Q_BLOCK, cols]
        if blk == 0:
            return jnp.concatenate([kvp_ref[:, cols], cur], axis=0)
        return kvc_ref[(blk - 1) * Q_BLOCK:(blk + 1) * Q_BLOCK, cols]

    def scores(blk, kv):
        heads = range(kv * GROUP, (kv + 1) * GROUP)
        rows = slice(blk * Q_BLOCK, (blk + 1) * Q_BLOCK)
        variant = first_variant if blk == 0 else 0
        qg = jnp.concatenate(
            [q_ref[rows, h * HEAD_DIM:(h + 1) * HEAD_DIM] for h in heads], axis=0) * scale
        st = lax.dot_general(band(blk, kv * HEAD_DIM), qg, (((1,), (1,)), ((), ())),
                             preferred_element_type=F32)
        return st + jnp.concatenate([bias_ref[variant, h] for h in heads], axis=1)

    def finish(blk, kv, st):
        heads = range(kv * GROUP, (kv + 1) * GROUP)
        vband = band(blk, KV_W + kv * HEAD_DIM)
        sk = jnp.concatenate([jnp.full((1, Q_BLOCK), sink_ref[h], F32) for h in heads], axis=1)
        m = jnp.maximum(jnp.max(st, axis=0, keepdims=True), sk)
        e = jnp.exp(st - m)
        denom = jnp.sum(e, axis=0, keepdims=True) + jnp.exp(sk - m)
        ot = lax.dot_general(vband, e.astype(BF16), (((0,), (0,)), ((), ())),
                             preferred_element_type=F32) * (1.0 / denom)
        o2 = jnp.concatenate(
            [ot[:, g * Q_BLOCK:(g + 1) * Q_BLOCK] for g in range(GROUP)], axis=0)
        rows = slice(blk * Q_BLOCK, (blk + 1) * Q_BLOCK)
        gs = slice(kv * GROUP * HEAD_DIM, (kv + 1) * GROUP * HEAD_DIM)
        o_ref[rows, gs] = o2.T.astype(o_ref.dtype)

    groups = [(blk, kv) for blk in range(BLOCKS_PER_STEP) for kv in range(N_KV_HEADS)]
    st = scores(*groups[0])
    for idx, grp in enumerate(groups):
        st_next = scores(*groups[idx + 1]) if idx + 1 < len(groups) else None
        finish(*grp, st)
        st = st_next


def _attn_prompt(proj_p, bias_tab, sinks):
    rows = BLOCKS_PER_STEP * Q_BLOCK
    steps = SEQ // rows
    qcol = OFF_Q // ATTN_W
    kvcol = OFF_K // (2 * KV_W)

    def prev(b, n):
        return b * (SEQ // Q_BLOCK) + jnp.maximum(n * BLOCKS_PER_STEP - 1, 0)

    return pl.pallas_call(
        _attn_prompt_kernel,
        grid=(BATCH, steps),
        in_specs=[pl.BlockSpec(memory_space=pltpu.SMEM),
                  pl.BlockSpec((rows, ATTN_W), lambda b, n: (b * steps + n, qcol)),
                  pl.BlockSpec((Q_BLOCK, 2 * KV_W), lambda b, n: (prev(b, n), kvcol)),
                  pl.BlockSpec((rows, 2 * KV_W), lambda b, n: (b * steps + n, kvcol)),
                  pl.BlockSpec((2, N_HEADS, 2 * Q_BLOCK, Q_BLOCK), lambda b, n: (0, 0, 0, 0))],
        out_specs=pl.BlockSpec((rows, ATTN_W), lambda b, n: (b * steps + n, 0)),
        out_shape=jax.ShapeDtypeStruct((M_PROMPT, ATTN_W), BF16),
        compiler_params=_cparams("parallel", "arbitrary"),
        name="attn_prompt",
    )(sinks, proj_p, proj_p, proj_p, bias_tab)


SAMPLES_PER_STEP = 8


def _attn_sample_kernel(q_ref, kn_ref, vn_ref, kt_ref, vt_ref, bias_ref, sink_ref,
                        o_ref, kto_ref, vto_ref):
    g = SAMPLES_PER_STEP
    scale = HEAD_DIM ** -0.5
    erow = lax.broadcasted_iota(jnp.int32, (HEAD_DIM, KV_W), 0)
    ecol = lax.broadcasted_iota(jnp.int32, (HEAD_DIM, KV_W), 1)
    expand = (ecol % HEAD_DIM == erow).astype(BF16)
    hrow = lax.broadcasted_iota(jnp.int32, (N_HEADS, KV_W), 0)
    hcol = lax.broadcasted_iota(jnp.int32, (N_HEADS, KV_W), 1)
    own = ((hrow // GROUP) == (hcol // HEAD_DIM))[None]
    bias_w = bias_ref[:, 0:WINDOW][None]
    bias_n = bias_ref[:, WINDOW:WINDOW + 1][None]
    sk = sink_ref[...][None]
    contract_last = (((1,), (1,)), ((), ()))

    q_all = q_ref[...].reshape(g * N_HEADS, HEAD_DIM) * scale
    qrow = jnp.where(own, _dot(q_all, expand).reshape(g, N_HEADS, KV_W), 0.0)
    qrow_b = qrow.astype(BF16)
    kn = kn_ref[...][:, None, :]
    vn = vn_ref[...][:, None, :]
    s_w = jnp.stack([_dot(qrow_b[b], kt_ref[b].astype(BF16)) for b in range(g)])
    s_w = s_w + bias_w
    s_n = jnp.sum(qrow * kn, axis=-1, keepdims=True) + bias_n
    m = jnp.maximum(jnp.maximum(jnp.max(s_w, axis=-1, keepdims=True), s_n), sk)
    e_w = jnp.exp(s_w - m)
    e_n = jnp.exp(s_n - m)
    r = 1.0 / (jnp.sum(e_w, axis=-1, keepdims=True) + e_n + jnp.exp(sk - m))
    p_w = (e_w * r).astype(BF16)
    o_all = jnp.stack([lax.dot_general(p_w[b], vt_ref[b].astype(BF16), contract_last,
                                       preferred_element_type=F32) for b in range(g)])
    o_own = jnp.where(own, o_all + (e_n * r) * vn, 0.0)
    o = o_own[:, :, 0:HEAD_DIM]
    for kv in range(1, N_KV_HEADS):
        o = o + o_own[:, :, kv * HEAD_DIM:(kv + 1) * HEAD_DIM]
    o_ref[...] = o.astype(o_ref.dtype)

    rows = jnp.concatenate([kn_ref[...], vn_ref[...]], axis=0).astype(BF16)
    place = (lax.broadcasted_iota(jnp.int32, (2 * g, 2 * g * WINDOW), 1)
             == WINDOW * lax.broadcasted_iota(jnp.int32, (2 * g, 2 * g * WINDOW), 0)
             ).astype(BF16)
    newcols = lax.dot_general(rows, place, (((0,), (0,)), ((), ())),
                              preferred_element_type=F32)
    first = lax.broadcasted_iota(jnp.int32, (KV_W, WINDOW), 1) == 0
    for b in range(g):
        kcol = newcols[:, b * WINDOW:(b + 1) * WINDOW]
        vcol = newcols[:, (g + b) * WINDOW:(g + b + 1) * WINDOW]
        kto_ref[b] = pltpu.roll(jnp.where(first, kcol, kt_ref[b]), WINDOW - 1, 1)
        vto_ref[b] = pltpu.roll(jnp.where(first, vcol, vt_ref[b]), WINDOW - 1, 1)


def _attn_sample(q3, k_new, v_new, k_buf, v_buf, bias_s, sinks):
    g = SAMPLES_PER_STEP
    nb = DEC_BATCH // g
    win = pl.BlockSpec((g, KV_W, WINDOW), lambda i: (i, 0, 0))
    row = pl.BlockSpec((g, KV_W), lambda i: (i, 0))
    return pl.pallas_call(
        _attn_sample_kernel,
        grid=(nb,),
        in_specs=[pl.BlockSpec((g, N_HEADS, HEAD_DIM), lambda i: (i, 0, 0)),
                  row, row, win, win,
                  pl.BlockSpec((N_HEADS, WINDOW + 1), lambda i: (0, 0)),
                  pl.BlockSpec((N_HEADS, 1), lambda i: (0, 0))],
        out_specs=[pl.BlockSpec((g, N_HEADS, HEAD_DIM), lambda i: (i, 0, 0)), win, win],
        out_shape=[jax.ShapeDtypeStruct((DEC_BATCH, N_HEADS, HEAD_DIM), BF16),
                   jax.ShapeDtypeStruct((DEC_BATCH, KV_W, WINDOW), F32),
                   jax.ShapeDtypeStruct((DEC_BATCH, KV_W, WINDOW), F32)],
        compiler_params=_cparams("parallel"),
        name="attn_sample",
    )(q3, k_new, v_new, k_buf, v_buf, bias_s, sinks.reshape(N_HEADS, 1))


def _merge_kernel(ap_ref, bp_ref, as_ref, bs_ref, waf_ref, wbf_ref,
                  gap_ref, gbp_ref, gas_ref, gbs_ref, op_ref, os_ref,
                  wa0_ref, wb0_ref, wa1_ref, wb1_ref):
    def compute(slot):
        wa_ref, wb_ref = slot

        def merged(a_ref, b_ref, ga_ref, gb_ref):
            return (_sigmoid(ga_ref[...].astype(F32)) * _dot(a_ref[...], wa_ref[...])
                    + _sigmoid(gb_ref[...].astype(F32)) * _dot(b_ref[...], wb_ref[...]))

        op_ref[...] = merged(ap_ref, bp_ref, gap_ref, gbp_ref).astype(op_ref.dtype)

        @pl.when(_is_last_row_tile())
        def _():
            os_ref[...] = merged(as_ref, bs_ref, gas_ref, gbs_ref).astype(os_ref.dtype)

    _ws_phases([waf_ref, wbf_ref], [wa0_ref, wb0_ref], [wa1_ref, wb1_ref], compute)


def _branch_merge(a_p, o_p, a_s, o_s, wa, wb, proj_p, proj_s, bm, bn):
    mp, k = a_p.shape
    ms = a_s.shape[0]
    n = wa.shape[1]
    grid = _ws_grid(n, bn, mp, bm)
    ga0 = OFF_GA // bn
    gb0 = OFF_GB // bn
    wchunk = pl.BlockSpec((k // grid[1], bn), _ws_wchunk(grid[0] - 1))
    wslot = pltpu.VMEM((k, bn), BF16)
    return pl.pallas_call(
        _merge_kernel,
        grid=grid,
        in_specs=[pl.BlockSpec((bm, k), _ws_rows),
                  pl.BlockSpec((bm, k), _ws_rows),
                  pl.BlockSpec((ms, k), lambda j, i: (0, 0)),
                  pl.BlockSpec((ms, k), lambda j, i: (0, 0)),
                  wchunk, wchunk,
                  pl.BlockSpec((bm, bn), _ws_tile(ga0)),
                  pl.BlockSpec((bm, bn), _ws_tile(gb0)),
                  pl.BlockSpec((ms, bn), _ws_stile(ga0)),
                  pl.BlockSpec((ms, bn), _ws_stile(gb0))],
        out_specs=[pl.BlockSpec((bm, bn), _ws_tile()),
                   pl.BlockSpec((ms, bn), _ws_stile())],
        out_shape=[jax.ShapeDtypeStruct((mp, n), BF16),
                   jax.ShapeDtypeStruct((ms, n), BF16)],
        scratch_shapes=[wslot, wslot, wslot, wslot],
        compiler_params=_cparams("arbitrary", "arbitrary"),
        name="branch_merge",
    )(a_p, o_p, a_s, o_s, wa, wb, proj_p, proj_p, proj_s, proj_s)


def _resid_kernel(xp_ref, xs_ref, wf_ref, rp_ref, rs_ref, op_ref, os_ref, w0_ref, w1_ref):
    def compute(slot):
        (w_ref,) = slot
        op_ref[...] = rp_ref[...] + _dot(xp_ref[...], w_ref[...])

        @pl.when(_is_last_row_tile())
        def _():
            os_ref[...] = rs_ref[...] + _dot(xs_ref[...], w_ref[...])

    _ws_phases([wf_ref], [w0_ref], [w1_ref], compute)


def _resid_matmul(xp, xs, w, rp, rs, bm, bn, name):
    mp, k = xp.shape
    ms = xs.shape[0]
    n = w.shape[1]
    grid = _ws_grid(n, bn, mp, bm)
    return pl.pallas_call(
        _resid_kernel,
        grid=grid,
        in_specs=[pl.BlockSpec((bm, k), _ws_rows),
                  pl.BlockSpec((ms, k), lambda j, i: (0, 0)),
                  pl.BlockSpec((k // grid[1], bn), _ws_wchunk(grid[0] - 1)),
                  pl.BlockSpec((bm, bn), _ws_tile()),
                  pl.BlockSpec((ms, bn), _ws_stile())],
        out_specs=[pl.BlockSpec((bm, bn), _ws_tile()),
                   pl.BlockSpec((ms, bn), _ws_stile())],
        out_shape=[jax.ShapeDtypeStruct((mp, n), F32),
                   jax.ShapeDtypeStruct((ms, n), F32)],
        scratch_shapes=[pltpu.VMEM((k, bn), BF16), pltpu.VMEM((k, bn), BF16)],
        compiler_params=_cparams("arbitrary", "arbitrary"),
        name=name,
    )(xp, xs, w, rp, rs)


LANES = 128


def _lane_partials(v):
    part = v[:, 0:LANES]
    for c in range(1, v.shape[1] // LANES):
        part = part + v[:, c * LANES:(c + 1) * LANES]
    return part


def _outproj_kernel(xp_ref, xs_ref, wf_ref, rp_ref, rs_ref, g_ref,
                    op_ref, os_ref, bp_ref, bs_ref, ssp_ref, sss_ref,
                    w0_ref, w1_ref, accp_ref, accs_ref):
    j = pl.program_id(0)
    i = pl.program_id(1)
    bm = xp_ref.shape[0]

    def compute(slot):
        (w_ref,) = slot

        def tile(x_ref, r_ref, o_ref, b_ref, acc_view, ss_ref):
            x1 = r_ref[...] + _dot(x_ref[...], w_ref[...])
            o_ref[...] = x1
            b_ref[...] = (x1 * g_ref[...]).astype(b_ref.dtype)
            total = jnp.where(j == 1, 0.0, acc_view[...]) + _lane_partials(x1 * x1)
            acc_view[...] = total
            ss_ref[...] = total

        rows = pl.ds(pl.multiple_of(i * bm, bm), bm)
        tile(xp_ref, rp_ref, op_ref, bp_ref, accp_ref.at[rows, :], ssp_ref)

        @pl.when(_is_last_row_tile())
        def _():
            tile(xs_ref, rs_ref, os_ref, bs_ref, accs_ref, sss_ref)

    _ws_phases([wf_ref], [w0_ref], [w1_ref], compute)


def _out_proj(xp, xs, w, rp, rs, gain, bm, bn):
    mp, k = xp.shape
    ms = xs.shape[0]
    n = w.shape[1]
    grid = _ws_grid(n, bn, mp, bm)
    last = grid[0] - 1
    ssp_map = lambda j, i: (jnp.where(j == last, i, 0), 0)
    return pl.pallas_call(
        _outproj_kernel,
        grid=grid,
        in_specs=[pl.BlockSpec((bm, k), _ws_rows),
                  pl.BlockSpec((ms, k), lambda j, i: (0, 0)),
                  pl.BlockSpec((k // grid[1], bn), _ws_wchunk(grid[0] - 1)),
                  pl.BlockSpec((bm, bn), _ws_tile()),
                  pl.BlockSpec((ms, bn), _ws_stile()),
                  pl.BlockSpec((1, bn), _ws_stile())],
        out_specs=[pl.BlockSpec((bm, bn), _ws_tile()),
                   pl.BlockSpec((ms, bn), _ws_stile()),
                   pl.BlockSpec((bm, bn), _ws_tile()),
                   pl.BlockSpec((ms, bn), _ws_stile()),
                   pl.BlockSpec((bm, LANES), ssp_map),
                   pl.BlockSpec((ms, LANES), lambda j, i: (0, 0))],
        out_shape=[jax.ShapeDtypeStruct((mp, n), F32),
                   jax.ShapeDtypeStruct((ms, n), F32),
                   jax.ShapeDtypeStruct((mp, n), BF16),
                   jax.ShapeDtypeStruct((ms, n), BF16),
                   jax.ShapeDtypeStruct((mp, LANES), F32),
                   jax.ShapeDtypeStruct((ms, LANES), F32)],
        scratch_shapes=[pltpu.VMEM((k, bn), BF16), pltpu.VMEM((k, bn), BF16),
                        pltpu.VMEM((mp, LANES), F32), pltpu.VMEM((ms, LANES), F32)],
        compiler_params=_cparams("arbitrary", "arbitrary"),
        name="out_proj",
    )(xp, xs, w, rp, rs, gain.reshape(1, n))


def _inv_rms(ss_ref):
    return lax.rsqrt(jnp.sum(ss_ref[...], axis=-1, keepdims=True) * (1.0 / D_MODEL) + EPS)


def _ffn_up_kernel(hp_ref, hs_ref, ssp_ref, sss_ref, wgf_ref, wuf_ref, cw_ref, cb_ref,
                   s0_ref, s1_ref, fp_ref, fs_ref, gs_ref, st_ref,
                   wg0_ref, wu0_ref, wg1_ref, wu1_ref, carry_ref, gbuf_ref, *, tiles_per_seq):
    i = pl.program_id(1)
    bm = hp_ref.shape[0]

    def compute(slot):
        wg_ref, wu_ref = slot
        w0 = cw_ref[0:1, :]
        w1 = cw_ref[1:2, :]
        w2 = cw_ref[2:3, :]
        bias = cb_ref[...]

        @pl.when(i % tiles_per_seq == 0)
        def _():
            carry_ref[...] = jnp.zeros(carry_ref.shape, F32)

        r = _inv_rms(ssp_ref)
        g = r * _dot(hp_ref[...], wg_ref[...])
        gbuf_ref[0:SUBLANES, :] = carry_ref[...]
        gbuf_ref[SUBLANES:SUBLANES + bm, :] = g
        g1 = gbuf_ref[SUBLANES - 1:SUBLANES - 1 + bm, :]
        g2 = gbuf_ref[SUBLANES - 2:SUBLANES - 2 + bm, :]
        gc = w0 * g2 + w1 * g1 + w2 * g + bias
        up = r * _dot(hp_ref[...], wu_ref[...])
        fp_ref[...] = (gc * _sigmoid(gc) * up).astype(fp_ref.dtype)
        tail = gbuf_ref[bm:SUBLANES + bm, :]
        carry_ref[...] = tail
        st_ref[0] = tail

        @pl.when(_is_last_row_tile())
        def _():
            rs = _inv_rms(sss_ref)
            gsm = rs * _dot(hs_ref[...], wg_ref[...])
            ups = rs * _dot(hs_ref[...], wu_ref[...])
            gcs = w0 * s0_ref[...] + w1 * s1_ref[...] + w2 * gsm + bias
            fs_ref[...] = (gcs * _sigmoid(gcs) * ups).astype(fs_ref.dtype)
            gs_ref[...] = gsm

    _ws_phases([wgf_ref, wuf_ref], [wg0_ref, wu0_ref], [wg1_ref, wu1_ref], compute)


def _ffn_up(hp, hs, ssp, sss, wg, wu, cw, cb, s0, s1, bm, bn):
    mp, k = hp.shape
    ms = hs.shape[0]
    n = wg.shape[1]
    grid = _ws_grid(n, bn, mp, bm)
    tiles_per_seq = SEQ // bm
    wchunk = pl.BlockSpec((k // grid[1], bn), _ws_wchunk(grid[0] - 1))
    wslot = pltpu.VMEM((k, bn), BF16)
    col = _ws_stile()

    def tail_map(j, i):
        return (jnp.where(j > 0, i, 0) // tiles_per_seq, 0, jnp.maximum(j - 1, 0))

    return pl.pallas_call(
        functools.partial(_ffn_up_kernel, tiles_per_seq=tiles_per_seq),
        grid=grid,
        in_specs=[pl.BlockSpec((bm, k), _ws_rows),
                  pl.BlockSpec((ms, k), lambda j, i: (0, 0)),
                  pl.BlockSpec((bm, LANES), _ws_rows),
                  pl.BlockSpec((ms, LANES), lambda j, i: (0, 0)),
                  wchunk, wchunk,
                  pl.BlockSpec((FFN_CONV_K, bn), col),
                  pl.BlockSpec((1, bn), col),
                  pl.BlockSpec((ms, bn), col),
                  pl.BlockSpec((ms, bn), col)],
        out_specs=[pl.BlockSpec((bm, bn), _ws_tile()),
                   pl.BlockSpec((ms, bn), col),
                   pl.BlockSpec((ms, bn), col),
                   pl.BlockSpec((1, SUBLANES, bn), tail_map)],
        out_shape=[jax.ShapeDtypeStruct((mp, n), BF16),
                   jax.ShapeDtypeStruct((ms, n), BF16),
                   jax.ShapeDtypeStruct((ms, n), F32),
                   jax.ShapeDtypeStruct((BATCH, SUBLANES, n), F32)],
        scratch_shapes=[wslot, wslot, wslot, wslot,
                        pltpu.VMEM((SUBLANES, bn), F32),
                        pltpu.VMEM((SUBLANES + bm, bn), F32)],
        compiler_params=_cparams("arbitrary", "arbitrary"),
        name="ffn_up",
    )(hp, hs, ssp, sss, wg, wu, cw, cb, s0, s1)


def _window_to_kernel(state):
    return jnp.transpose(state[0], (0, 2, 3, 1)).reshape(DEC_BATCH, KV_W, WINDOW)


def _window_from_kernel(win):
    win = win.reshape(DEC_BATCH, N_KV_HEADS, HEAD_DIM, WINDOW)
    return jnp.transpose(win, (0, 3, 1, 2))[None]


def kernel(x_prompt, x_sample, state_k_window, state_v_window, state_conv, state_ffn_conv,
           attn_norm_g, w_in, conv_w, w_branch_a, w_branch_b, sinks, w_out, ffn_norm_g,
           w_ffn_gate, w_ffn_up, ffn_conv_w, ffn_conv_b, w_ffn_down, rel_bias,
           final_norm_g):
    xp = x_prompt.reshape(M_PROMPT, D_MODEL)
    xs = x_sample.reshape(DEC_BATCH, D_MODEL)

    fs0 = state_ffn_conv[0][:, 0, :]
    fs1 = state_ffn_conv[0][:, 1, :]

    cs0 = state_conv[0][:, 0, :]
    cs1 = state_conv[0][:, 1, :]
    hs = _rmsnorm(xs, attn_norm_g[0], BF16, DEC_BATCH)
    a_p, a_s, u_s, u_tail, hp, hssp = _mixer_a(
        xp, hs, attn_norm_g[0], w_in[0], conv_w[0], cs0, cs1, 512, 512)
    proj_p, proj_s = _in_proj(hp, hs, hssp, w_in[0], M_PROMPT, W_REST, PROJ_W, 1024, 1024)

    bias_tab = _bias_table(rel_bias)
    o_p = _attn_prompt(proj_p, bias_tab, sinks[0])
    q3 = proj_s[:, OFF_Q:OFF_Q + ATTN_W].reshape(DEC_BATCH, N_HEADS, HEAD_DIM)
    k_new = proj_s[:, OFF_K:OFF_K + KV_W].astype(F32)
    v_new = proj_s[:, OFF_V:OFF_V + KV_W].astype(F32)
    bias_s = bias_tab[0, :, Q_BLOCK - 1:, Q_BLOCK - 1]
    o_s3, k_win_s, v_win_s = _attn_sample(
        q3, k_new, v_new,
        _window_to_kernel(state_k_window), _window_to_kernel(state_v_window),
        bias_s, sinks[0])
    o_s = o_s3.reshape(DEC_BATCH, ATTN_W)

    mg_p, mg_s = _branch_merge(a_p, o_p, a_s, o_s, w_branch_a[0], w_branch_b[0],
                               proj_p, proj_s, 1024, 1024)
    x1p, x1s, xbp, xbs, ssp, sss = _out_proj(mg_p, mg_s, w_out[0], xp, xs, ffn_norm_g[0],
                                             512, 1024)

    f_p, f_s, g_s, g_tail = _ffn_up(xbp, xbs, ssp, sss, w_ffn_gate[0], w_ffn_up[0],
                                    ffn_conv_w[0], ffn_conv_b, fs0, fs1, 1024, 512)
    x2p, x2s = _resid_matmul(f_p, f_s, w_ffn_down[0], x1p, x1s, 512, 512, "ffn_down")
    y_p = _rmsnorm(x2p, final_norm_g, F32, 256)
    y_s = _rmsnorm(x2s, final_norm_g, F32, DEC_BATCH)

    kv_p = proj_p.reshape(BATCH, SEQ, PROJ_W)[:, SEQ - WINDOW:, OFF_K:OFF_GA].astype(F32)
    k_win_p = kv_p[:, :, :KV_W].reshape(1, BATCH, WINDOW, N_KV_HEADS, HEAD_DIM)
    v_win_p = kv_p[:, :, KV_W:].reshape(1, BATCH, WINDOW, N_KV_HEADS, HEAD_DIM)
    conv_p = u_tail[:, SUBLANES - (CONV_K - 1):, :]
    ffn_p = g_tail[:, SUBLANES - (FFN_CONV_K - 1):, :]
    conv_s = jnp.stack([cs1, u_s], axis=1)
    ffn_s = jnp.stack([fs1, g_s], axis=1)

    return (y_p.reshape(BATCH, SEQ, D_MODEL),
            y_s.reshape(DEC_BATCH, 1, D_MODEL),
            k_win_p, v_win_p, conv_p[None], ffn_p[None],
            _window_from_kernel(k_win_s), _window_from_kernel(v_win_s),
            conv_s[None], ffn_s[None])
```

```python
import functools
import math

import numpy as np
import jax
import jax.numpy as jnp
from jax import lax
from jax.experimental import pallas as pl
from jax.experimental.pallas import tpu as pltpu

F32 = jnp.float32
BF16 = jnp.bfloat16

D_MODEL = 4096
BATCH = 4
SEQ = 2048
DEC_BATCH = 128
N_HEADS = 32
N_KV_HEADS = 8
HEAD_DIM = 64
GROUP = N_HEADS // N_KV_HEADS
ATTN_W = N_HEADS * HEAD_DIM
KV_W = N_KV_HEADS * HEAD_DIM
CONV_W = D_MODEL // 2
CONV_K = 3
WINDOW = 128
Q_BLOCK = 128
N_BUCKETS = 32
MAX_DISTANCE = 128
D_FF = 11008
FFN_CONV_K = 3
EPS = 1e-5
NEG = -1e30
IN_W = 3 * CONV_W + ATTN_W + 2 * KV_W + 2 * D_MODEL
M_PROMPT = BATCH * SEQ

W_CB = 0
W_CC = CONV_W
W_CH = 2 * CONV_W
W_REST = 3 * CONV_W
OFF_Q = 0
OFF_K = OFF_Q + ATTN_W
OFF_V = OFF_K + KV_W
OFF_GA = OFF_V + KV_W
OFF_GB = OFF_GA + D_MODEL
PROJ_W = OFF_GB + D_MODEL

V7X_VMEM_BYTES = 64 * 1024 * 1024
VMEM_LIMIT = V7X_VMEM_BYTES - 1024 * 1024
SUBLANES = 8


def _cparams(*sem):
    return pltpu.CompilerParams(dimension_semantics=sem, vmem_limit_bytes=VMEM_LIMIT)


def _dot(a, b):
    return jnp.dot(a, b, preferred_element_type=F32)


def _sigmoid(x):
    return 1.0 / (1.0 + jnp.exp(-x))


def _ws_grid(n, bn, mp, bm):
    return (pl.cdiv(n, bn) + 1, mp // bm)


def _ws_rows(j, i):
    return (jnp.where(j > 0, i, 0), 0)


def _ws_wchunk(nj, col0=0):
    return lambda j, i: (i, col0 + jnp.minimum(j, nj - 1))


def _ws_tile(col0=0):
    return lambda j, i: (jnp.where(j > 0, i, 0), col0 + jnp.maximum(j - 1, 0))


def _ws_stile(col0=0):
    return lambda j, i: (0, col0 + jnp.maximum(j - 1, 0))


def _ws_cast(wf_refs, dst_refs):
    i = pl.program_id(1)
    for wf_ref, dst_ref in zip(wf_refs, dst_refs):
        rows = wf_ref.shape[0]
        r0 = pl.multiple_of(i * rows, rows)
        dst_ref[pl.ds(r0, rows), :] = wf_ref[...].astype(BF16)


def _ws_phases(wf_refs, slot0, slot1, compute, compute_last=None, last=None):
    j = pl.program_id(0)
    regular = (j > 0) if compute_last is None else ((j > 0) & (j < last))

    @pl.when(j == 0)
    def _():
        _ws_cast(wf_refs, slot0)

    @pl.when(regular & (j % 2 == 1))
    def _():
        _ws_cast(wf_refs, slot1)
        compute(slot0)

    @pl.when(regular & (j % 2 == 0))
    def _():
        _ws_cast(wf_refs, slot0)
        compute(slot1)

    if compute_last is not None:
        @pl.when(j == last)
        def _():
            compute_last(slot0 if last % 2 == 1 else slot1)


def _is_last_row_tile():
    return pl.program_id(1) == pl.num_programs(1) - 1


def _rms_kernel(x_ref, g_ref, o_ref):
    x = x_ref[...]
    r = lax.rsqrt(jnp.mean(x * x, axis=-1, keepdims=True) + EPS)
    o_ref[...] = ((x * r) * g_ref[...]).astype(o_ref.dtype)


def _rmsnorm(x, g, out_dtype, block_rows):
    rows, d = x.shape
    return pl.pallas_call(
        _rms_kernel,
        grid=(rows // block_rows,),
        in_specs=[pl.BlockSpec((block_rows, d), lambda i: (i, 0)),
                  pl.BlockSpec((1, d), lambda i: (0, 0))],
        out_specs=pl.BlockSpec((block_rows, d), lambda i: (i, 0)),
        out_shape=jax.ShapeDtypeStruct((rows, d), out_dtype),
        compiler_params=_cparams("parallel"),
        name="rmsnorm",
    )(x, g.reshape(1, d))


def _proj_kernel(xp_ref, xs_ref, ssp_ref, wf_ref, op_ref, os_ref, w0_ref, w1_ref):
    def compute(slot):
        (w_ref,) = slot
        op_ref[...] = (_inv_rms(ssp_ref) * _dot(xp_ref[...], w_ref[...])).astype(op_ref.dtype)

        @pl.when(_is_last_row_tile())
        def _():
            os_ref[...] = _dot(xs_ref[...], w_ref[...]).astype(os_ref.dtype)

    _ws_phases([wf_ref], [w0_ref], [w1_ref], compute)


def _in_proj(hp, hs, ssp, w, mp, col0, n, bm, bn):
    k = hp.shape[1]
    ms = hs.shape[0]
    grid = _ws_grid(n, bn, mp, bm)
    return pl.pallas_call(
        _proj_kernel,
        grid=grid,
        in_specs=[pl.BlockSpec((bm, k), _ws_rows),
                  pl.BlockSpec((ms, k), lambda j, i: (0, 0)),
                  pl.BlockSpec((bm, LANES), _ws_rows),
                  pl.BlockSpec((k // grid[1], bn), _ws_wchunk(grid[0] - 1, col0 // bn))],
        out_specs=[pl.BlockSpec((bm, bn), _ws_tile()),
                   pl.BlockSpec((ms, bn), _ws_stile())],
        out_shape=[jax.ShapeDtypeStruct((mp, n), BF16),
                   jax.ShapeDtypeStruct((ms, n), BF16)],
        scratch_shapes=[pltpu.VMEM((k, bn), BF16), pltpu.VMEM((k, bn), BF16)],
        compiler_params=_cparams("arbitrary", "arbitrary"),
        name="in_proj",
    )(hp, hs, ssp, w)


def _mixa_kernel(xp_ref, hs_ref, g_ref, wbf_ref, wcf_ref, whf_ref, cw_ref, s0_ref, s1_ref,
                 ap_ref, as_ref, us_ref, st_ref, hp_ref, ssp_ref,
                 wb0_ref, wc0_ref, wh0_ref, wb1_ref, wc1_ref, wh1_ref, carry_ref, ubuf_ref,
                 *, tiles_per_seq):
    i = pl.program_id(1)
    bm = xp_ref.shape[0]

    def compute(slot):
        wb_ref, wc_ref, wh_ref = slot
        w0 = cw_ref[0:1, :]
        w1 = cw_ref[1:2, :]
        w2 = cw_ref[2:3, :]

        @pl.when(i % tiles_per_seq == 0)
        def _():
            carry_ref[...] = jnp.zeros(carry_ref.shape, F32)

        hp_ref[...] = (xp_ref[...] * g_ref[...]).astype(hp_ref.dtype)
        ssp_ref[...] = _lane_partials(xp_ref[...] * xp_ref[...])
        r = _inv_rms(ssp_ref)
        u = (r * _dot(hp_ref[...], wc_ref[...])) * (r * _dot(hp_ref[...], wh_ref[...]))
        ubuf_ref[0:SUBLANES, :] = carry_ref[...]
        ubuf_ref[SUBLANES:SUBLANES + bm, :] = u
        u1 = ubuf_ref[SUBLANES - 1:SUBLANES - 1 + bm, :]
        u2 = ubuf_ref[SUBLANES - 2:SUBLANES - 2 + bm, :]
        z = w0 * u2 + w1 * u1 + w2 * u
        ap_ref[...] = ((r * _dot(hp_ref[...], wb_ref[...])) * z).astype(ap_ref.dtype)
        tail = ubuf_ref[bm:SUBLANES + bm, :]
        carry_ref[...] = tail
        st_ref[0] = tail

        @pl.when(_is_last_row_tile())
        def _():
            us = _dot(hs_ref[...], wc_ref[...]) * _dot(hs_ref[...], wh_ref[...])
            zs = w0 * s0_ref[...] + w1 * s1_ref[...] + w2 * us
            as_ref[...] = (_dot(hs_ref[...], wb_ref[...]) * zs).astype(as_ref.dtype)
            us_ref[...] = us

    _ws_phases([wbf_ref, wcf_ref, whf_ref], [wb0_ref, wc0_ref, wh0_ref],
               [wb1_ref, wc1_ref, wh1_ref], compute)


def _mixer_a(xp, hs, gain, w_in, conv_w, s0, s1, bm, bn):
    mp, k = xp.shape
    ms = hs.shape[0]
    n = CONV_W
    grid = _ws_grid(n, bn, mp, bm)
    nj = grid[0] - 1
    tiles_per_seq = SEQ // bm
    wslot = pltpu.VMEM((k, bn), BF16)
    col = _ws_stile()

    def wchunk(col0):
        return pl.BlockSpec((k // grid[1], bn), _ws_wchunk(nj, col0 // bn))

    def tail_map(j, i):
        return (jnp.where(j > 0, i, 0) // tiles_per_seq, 0, jnp.maximum(j - 1, 0))

    spare = grid[1]

    def operand_map(j, i):
        return (jnp.where(j == 1, i, jnp.where(j == 0, 0, spare)), 0)

    return pl.pallas_call(
        functools.partial(_mixa_kernel, tiles_per_seq=tiles_per_seq),
        grid=grid,
        in_specs=[pl.BlockSpec((bm, k), _ws_rows),
                  pl.BlockSpec((ms, k), lambda j, i: (0, 0)),
                  pl.BlockSpec((1, k), lambda j, i: (0, 0)),
                  wchunk(W_CB), wchunk(W_CC), wchunk(W_CH),
                  pl.BlockSpec((CONV_K, bn), col),
                  pl.BlockSpec((ms, bn), col),
                  pl.BlockSpec((ms, bn), col)],
        out_specs=[pl.BlockSpec((bm, bn), _ws_tile()),
                   pl.BlockSpec((ms, bn), col),
                   pl.BlockSpec((ms, bn), col),
                   pl.BlockSpec((1, SUBLANES, bn), tail_map),
                   pl.BlockSpec((bm, k), operand_map),
                   pl.BlockSpec((bm, LANES), operand_map)],
        out_shape=[jax.ShapeDtypeStruct((mp, n), BF16),
                   jax.ShapeDtypeStruct((ms, n), BF16),
                   jax.ShapeDtypeStruct((ms, n), F32),
                   jax.ShapeDtypeStruct((BATCH, SUBLANES, n), F32),
                   jax.ShapeDtypeStruct((mp + bm, k), BF16),
                   jax.ShapeDtypeStruct((mp + bm, LANES), F32)],
        scratch_shapes=[wslot] * 6 + [pltpu.VMEM((SUBLANES, bn), F32),
                                      pltpu.VMEM((SUBLANES + bm, bn), F32)],
        compiler_params=_cparams("arbitrary", "arbitrary"),
        name="mixer_a",
    )(xp, hs, gain.reshape(1, k), w_in, w_in, w_in, conv_w, s0, s1)


def _bucket_table():
    qi = np.arange(Q_BLOCK)[None, :]
    kj = np.arange(2 * Q_BLOCK)[:, None]
    dist = qi + Q_BLOCK - kj
    max_exact = N_BUCKETS // 2
    d = np.maximum(dist, 0)
    df = np.maximum(d, 1).astype(np.float32)
    large = max_exact + (np.log(df / np.float32(max_exact))
                         / np.float32(math.log(MAX_DISTANCE / max_exact))
                         * np.float32(N_BUCKETS - max_exact)).astype(np.int32)
    large = np.minimum(large, N_BUCKETS - 1)
    bucket = np.where(d < max_exact, d, large).astype(np.int32)
    valid = ((dist >= 0) & (dist <= WINDOW)).astype(np.int32)
    return bucket, valid


def _bias_kernel(rb_ref, bucket_ref, valid_ref, o_ref):
    bucket = bucket_ref[...]
    row = lax.broadcasted_iota(jnp.int32, bucket.shape, 0)
    keep = valid_ref[...] != 0
    keep_first = keep & (row >= Q_BLOCK)

    def head(h, carry):
        acc = jnp.zeros(bucket.shape, F32)
        for b in range(N_BUCKETS):
            acc = jnp.where(bucket == b, rb_ref[b, h], acc)
        o_ref[0, h] = jnp.where(keep, acc, NEG)
        o_ref[1, h] = jnp.where(keep_first, acc, NEG)
        return carry

    lax.fori_loop(0, N_HEADS, head, 0)


def _bias_table(rel_bias):
    bucket, valid = _bucket_table()
    shp = (2 * Q_BLOCK, Q_BLOCK)
    return pl.pallas_call(
        _bias_kernel,
        grid=(1,),
        in_specs=[pl.BlockSpec(memory_space=pltpu.SMEM),
                  pl.BlockSpec(shp, lambda v: (0, 0)),
                  pl.BlockSpec(shp, lambda v: (0, 0))],
        out_specs=pl.BlockSpec((2, N_HEADS) + shp, lambda v: (0, 0, 0, 0)),
        out_shape=jax.ShapeDtypeStruct((2, N_HEADS) + shp, F32),
        compiler_params=_cparams("arbitrary"),
        name="rel_bias_table",
    )(rel_bias, jnp.asarray(bucket), jnp.asarray(valid))


BLOCKS_PER_STEP = 2


def _attn_prompt_kernel(sink_ref, q_ref, kvp_ref, kvc_ref, bias_ref, o_ref):
    scale = HEAD_DIM ** -0.5
    first_variant = jnp.where(pl.program_id(1) == 0, 1, 0)

    def band(blk, col):
        cols = slice(col, col + HEAD_DIM)
        cur = kvc_ref[blk * Q_BLOCK:(blk + 1) * Q_BLOCK, cols]
        if blk == 0:
            return jnp.concatenate([kvp_ref[:, cols], cur], axis=0)
        return kvc_ref[(blk - 1) * Q_BLOCK:(blk + 1) * Q_BLOCK, cols]

    def scores(blk, kv):
        heads = range(kv * GROUP, (kv + 1) * GROUP)
        rows = slice(blk * Q_BLOCK, (blk + 1) * Q_BLOCK)
        variant = first_variant if blk == 0 else 0
        qg = jnp.concatenate(
            [q_ref[rows, h * HEAD_DIM:(h + 1) * HEAD_DIM] for h in heads], axis=0) * scale
        st = lax.dot_general(band(blk, kv * HEAD_DIM), qg, (((1,), (1,)), ((), ())),
                             preferred_element_type=F32)
        return st + jnp.concatenate([bias_ref[variant, h] for h in heads], axis=1)

    def finish(blk, kv, st):
        heads = range(kv * GROUP, (kv + 1) * GROUP)
        vband = band(blk, KV_W + kv * HEAD_DIM)
        sk = jnp.concatenate([jnp.full((1, Q_BLOCK), sink_ref[h], F32) for h in heads], axis=1)
        m = jnp.maximum(jnp.max(st, axis=0, keepdims=True), sk)
        e = jnp.exp(st - m)
        denom = jnp.sum(e, axis=0, keepdims=True) + jnp.exp(sk - m)
        ot = lax.dot_general(vband, e.astype(BF16), (((0,), (0,)), ((), ())),
                             preferred_element_type=F32) * (1.0 / denom)
        o2 = jnp.concatenate(
            [ot[:, g * Q_BLOCK:(g + 1) * Q_BLOCK] for g in range(GROUP)], axis=0)
        rows = slice(blk * Q_BLOCK, (blk + 1) * Q_BLOCK)
        gs = slice(kv * GROUP * HEAD_DIM, (kv + 1) * GROUP * HEAD_DIM)
        o_ref[rows, gs] = o2.T.astype(o_ref.dtype)

    groups = [(blk, kv) for blk in range(BLOCKS_PER_STEP) for kv in range(N_KV_HEADS)]
    st = scores(*groups[0])
    for idx, grp in enumerate(groups):
        st_next = scores(*groups[idx + 1]) if idx + 1 < len(groups) else None
        finish(*grp, st)
        st = st_next


def _attn_prompt(proj_p, bias_tab, sinks):
    rows = BLOCKS_PER_STEP * Q_BLOCK
    steps = SEQ // rows
    qcol = OFF_Q // ATTN_W
    kvcol = OFF_K // (2 * KV_W)

    def prev(b, n):
        return b * (SEQ // Q_BLOCK) + jnp.maximum(n * BLOCKS_PER_STEP - 1, 0)

    return pl.pallas_call(
        _attn_prompt_kernel,
        grid=(BATCH, steps),
        in_specs=[pl.BlockSpec(memory_space=pltpu.SMEM),
                  pl.BlockSpec((rows, ATTN_W), lambda b, n: (b * steps + n, qcol)),
                  pl.BlockSpec((Q_BLOCK, 2 * KV_W), lambda b, n: (prev(b, n), kvcol)),
                  pl.BlockSpec((rows, 2 * KV_W), lambda b, n: (b * steps + n, kvcol)),
                  pl.BlockSpec((2, N_HEADS, 2 * Q_BLOCK, Q_BLOCK), lambda b, n: (0, 0, 0, 0))],
        out_specs=pl.BlockSpec((rows, ATTN_W), lambda b, n: (b * steps + n, 0)),
        out_shape=jax.ShapeDtypeStruct((M_PROMPT, ATTN_W), BF16),
        compiler_params=_cparams("parallel", "arbitrary"),
        name="attn_prompt",
    )(sinks, proj_p, proj_p, proj_p, bias_tab)


SAMPLES_PER_STEP = 8


def _attn_sample_kernel(q_ref, kn_ref, vn_ref, kt_ref, vt_ref, bias_ref, sink_ref,
                        o_ref, kto_ref, vto_ref):
    g = SAMPLES_PER_STEP
    scale = HEAD_DIM ** -0.5
    erow = lax.broadcasted_iota(jnp.int32, (HEAD_DIM, KV_W), 0)
    ecol = lax.broadcasted_iota(jnp.int32, (HEAD_DIM, KV_W), 1)
    expand = (ecol % HEAD_DIM == erow).astype(BF16)
    hrow = lax.broadcasted_iota(jnp.int32, (N_HEADS, KV_W), 0)
    hcol = lax.broadcasted_iota(jnp.int32, (N_HEADS, KV_W), 1)
    own = ((hrow // GROUP) == (hcol // HEAD_DIM))[None]
    bias_w = bias_ref[:, 0:WINDOW][None]
    bias_n = bias_ref[:, WINDOW:WINDOW + 1][None]
    sk = sink_ref[...][None]
    contract_last = (((1,), (1,)), ((), ()))

    q_all = q_ref[...].reshape(g * N_HEADS, HEAD_DIM) * scale
    qrow = jnp.where(own, _dot(q_all, expand).reshape(g, N_HEADS, KV_W), 0.0)
    qrow_b = qrow.astype(BF16)
    kn = kn_ref[...][:, None, :]
    vn = vn_ref[...][:, None, :]
    s_w = jnp.stack([_dot(qrow_b[b], kt_ref[b].astype(BF16)) for b in range(g)])
    s_w = s_w + bias_w
    s_n = jnp.sum(qrow * kn, axis=-1, keepdims=True) + bias_n
    m = jnp.maximum(jnp.maximum(jnp.max(s_w, axis=-1, keepdims=True), s_n), sk)
    e_w = jnp.exp(s_w - m)
    e_n = jnp.exp(s_n - m)
    r = 1.0 / (jnp.sum(e_w, axis=-1, keepdims=True) + e_n + jnp.exp(sk - m))
    p_w = (e_w * r).astype(BF16)
    o_all = jnp.stack([lax.dot_general(p_w[b], vt_ref[b].astype(BF16), contract_last,
                                       preferred_element_type=F32) for b in range(g)])
    o_own = jnp.where(own, o_all + (e_n * r) * vn, 0.0)
    o = o_own[:, :, 0:HEAD_DIM]
    for kv in range(1, N_KV_HEADS):
        o = o + o_own[:, :, kv * HEAD_DIM:(kv + 1) * HEAD_DIM]
    o_ref[...] = o.astype(o_ref.dtype)

    rows = jnp.concatenate([kn_ref[...], vn_ref[...]], axis=0).astype(BF16)
    place = (lax.broadcasted_iota(jnp.int32, (2 * g, 2 * g * WINDOW), 1)
             == WINDOW * lax.broadcasted_iota(jnp.int32, (2 * g, 2 * g * WINDOW), 0)
             ).astype(BF16)
    newcols = lax.dot_general(rows, place, (((0,), (0,)), ((), ())),
                              preferred_element_type=F32)
    first = lax.broadcasted_iota(jnp.int32, (KV_W, WINDOW), 1) == 0
    for b in range(g):
        kcol = newcols[:, b * WINDOW:(b + 1) * WINDOW]
        vcol = newcols[:, (g + b) * WINDOW:(g + b + 1) * WINDOW]
        kto_ref[b] = pltpu.roll(jnp.where(first, kcol, kt_ref[b]), WINDOW - 1, 1)
        vto_ref[b] = pltpu.roll(jnp.where(first, vcol, vt_ref[b]), WINDOW - 1, 1)


def _attn_sample(q3, k_new, v_new, k_buf, v_buf, bias_s, sinks):
    g = SAMPLES_PER_STEP
    nb = DEC_BATCH // g
    win = pl.BlockSpec((g, KV_W, WINDOW), lambda i: (i, 0, 0))
    row = pl.BlockSpec((g, KV_W), lambda i: (i, 0))
    return pl.pallas_call(
        _attn_sample_kernel,
        grid=(nb,),
        in_specs=[pl.BlockSpec((g, N_HEADS, HEAD_DIM), lambda i: (i, 0, 0)),
                  row, row, win, win,
                  pl.BlockSpec((N_HEADS, WINDOW + 1), lambda i: (0, 0)),
                  pl.BlockSpec((N_HEADS, 1), lambda i: (0, 0))],
        out_specs=[pl.BlockSpec((g, N_HEADS, HEAD_DIM), lambda i: (i, 0, 0)), win, win],
        out_shape=[jax.ShapeDtypeStruct((DEC_BATCH, N_HEADS, HEAD_DIM), BF16),
                   jax.ShapeDtypeStruct((DEC_BATCH, KV_W, WINDOW), F32),
                   jax.ShapeDtypeStruct((DEC_BATCH, KV_W, WINDOW), F32)],
        compiler_params=_cparams("parallel"),
        name="attn_sample",
    )(q3, k_new, v_new, k_buf, v_buf, bias_s, sinks.reshape(N_HEADS, 1))


def _merge_kernel(ap_ref, bp_ref, as_ref, bs_ref, waf_ref, wbf_ref,
                  gap_ref, gbp_ref, gas_ref, gbs_ref, op_ref, os_ref,
                  wa0_ref, wb0_ref, wa1_ref, wb1_ref):
    def compute(slot):
        wa_ref, wb_ref = slot

        def merged(a_ref, b_ref, ga_ref, gb_ref):
            return (_sigmoid(ga_ref[...].astype(F32)) * _dot(a_ref[...], wa_ref[...])
                    + _sigmoid(gb_ref[...].astype(F32)) * _dot(b_ref[...], wb_ref[...]))

        op_ref[...] = merged(ap_ref, bp_ref, gap_ref, gbp_ref).astype(op_ref.dtype)

        @pl.when(_is_last_row_tile())
        def _():
            os_ref[...] = merged(as_ref, bs_ref, gas_ref, gbs_ref).astype(os_ref.dtype)

    _ws_phases([waf_ref, wbf_ref], [wa0_ref, wb0_ref], [wa1_ref, wb1_ref], compute)


def _branch_merge(a_p, o_p, a_s, o_s, wa, wb, proj_p, proj_s, bm, bn):
    mp, k = a_p.shape
    ms = a_s.shape[0]
    n = wa.shape[1]
    grid = _ws_grid(n, bn, mp, bm)
    ga0 = OFF_GA // bn
    gb0 = OFF_GB // bn
    wchunk = pl.BlockSpec((k // grid[1], bn), _ws_wchunk(grid[0] - 1))
    wslot = pltpu.VMEM((k, bn), BF16)
    return pl.pallas_call(
        _merge_kernel,
        grid=grid,
        in_specs=[pl.BlockSpec((bm, k), _ws_rows),
                  pl.BlockSpec((bm, k), _ws_rows),
                  pl.BlockSpec((ms, k), lambda j, i: (0, 0)),
                  pl.BlockSpec((ms, k), lambda j, i: (0, 0)),
                  wchunk, wchunk,
                  pl.BlockSpec((bm, bn), _ws_tile(ga0)),
                  pl.BlockSpec((bm, bn), _ws_tile(gb0)),
                  pl.BlockSpec((ms, bn), _ws_stile(ga0)),
                  pl.BlockSpec((ms, bn), _ws_stile(gb0))],
        out_specs=[pl.BlockSpec((bm, bn), _ws_tile()),
                   pl.BlockSpec((ms, bn), _ws_stile())],
        out_shape=[jax.ShapeDtypeStruct((mp, n), BF16),
                   jax.ShapeDtypeStruct((ms, n), BF16)],
        scratch_shapes=[wslot, wslot, wslot, wslot],
        compiler_params=_cparams("arbitrary", "arbitrary"),
        name="branch_merge",
    )(a_p, o_p, a_s, o_s, wa, wb, proj_p, proj_p, proj_s, proj_s)


def _resid_kernel(xp_ref, xs_ref, wf_ref, rp_ref, rs_ref, op_ref, os_ref, w0_ref, w1_ref):
    def compute(slot):
        (w_ref,) = slot
        op_ref[...] = rp_ref[...] + _dot(xp_ref[...], w_ref[...])

        @pl.when(_is_last_row_tile())
        def _():
            os_ref[...] = rs_ref[...] + _dot(xs_ref[...], w_ref[...])

    _ws_phases([wf_ref], [w0_ref], [w1_ref], compute)


def _resid_matmul(xp, xs, w, rp, rs, bm, bn, name):
    mp, k = xp.shape
    ms = xs.shape[0]
    n = w.shape[1]
    grid = _ws_grid(n, bn, mp, bm)
    return pl.pallas_call(
        _resid_kernel,
        grid=grid,
        in_specs=[pl.BlockSpec((bm, k), _ws_rows),
                  pl.BlockSpec((ms, k), lambda j, i: (0, 0)),
                  pl.BlockSpec((k // grid[1], bn), _ws_wchunk(grid[0] - 1)),
                  pl.BlockSpec((bm, bn), _ws_tile()),
                  pl.BlockSpec((ms, bn), _ws_stile())],
        out_specs=[pl.BlockSpec((bm, bn), _ws_tile()),
                   pl.BlockSpec((ms, bn), _ws_stile())],
        out_shape=[jax.ShapeDtypeStruct((mp, n), F32),
                   jax.ShapeDtypeStruct((ms, n), F32)],
        scratch_shapes=[pltpu.VMEM((k, bn), BF16), pltpu.VMEM((k, bn), BF16)],
        compiler_params=_cparams("arbitrary", "arbitrary"),
        name=name,
    )(xp, xs, w, rp, rs)


LANES = 128


def _lane_partials(v):
    part = v[:, 0:LANES]
    for c in range(1, v.shape[1] // LANES):
        part = part + v[:, c * LANES:(c + 1) * LANES]
    return part


def _outproj_kernel(xp_ref, xs_ref, wf_ref, rp_ref, rs_ref, g_ref,
                    op_ref, os_ref, bp_ref, bs_ref, ssp_ref, sss_ref,
                    w0_ref, w1_ref, accp_ref, accs_ref):
    j = pl.program_id(0)
    i = pl.program_id(1)
    bm = xp_ref.shape[0]

    def compute(slot):
        (w_ref,) = slot

        def tile(x_ref, r_ref, o_ref, b_ref, acc_view, ss_ref):
            x1 = r_ref[...] + _dot(x_ref[...], w_ref[...])
            o_ref[...] = x1
            b_ref[...] = (x1 * g_ref[...]).astype(b_ref.dtype)
            total = jnp.where(j == 1, 0.0, acc_view[...]) + _lane_partials(x1 * x1)
            acc_view[...] = total
            ss_ref[...] = total

        rows = pl.ds(pl.multiple_of(i * bm, bm), bm)
        tile(xp_ref, rp_ref, op_ref, bp_ref, accp_ref.at[rows, :], ssp_ref)

        @pl.when(_is_last_row_tile())
        def _():
            tile(xs_ref, rs_ref, os_ref, bs_ref, accs_ref, sss_ref)

    _ws_phases([wf_ref], [w0_ref], [w1_ref], compute)


def _out_proj(xp, xs, w, rp, rs, gain, bm, bn):
    mp, k = xp.shape
    ms = xs.shape[0]
    n = w.shape[1]
    grid = _ws_grid(n, bn, mp, bm)
    last = grid[0] - 1
    ssp_map = lambda j, i: (jnp.where(j == last, i, 0), 0)
    return pl.pallas_call(
        _outproj_kernel,
        grid=grid,
        in_specs=[pl.BlockSpec((bm, k), _ws_rows),
                  pl.BlockSpec((ms, k), lambda j, i: (0, 0)),
                  pl.BlockSpec((k // grid[1], bn), _ws_wchunk(grid[0] - 1)),
                  pl.BlockSpec((bm, bn), _ws_tile()),
                  pl.BlockSpec((ms, bn), _ws_stile()),
                  pl.BlockSpec((1, bn), _ws_stile())],
        out_specs=[pl.BlockSpec((bm, bn), _ws_tile()),
                   pl.BlockSpec((ms, bn), _ws_stile()),
                   pl.BlockSpec((bm, bn), _ws_tile()),
                   pl.BlockSpec((ms, bn), _ws_stile()),
                   pl.BlockSpec((bm, LANES), ssp_map),
                   pl.BlockSpec((ms, LANES), lambda j, i: (0, 0))],
        out_shape=[jax.ShapeDtypeStruct((mp, n), F32),
                   jax.ShapeDtypeStruct((ms, n), F32),
                   jax.ShapeDtypeStruct((mp, n), BF16),
                   jax.ShapeDtypeStruct((ms, n), BF16),
                   jax.ShapeDtypeStruct((mp, LANES), F32),
                   jax.ShapeDtypeStruct((ms, LANES), F32)],
        scratch_shapes=[pltpu.VMEM((k, bn), BF16), pltpu.VMEM((k, bn), BF16),
                        pltpu.VMEM((mp, LANES), F32), pltpu.VMEM((ms, LANES), F32)],
        compiler_params=_cparams("arbitrary", "arbitrary"),
        name="out_proj",
    )(xp, xs, w, rp, rs, gain.reshape(1, n))


def _inv_rms(ss_ref):
    return lax.rsqrt(jnp.sum(ss_ref[...], axis=-1, keepdims=True) * (1.0 / D_MODEL) + EPS)


def _ffn_up_kernel(hp_ref, hs_ref, ssp_ref, sss_ref, wgf_ref, wuf_ref, cw_ref, cb_ref,
                   s0_ref, s1_ref, fp_ref, fs_ref, gs_ref, st_ref,
                   wg0_ref, wu0_ref, wg1_ref, wu1_ref, carry_ref, gbuf_ref,
                   *, tiles_per_seq, last_phase, last_width):
    i = pl.program_id(1)
    bm = hp_ref.shape[0]

    def compute(slot, width):
        wg_ref, wu_ref = slot
        cols = slice(0, width)
        w0 = cw_ref[0:1, cols]
        w1 = cw_ref[1:2, cols]
        w2 = cw_ref[2:3, cols]
        bias = cb_ref[:, cols]

        @pl.when(i % tiles_per_seq == 0)
        def _():
            carry_ref[...] = jnp.zeros(carry_ref.shape, F32)

        r = _inv_rms(ssp_ref)
        g = r * _dot(hp_ref[...], wg_ref[:, cols])
        gbuf_ref[0:SUBLANES, cols] = carry_ref[:, cols]
        gbuf_ref[SUBLANES:SUBLANES + bm, cols] = g
        g1 = gbuf_ref[SUBLANES - 1:SUBLANES - 1 + bm, cols]
        g2 = gbuf_ref[SUBLANES - 2:SUBLANES - 2 + bm, cols]
        gc = w0 * g2 + w1 * g1 + w2 * g + bias
        up = r * _dot(hp_ref[...], wu_ref[:, cols])
        fp_ref[:, cols] = (gc * _sigmoid(gc) * up).astype(fp_ref.dtype)
        tail = gbuf_ref[bm:SUBLANES + bm, cols]
        carry_ref[:, cols] = tail
        st_ref[0, :, cols] = tail

        @pl.when(_is_last_row_tile())
        def _():
            rs = _inv_rms(sss_ref)
            gsm = rs * _dot(hs_ref[...], wg_ref[:, cols])
            ups = rs * _dot(hs_ref[...], wu_ref[:, cols])
            gcs = w0 * s0_ref[:, cols] + w1 * s1_ref[:, cols] + w2 * gsm + bias
            fs_ref[:, cols] = (gcs * _sigmoid(gcs) * ups).astype(fs_ref.dtype)
            gs_ref[:, cols] = gsm

    full_width = fp_ref.shape[1]
    _ws_phases([wgf_ref, wuf_ref], [wg0_ref, wu0_ref], [wg1_ref, wu1_ref],
               functools.partial(compute, width=full_width),
               functools.partial(compute, width=last_width) if last_width < full_width else None,
               last_phase)


def _ffn_up(hp, hs, ssp, sss, wg, wu, cw, cb, s0, s1, bm, bn):
    mp, k = hp.shape
    ms = hs.shape[0]
    n = wg.shape[1]
    grid = _ws_grid(n, bn, mp, bm)
    tiles_per_seq = SEQ // bm
    wchunk = pl.BlockSpec((k // grid[1], bn), _ws_wchunk(grid[0] - 1))
    wslot = pltpu.VMEM((k, bn), BF16)
    col = _ws_stile()

    def tail_map(j, i):
        return (jnp.where(j > 0, i, 0) // tiles_per_seq, 0, jnp.maximum(j - 1, 0))

    last_phase = grid[0] - 1
    last_width = n - (last_phase - 1) * bn
    return pl.pallas_call(
        functools.partial(_ffn_up_kernel, tiles_per_seq=tiles_per_seq,
                          last_phase=last_phase, last_width=last_width),
        grid=grid,
        in_specs=[pl.BlockSpec((bm, k), _ws_rows),
                  pl.BlockSpec((ms, k), lambda j, i: (0, 0)),
                  pl.BlockSpec((bm, LANES), _ws_rows),
                  pl.BlockSpec((ms, LANES), lambda j, i: (0, 0)),
                  wchunk, wchunk,
                  pl.BlockSpec((FFN_CONV_K, bn), col),
                  pl.BlockSpec((1, bn), col),
                  pl.BlockSpec((ms, bn), col),
                  pl.BlockSpec((ms, bn), col)],
        out_specs=[pl.BlockSpec((bm, bn), _ws_tile()),
                   pl.BlockSpec((ms, bn), col),
                   pl.BlockSpec((ms, bn), col),
                   pl.BlockSpec((1, SUBLANES, bn), tail_map)],
        out_shape=[jax.ShapeDtypeStruct((mp, n), BF16),
                   jax.ShapeDtypeStruct((ms, n), BF16),
                   jax.ShapeDtypeStruct((ms, n), F32),
                   jax.ShapeDtypeStruct((BATCH, SUBLANES, n), F32)],
        scratch_shapes=[wslot, wslot, wslot, wslot,
                        pltpu.VMEM((SUBLANES, bn), F32),
                        pltpu.VMEM((SUBLANES + bm, bn), F32)],
        compiler_params=_cparams("arbitrary", "arbitrary"),
        name="ffn_up",
    )(hp, hs, ssp, sss, wg, wu, cw, cb, s0, s1)


def _window_to_kernel(state):
    return jnp.transpose(state[0], (0, 2, 3, 1)).reshape(DEC_BATCH, KV_W, WINDOW)


def _window_from_kernel(win):
    win = win.reshape(DEC_BATCH, N_KV_HEADS, HEAD_DIM, WINDOW)
    return jnp.transpose(win, (0, 3, 1, 2))[None]


def kernel(x_prompt, x_sample, state_k_window, state_v_window, state_conv, state_ffn_conv,
           attn_norm_g, w_in, conv_w, w_branch_a, w_branch_b, sinks, w_out, ffn_norm_g,
           w_ffn_gate, w_ffn_up, ffn_conv_w, ffn_conv_b, w_ffn_down, rel_bias,
           final_norm_g):
    xp = x_prompt.reshape(M_PROMPT, D_MODEL)
    xs = x_sample.reshape(DEC_BATCH, D_MODEL)

    fs0 = state_ffn_conv[0][:, 0, :]
    fs1 = state_ffn_conv[0][:, 1, :]

    cs0 = state_conv[0][:, 0, :]
    cs1 = state_conv[0][:, 1, :]
    hs = _rmsnorm(xs, attn_norm_g[0], BF16, DEC_BATCH)
    a_p, a_s, u_s, u_tail, hp, hssp = _mixer_a(
        xp, hs, attn_norm_g[0], w_in[0], conv_w[0], cs0, cs1, 512, 512)
    proj_p, proj_s = _in_proj(hp, hs, hssp, w_in[0], M_PROMPT, W_REST, PROJ_W, 1024, 1024)

    bias_tab = _bias_table(rel_bias)
    o_p = _attn_prompt(proj_p, bias_tab, sinks[0])
    q3 = proj_s[:, OFF_Q:OFF_Q + ATTN_W].reshape(DEC_BATCH, N_HEADS, HEAD_DIM)
    k_new = proj_s[:, OFF_K:OFF_K + KV_W].astype(F32)
    v_new = proj_s[:, OFF_V:OFF_V + KV_W].astype(F32)
    bias_s = bias_tab[0, :, Q_BLOCK - 1:, Q_BLOCK - 1]
    o_s3, k_win_s, v_win_s = _attn_sample(
        q3, k_new, v_new,
        _window_to_kernel(state_k_window), _window_to_kernel(state_v_window),
        bias_s, sinks[0])
    o_s = o_s3.reshape(DEC_BATCH, ATTN_W)

    mg_p, mg_s = _branch_merge(a_p, o_p, a_s, o_s, w_branch_a[0], w_branch_b[0],
                               proj_p, proj_s, 1024, 1024)
    x1p, x1s, xbp, xbs, ssp, sss = _out_proj(mg_p, mg_s, w_out[0], xp, xs, ffn_norm_g[0],
                                             512, 1024)

    f_p, f_s, g_s, g_tail = _ffn_up(xbp, xbs, ssp, sss, w_ffn_gate[0], w_ffn_up[0],
                                    ffn_conv_w[0], ffn_conv_b, fs0, fs1, 1024, 512)
    x2p, x2s = _resid_matmul(f_p, f_s, w_ffn_down[0], x1p, x1s, 512, 512, "ffn_down")
    y_p = _rmsnorm(x2p, final_norm_g, F32, 256)
    y_s = _rmsnorm(x2s, final_norm_g, F32, DEC_BATCH)

    kv_p = proj_p.reshape(BATCH, SEQ, PROJ_W)[:, SEQ - WINDOW:, OFF_K:OFF_GA].astype(F32)
    k_win_p = kv_p[:, :, :KV_W].reshape(1, BATCH, WINDOW, N_KV_HEADS, HEAD_DIM)
    v_win_p = kv_p[:, :, KV_W:].reshape(1, BATCH, WINDOW, N_KV_HEADS, HEAD_DIM)
    conv_p = u_tail[:, SUBLANES - (CONV_K - 1):, :]
    ffn_p = g_tail[:, SUBLANES - (FFN_CONV_K - 1):, :]
    conv_s = jnp.stack([cs1, u_s], axis=1)
    ffn_s = jnp.stack([fs1, g_s], axis=1)

    return (y_p.reshape(BATCH, SEQ, D_MODEL),
            y_s.reshape(DEC_BATCH, 1, D_MODEL),
            k_win_p, v_win_p, conv_p[None], ffn_p[None],
            _window_from_kernel(k_win_s), _window_from_kernel(v_win_s),
            conv_s[None], ffn_s[None])
```

```python
import functools
import math

import numpy as np
import jax
import jax.numpy as jnp
from jax import lax
from jax.experimental import pallas as pl
from jax.experimental.pallas import tpu as pltpu

F32 = jnp.float32
BF16 = jnp.bfloat16

D_MODEL = 4096
BATCH = 4
SEQ = 2048
DEC_BATCH = 128
N_HEADS = 32
N_KV_HEADS = 8
HEAD_DIM = 64
GROUP = N_HEADS // N_KV_HEADS
ATTN_W = N_HEADS * HEAD_DIM
KV_W = N_KV_HEADS * HEAD_DIM
CONV_W = D_MODEL // 2
CONV_K = 3
WINDOW = 128
Q_BLOCK = 128
N_BUCKETS = 32
MAX_DISTANCE = 128
D_FF = 11008
FFN_CONV_K = 3
EPS = 1e-5
NEG = -1e30
IN_W = 3 * CONV_W + ATTN_W + 2 * KV_W + 2 * D_MODEL
M_PROMPT = BATCH * SEQ

W_CB = 0
W_CC = CONV_W
W_CH = 2 * CONV_W
W_REST = 3 * CONV_W
OFF_Q = 0
OFF_K = OFF_Q + ATTN_W
OFF_V = OFF_K + KV_W
OFF_GA = OFF_V + KV_W
OFF_GB = OFF_GA + D_MODEL
PROJ_W = OFF_GB + D_MODEL

V7X_VMEM_BYTES = 64 * 1024 * 1024
VMEM_LIMIT = V7X_VMEM_BYTES - 1024 * 1024
SUBLANES = 8


def _cparams(*sem):
    return pltpu.CompilerParams(dimension_semantics=sem, vmem_limit_bytes=VMEM_LIMIT)


def _dot(a, b):
    return jnp.dot(a, b, preferred_element_type=F32)


def _sigmoid(x):
    return 1.0 / (1.0 + jnp.exp(-x))


def _ws_grid(n, bn, mp, bm):
    return (pl.cdiv(n, bn) + 1, mp // bm)


def _ws_rows(j, i):
    return (jnp.where(j > 0, i, 0), 0)


def _ws_wchunk(nj, col0=0):
    return lambda j, i: (i, col0 + jnp.minimum(j, nj - 1))


def _ws_tile(col0=0):
    return lambda j, i: (jnp.where(j > 0, i, 0), col0 + jnp.maximum(j - 1, 0))


def _ws_stile(col0=0):
    return lambda j, i: (0, col0 + jnp.maximum(j - 1, 0))


def _ws_cast(wf_refs, dst_refs):
    i = pl.program_id(1)
    for wf_ref, dst_ref in zip(wf_refs, dst_refs):
        rows = wf_ref.shape[0]
        r0 = pl.multiple_of(i * rows, rows)
        dst_ref[pl.ds(r0, rows), :] = wf_ref[...].astype(BF16)


def _ws_phases(wf_refs, slot0, slot1, compute, compute_last=None, last=None):
    j = pl.program_id(0)
    regular = (j > 0) if compute_last is None else ((j > 0) & (j < last))

    @pl.when(j == 0)
    def _():
        _ws_cast(wf_refs, slot0)

    @pl.when(regular & (j % 2 == 1))
    def _():
        _ws_cast(wf_refs, slot1)
        compute(slot0)

    @pl.when(regular & (j % 2 == 0))
    def _():
        _ws_cast(wf_refs, slot0)
        compute(slot1)

    if compute_last is not None:
        @pl.when(j == last)
        def _():
            compute_last(slot0 if last % 2 == 1 else slot1)


def _is_last_row_tile():
    return pl.program_id(1) == pl.num_programs(1) - 1


def _rms_kernel(x_ref, g_ref, o_ref):
    x = x_ref[...]
    r = lax.rsqrt(jnp.mean(x * x, axis=-1, keepdims=True) + EPS)
    o_ref[...] = ((x * r) * g_ref[...]).astype(o_ref.dtype)


def _rmsnorm(x, g, out_dtype, block_rows):
    rows, d = x.shape
    return pl.pallas_call(
        _rms_kernel,
        grid=(rows // block_rows,),
        in_specs=[pl.BlockSpec((block_rows, d), lambda i: (i, 0)),
                  pl.BlockSpec((1, d), lambda i: (0, 0))],
        out_specs=pl.BlockSpec((block_rows, d), lambda i: (i, 0)),
        out_shape=jax.ShapeDtypeStruct((rows, d), out_dtype),
        compiler_params=_cparams("parallel"),
        name="rmsnorm",
    )(x, g.reshape(1, d))


def _proj_kernel(xp_ref, xs_ref, ssp_ref, wf_ref, op_ref, os_ref, w0_ref, w1_ref):
    def compute(slot):
        (w_ref,) = slot
        op_ref[...] = (_inv_rms(ssp_ref) * _dot(xp_ref[...], w_ref[...])).astype(op_ref.dtype)

        @pl.when(_is_last_row_tile())
        def _():
            os_ref[...] = _dot(xs_ref[...], w_ref[...]).astype(os_ref.dtype)

    _ws_phases([wf_ref], [w0_ref], [w1_ref], compute)


def _in_proj(hp, hs, ssp, w, mp, col0, n, bm, bn):
    k = hp.shape[1]
    ms = hs.shape[0]
    grid = _ws_grid(n, bn, mp, bm)
    return pl.pallas_call(
        _proj_kernel,
        grid=grid,
        in_specs=[pl.BlockSpec((bm, k), _ws_rows),
                  pl.BlockSpec((ms, k), lambda j, i: (0, 0)),
                  pl.BlockSpec((bm, LANES), _ws_rows),
                  pl.BlockSpec((k // grid[1], bn), _ws_wchunk(grid[0] - 1, col0 // bn))],
        out_specs=[pl.BlockSpec((bm, bn), _ws_tile()),
                   pl.BlockSpec((ms, bn), _ws_stile())],
        out_shape=[jax.ShapeDtypeStruct((mp, n), BF16),
                   jax.ShapeDtypeStruct((ms, n), BF16)],
        scratch_shapes=[pltpu.VMEM((k, bn), BF16), pltpu.VMEM((k, bn), BF16)],
        compiler_params=_cparams("arbitrary", "arbitrary"),
        name="in_proj",
    )(hp, hs, ssp, w)


def _mixa_kernel(xp_ref, hs_ref, g_ref, wbf_ref, wcf_ref, whf_ref, cw_ref, s0_ref, s1_ref,
                 ap_ref, as_ref, us_ref, st_ref, hp_ref, ssp_ref,
                 wb0_ref, wc0_ref, wh0_ref, wb1_ref, wc1_ref, wh1_ref, carry_ref, ubuf_ref,
                 *, tiles_per_seq):
    i = pl.program_id(1)
    bm = xp_ref.shape[0]

    def compute(slot):
        wb_ref, wc_ref, wh_ref = slot
        w0 = cw_ref[0:1, :]
        w1 = cw_ref[1:2, :]
        w2 = cw_ref[2:3, :]

        @pl.when(i % tiles_per_seq == 0)
        def _():
            carry_ref[...] = jnp.zeros(carry_ref.shape, F32)

        hp_ref[...] = (xp_ref[...] * g_ref[...]).astype(hp_ref.dtype)
        ssp_ref[...] = _lane_partials(xp_ref[...] * xp_ref[...])
        r = _inv_rms(ssp_ref)
        u = (r * _dot(hp_ref[...], wc_ref[...])) * (r * _dot(hp_ref[...], wh_ref[...]))
        ubuf_ref[0:SUBLANES, :] = carry_ref[...]
        ubuf_ref[SUBLANES:SUBLANES + bm, :] = u
        u1 = ubuf_ref[SUBLANES - 1:SUBLANES - 1 + bm, :]
        u2 = ubuf_ref[SUBLANES - 2:SUBLANES - 2 + bm, :]
        z = w0 * u2 + w1 * u1 + w2 * u
        ap_ref[...] = ((r * _dot(hp_ref[...], wb_ref[...])) * z).astype(ap_ref.dtype)
        tail = ubuf_ref[bm:SUBLANES + bm, :]
        carry_ref[...] = tail
        st_ref[0] = tail

        @pl.when(_is_last_row_tile())
        def _():
            us = _dot(hs_ref[...], wc_ref[...]) * _dot(hs_ref[...], wh_ref[...])
            zs = w0 * s0_ref[...] + w1 * s1_ref[...] + w2 * us
            as_ref[...] = (_dot(hs_ref[...], wb_ref[...]) * zs).astype(as_ref.dtype)
            us_ref[...] = us

    _ws_phases([wbf_ref, wcf_ref, whf_ref], [wb0_ref, wc0_ref, wh0_ref],
               [wb1_ref, wc1_ref, wh1_ref], compute)


def _mixer_a(xp, hs, gain, w_in, conv_w, s0, s1, bm, bn):
    mp, k = xp.shape
    ms = hs.shape[0]
    n = CONV_W
    grid = _ws_grid(n, bn, mp, bm)
    nj = grid[0] - 1
    tiles_per_seq = SEQ // bm
    wslot = pltpu.VMEM((k, bn), BF16)
    col = _ws_stile()

    def wchunk(col0):
        return pl.BlockSpec((k // grid[1], bn), _ws_wchunk(nj, col0 // bn))

    def tail_map(j, i):
        return (jnp.where(j > 0, i, 0) // tiles_per_seq, 0, jnp.maximum(j - 1, 0))

    spare = grid[1]

    def operand_map(j, i):
        return (jnp.where(j == 1, i, jnp.where(j == 0, 0, spare)), 0)

    return pl.pallas_call(
        functools.partial(_mixa_kernel, tiles_per_seq=tiles_per_seq),
        grid=grid,
        in_specs=[pl.BlockSpec((bm, k), _ws_rows),
                  pl.BlockSpec((ms, k), lambda j, i: (0, 0)),
                  pl.BlockSpec((1, k), lambda j, i: (0, 0)),
                  wchunk(W_CB), wchunk(W_CC), wchunk(W_CH),
                  pl.BlockSpec((CONV_K, bn), col),
                  pl.BlockSpec((ms, bn), col),
                  pl.BlockSpec((ms, bn), col)],
        out_specs=[pl.BlockSpec((bm, bn), _ws_tile()),
                   pl.BlockSpec((ms, bn), col),
                   pl.BlockSpec((ms, bn), col),
                   pl.BlockSpec((1, SUBLANES, bn), tail_map),
                   pl.BlockSpec((bm, k), operand_map),
                   pl.BlockSpec((bm, LANES), operand_map)],
        out_shape=[jax.ShapeDtypeStruct((mp, n), BF16),
                   jax.ShapeDtypeStruct((ms, n), BF16),
                   jax.ShapeDtypeStruct((ms, n), F32),
                   jax.ShapeDtypeStruct((BATCH, SUBLANES, n), F32),
                   jax.ShapeDtypeStruct((mp + bm, k), BF16),
                   jax.ShapeDtypeStruct((mp + bm, LANES), F32)],
        scratch_shapes=[wslot] * 6 + [pltpu.VMEM((SUBLANES, bn), F32),
                                      pltpu.VMEM((SUBLANES + bm, bn), F32)],
        compiler_params=_cparams("arbitrary", "arbitrary"),
        name="mixer_a",
    )(xp, hs, gain.reshape(1, k), w_in, w_in, w_in, conv_w, s0, s1)


def _bucket_table():
    qi = np.arange(Q_BLOCK)[None, :]
    kj = np.arange(2 * Q_BLOCK)[:, None]
    dist = qi + Q_BLOCK - kj
    max_exact = N_BUCKETS // 2
    d = np.maximum(dist, 0)
    df = np.maximum(d, 1).astype(np.float32)
    large = max_exact + (np.log(df / np.float32(max_exact))
                         / np.float32(math.log(MAX_DISTANCE / max_exact))
                         * np.float32(N_BUCKETS - max_exact)).astype(np.int32)
    large = np.minimum(large, N_BUCKETS - 1)
    bucket = np.where(d < max_exact, d, large).astype(np.int32)
    valid = ((dist >= 0) & (dist <= WINDOW)).astype(np.int32)
    return bucket, valid


def _bias_kernel(rb_ref, bucket_ref, valid_ref, o_ref):
    bucket = bucket_ref[...]
    row = lax.broadcasted_iota(jnp.int32, bucket.shape, 0)
    keep = valid_ref[...] != 0
    keep_first = keep & (row >= Q_BLOCK)

    def head(h, carry):
        acc = jnp.zeros(bucket.shape, F32)
        for b in range(N_BUCKETS):
            acc = jnp.where(bucket == b, rb_ref[b, h], acc)
        o_ref[0, h] = jnp.where(keep, acc, NEG)
        o_ref[1, h] = jnp.where(keep_first, acc, NEG)
        return carry

    lax.fori_loop(0, N_HEADS, head, 0)


def _bias_table(rel_bias):
    bucket, valid = _bucket_table()
    shp = (2 * Q_BLOCK, Q_BLOCK)
    return pl.pallas_call(
        _bias_kernel,
        grid=(1,),
        in_specs=[pl.BlockSpec(memory_space=pltpu.SMEM),
                  pl.BlockSpec(shp, lambda v: (0, 0)),
                  pl.BlockSpec(shp, lambda v: (0, 0))],
        out_specs=pl.BlockSpec((2, N_HEADS) + shp, lambda v: (0, 0, 0, 0)),
        out_shape=jax.ShapeDtypeStruct((2, N_HEADS) + shp, F32),
        compiler_params=_cparams("arbitrary"),
        name="rel_bias_table",
    )(rel_bias, jnp.asarray(bucket), jnp.asarray(valid))


BLOCKS_PER_STEP = 4


def _attn_prompt_kernel(sink_ref, q_ref, kvp_ref, kvc_ref, bias_ref, o_ref):
    scale = HEAD_DIM ** -0.5
    first_variant = jnp.where(pl.program_id(1) == 0, 1, 0)

    def band(blk, col):
        cols = slice(col, col + HEAD_DIM)
        cur = kvc_ref[blk * Q_BLOCK:(blk + 1) * Q_BLOCK, cols]
        if blk == 0:
            return jnp.concatenate([kvp_ref[:, cols], cur], axis=0)
        return kvc_ref[(blk - 1) * Q_BLOCK:(blk + 1) * Q_BLOCK, cols]

    def scores(blk, kv):
        heads = range(kv * GROUP, (kv + 1) * GROUP)
        rows = slice(blk * Q_BLOCK, (blk + 1) * Q_BLOCK)
        variant = first_variant if blk == 0 else 0
        qg = jnp.concatenate(
            [q_ref[rows, h * HEAD_DIM:(h + 1) * HEAD_DIM] for h in heads], axis=0) * scale
        st = lax.dot_general(band(blk, kv * HEAD_DIM), qg, (((1,), (1,)), ((), ())),
                             preferred_element_type=F32)
        return st + jnp.concatenate([bias_ref[variant, h] for h in heads], axis=1)

    def finish(blk, kv, st):
        heads = range(kv * GROUP, (kv + 1) * GROUP)
        vband = band(blk, KV_W + kv * HEAD_DIM)
        sk = jnp.concatenate([jnp.full((1, Q_BLOCK), sink_ref[h], F32) for h in heads], axis=1)
        m = jnp.maximum(jnp.max(st, axis=0, keepdims=True), sk)
        e = jnp.exp(st - m)
        denom = jnp.sum(e, axis=0, keepdims=True) + jnp.exp(sk - m)
        ot = lax.dot_general(vband, e.astype(BF16), (((0,), (0,)), ((), ())),
                             preferred_element_type=F32) * (1.0 / denom)
        o2 = jnp.concatenate(
            [ot[:, g * Q_BLOCK:(g + 1) * Q_BLOCK] for g in range(GROUP)], axis=0)
        rows = slice(blk * Q_BLOCK, (blk + 1) * Q_BLOCK)
        gs = slice(kv * GROUP * HEAD_DIM, (kv + 1) * GROUP * HEAD_DIM)
        o_ref[rows, gs] = o2.T.astype(o_ref.dtype)

    groups = [(blk, kv) for blk in range(BLOCKS_PER_STEP) for kv in range(N_KV_HEADS)]
    st = scores(*groups[0])
    for idx, grp in enumerate(groups):
        st_next = scores(*groups[idx + 1]) if idx + 1 < len(groups) else None
        finish(*grp, st)
        st = st_next


def _attn_prompt(proj_p, bias_tab, sinks):
    rows = BLOCKS_PER_STEP * Q_BLOCK
    steps = SEQ // rows
    qcol = OFF_Q // ATTN_W
    kvcol = OFF_K // (2 * KV_W)

    def prev(b, n):
        return b * (SEQ // Q_BLOCK) + jnp.maximum(n * BLOCKS_PER_STEP - 1, 0)

    return pl.pallas_call(
        _attn_prompt_kernel,
        grid=(BATCH, steps),
        in_specs=[pl.BlockSpec(memory_space=pltpu.SMEM),
                  pl.BlockSpec((rows, ATTN_W), lambda b, n: (b * steps + n, qcol)),
                  pl.BlockSpec((Q_BLOCK, 2 * KV_W), lambda b, n: (prev(b, n), kvcol)),
                  pl.BlockSpec((rows, 2 * KV_W), lambda b, n: (b * steps + n, kvcol)),
                  pl.BlockSpec((2, N_HEADS, 2 * Q_BLOCK, Q_BLOCK), lambda b, n: (0, 0, 0, 0))],
        out_specs=pl.BlockSpec((rows, ATTN_W), lambda b, n: (b * steps + n, 0)),
        out_shape=jax.ShapeDtypeStruct((M_PROMPT, ATTN_W), BF16),
        compiler_params=_cparams("parallel", "arbitrary"),
        name="attn_prompt",
    )(sinks, proj_p, proj_p, proj_p, bias_tab)


SAMPLES_PER_STEP = 16


def _attn_sample_kernel(q_ref, kn_ref, vn_ref, kt_ref, vt_ref, bias_ref, sink_ref,
                        o_ref, kto_ref, vto_ref):
    g = SAMPLES_PER_STEP
    scale = HEAD_DIM ** -0.5
    erow = lax.broadcasted_iota(jnp.int32, (HEAD_DIM, KV_W), 0)
    ecol = lax.broadcasted_iota(jnp.int32, (HEAD_DIM, KV_W), 1)
    expand = (ecol % HEAD_DIM == erow).astype(BF16)
    hrow = lax.broadcasted_iota(jnp.int32, (N_HEADS, KV_W), 0)
    hcol = lax.broadcasted_iota(jnp.int32, (N_HEADS, KV_W), 1)
    own = ((hrow // GROUP) == (hcol // HEAD_DIM))[None]
    bias_w = bias_ref[:, 0:WINDOW][None]
    bias_n = bias_ref[:, WINDOW:WINDOW + 1][None]
    sk = sink_ref[...][None]
    contract_last = (((1,), (1,)), ((), ()))

    q_all = q_ref[...].reshape(g * N_HEADS, HEAD_DIM) * scale
    qrow = jnp.where(own, _dot(q_all, expand).reshape(g, N_HEADS, KV_W), 0.0)
    qrow_b = qrow.astype(BF16)
    kn = kn_ref[...][:, None, :]
    vn = vn_ref[...][:, None, :]
    s_w = jnp.stack([_dot(qrow_b[b], kt_ref[b].astype(BF16)) for b in range(g)])
    s_w = s_w + bias_w
    s_n = jnp.sum(qrow * kn, axis=-1, keepdims=True) + bias_n
    m = jnp.maximum(jnp.maximum(jnp.max(s_w, axis=-1, keepdims=True), s_n), sk)
    e_w = jnp.exp(s_w - m)
    e_n = jnp.exp(s_n - m)
    r = 1.0 / (jnp.sum(e_w, axis=-1, keepdims=True) + e_n + jnp.exp(sk - m))
    p_w = (e_w * r).astype(BF16)
    o_all = jnp.stack([lax.dot_general(p_w[b], vt_ref[b].astype(BF16), contract_last,
                                       preferred_element_type=F32) for b in range(g)])
    o_own = jnp.where(own, o_all + (e_n * r) * vn, 0.0)
    o = o_own[:, :, 0:HEAD_DIM]
    for kv in range(1, N_KV_HEADS):
        o = o + o_own[:, :, kv * HEAD_DIM:(kv + 1) * HEAD_DIM]
    o_ref[...] = o.astype(o_ref.dtype)

    rows = jnp.concatenate([kn_ref[...], vn_ref[...]], axis=0).astype(BF16)
    place = (lax.broadcasted_iota(jnp.int32, (2 * g, 2 * g * WINDOW), 1)
             == WINDOW * lax.broadcasted_iota(jnp.int32, (2 * g, 2 * g * WINDOW), 0)
             ).astype(BF16)
    newcols = lax.dot_general(rows, place, (((0,), (0,)), ((), ())),
                              preferred_element_type=F32)
    first = lax.broadcasted_iota(jnp.int32, (KV_W, WINDOW), 1) == 0
    for b in range(g):
        kcol = newcols[:, b * WINDOW:(b + 1) * WINDOW]
        vcol = newcols[:, (g + b) * WINDOW:(g + b + 1) * WINDOW]
        kto_ref[b] = pltpu.roll(jnp.where(first, kcol, kt_ref[b]), WINDOW - 1, 1)
        vto_ref[b] = pltpu.roll(jnp.where(first, vcol, vt_ref[b]), WINDOW - 1, 1)


def _attn_sample(q3, k_new, v_new, k_buf, v_buf, bias_s, sinks):
    g = SAMPLES_PER_STEP
    nb = DEC_BATCH // g
    win = pl.BlockSpec((g, KV_W, WINDOW), lambda i: (i, 0, 0))
    row = pl.BlockSpec((g, KV_W), lambda i: (i, 0))
    return pl.pallas_call(
        _attn_sample_kernel,
        grid=(nb,),
        in_specs=[pl.BlockSpec((g, N_HEADS, HEAD_DIM), lambda i: (i, 0, 0)),
                  row, row, win, win,
                  pl.BlockSpec((N_HEADS, WINDOW + 1), lambda i: (0, 0)),
                  pl.BlockSpec((N_HEADS, 1), lambda i: (0, 0))],
        out_specs=[pl.BlockSpec((g, N_HEADS, HEAD_DIM), lambda i: (i, 0, 0)), win, win],
        out_shape=[jax.ShapeDtypeStruct((DEC_BATCH, N_HEADS, HEAD_DIM), BF16),
                   jax.ShapeDtypeStruct((DEC_BATCH, KV_W, WINDOW), F32),
                   jax.ShapeDtypeStruct((DEC_BATCH, KV_W, WINDOW), F32)],
        compiler_params=_cparams("parallel"),
        name="attn_sample",
    )(q3, k_new, v_new, k_buf, v_buf, bias_s, sinks.reshape(N_HEADS, 1))


def _merge_kernel(ap_ref, bp_ref, as_ref, bs_ref, waf_ref, wbf_ref,
                  gap_ref, gbp_ref, gas_ref, gbs_ref, op_ref, os_ref,
                  wa0_ref, wb0_ref, wa1_ref, wb1_ref):
    def compute(slot):
        wa_ref, wb_ref = slot

        def merged(a_ref, b_ref, ga_ref, gb_ref):
            return (_sigmoid(ga_ref[...].astype(F32)) * _dot(a_ref[...], wa_ref[...])
                    + _sigmoid(gb_ref[...].astype(F32)) * _dot(b_ref[...], wb_ref[...]))

        op_ref[...] = merged(ap_ref, bp_ref, gap_ref, gbp_ref).astype(op_ref.dtype)

        @pl.when(_is_last_row_tile())
        def _():
            os_ref[...] = merged(as_ref, bs_ref, gas_ref, gbs_ref).astype(os_ref.dtype)

    _ws_phases([waf_ref, wbf_ref], [wa0_ref, wb0_ref], [wa1_ref, wb1_ref], compute)


def _branch_merge(a_p, o_p, a_s, o_s, wa, wb, proj_p, proj_s, bm, bn):
    mp, k = a_p.shape
    ms = a_s.shape[0]
    n = wa.shape[1]
    grid = _ws_grid(n, bn, mp, bm)
    ga0 = OFF_GA // bn
    gb0 = OFF_GB // bn
    wchunk = pl.BlockSpec((k // grid[1], bn), _ws_wchunk(grid[0] - 1))
    wslot = pltpu.VMEM((k, bn), BF16)
    return pl.pallas_call(
        _merge_kernel,
        grid=grid,
        in_specs=[pl.BlockSpec((bm, k), _ws_rows),
                  pl.BlockSpec((bm, k), _ws_rows),
                  pl.BlockSpec((ms, k), lambda j, i: (0, 0)),
                  pl.BlockSpec((ms, k), lambda j, i: (0, 0)),
                  wchunk, wchunk,
                  pl.BlockSpec((bm, bn), _ws_tile(ga0)),
                  pl.BlockSpec((bm, bn), _ws_tile(gb0)),
                  pl.BlockSpec((ms, bn), _ws_stile(ga0)),
                  pl.BlockSpec((ms, bn), _ws_stile(gb0))],
        out_specs=[pl.BlockSpec((bm, bn), _ws_tile()),
                   pl.BlockSpec((ms, bn), _ws_stile())],
        out_shape=[jax.ShapeDtypeStruct((mp, n), BF16),
                   jax.ShapeDtypeStruct((ms, n), BF16)],
        scratch_shapes=[wslot, wslot, wslot, wslot],
        compiler_params=_cparams("arbitrary", "arbitrary"),
        name="branch_merge",
    )(a_p, o_p, a_s, o_s, wa, wb, proj_p, proj_p, proj_s, proj_s)


def _resid_kernel(xp_ref, xs_ref, wf_ref, rp_ref, rs_ref, op_ref, os_ref, w0_ref, w1_ref):
    def compute(slot):
        (w_ref,) = slot
        op_ref[...] = rp_ref[...] + _dot(xp_ref[...], w_ref[...])

        @pl.when(_is_last_row_tile())
        def _():
            os_ref[...] = rs_ref[...] + _dot(xs_ref[...], w_ref[...])

    _ws_phases([wf_ref], [w0_ref], [w1_ref], compute)


def _resid_matmul(xp, xs, w, rp, rs, bm, bn, name):
    mp, k = xp.shape
    ms = xs.shape[0]
    n = w.shape[1]
    grid = _ws_grid(n, bn, mp, bm)
    return pl.pallas_call(
        _resid_kernel,
        grid=grid,
        in_specs=[pl.BlockSpec((bm, k), _ws_rows),
                  pl.BlockSpec((ms, k), lambda j, i: (0, 0)),
                  pl.BlockSpec((k // grid[1], bn), _ws_wchunk(grid[0] - 1)),
                  pl.BlockSpec((bm, bn), _ws_tile()),
                  pl.BlockSpec((ms, bn), _ws_stile())],
        out_specs=[pl.BlockSpec((bm, bn), _ws_tile()),
                   pl.BlockSpec((ms, bn), _ws_stile())],
        out_shape=[jax.ShapeDtypeStruct((mp, n), F32),
                   jax.ShapeDtypeStruct((ms, n), F32)],
        scratch_shapes=[pltpu.VMEM((k, bn), BF16), pltpu.VMEM((k, bn), BF16)],
        compiler_params=_cparams("arbitrary", "arbitrary"),
        name=name,
    )(xp, xs, w, rp, rs)


LANES = 128


def _lane_partials(v):
    part = v[:, 0:LANES]
    for c in range(1, v.shape[1] // LANES):
        part = part + v[:, c * LANES:(c + 1) * LANES]
    return part


def _outproj_kernel(xp_ref, xs_ref, wf_ref, rp_ref, rs_ref, g_ref,
                    op_ref, os_ref, bp_ref, bs_ref, ssp_ref, sss_ref,
                    w0_ref, w1_ref, accp_ref, accs_ref):
    j = pl.program_id(0)
    i = pl.program_id(1)
    bm = xp_ref.shape[0]

    def compute(slot):
        (w_ref,) = slot

        def tile(x_ref, r_ref, o_ref, b_ref, acc_view, ss_ref):
            x1 = r_ref[...] + _dot(x_ref[...], w_ref[...])
            o_ref[...] = x1
            b_ref[...] = (x1 * g_ref[...]).astype(b_ref.dtype)
            total = jnp.where(j == 1, 0.0, acc_view[...]) + _lane_partials(x1 * x1)
            acc_view[...] = total
            ss_ref[...] = total

        rows = pl.ds(pl.multiple_of(i * bm, bm), bm)
        tile(xp_ref, rp_ref, op_ref, bp_ref, accp_ref.at[rows, :], ssp_ref)

        @pl.when(_is_last_row_tile())
        def _():
            tile(xs_ref, rs_ref, os_ref, bs_ref, accs_ref, sss_ref)

    _ws_phases([wf_ref], [w0_ref], [w1_ref], compute)


def _out_proj(xp, xs, w, rp, rs, gain, bm, bn):
    mp, k = xp.shape
    ms = xs.shape[0]
    n = w.shape[1]
    grid = _ws_grid(n, bn, mp, bm)
    last = grid[0] - 1
    ssp_map = lambda j, i: (jnp.where(j == last, i, 0), 0)
    return pl.pallas_call(
        _outproj_kernel,
        grid=grid,
        in_specs=[pl.BlockSpec((bm, k), _ws_rows),
                  pl.BlockSpec((ms, k), lambda j, i: (0, 0)),
                  pl.BlockSpec((k // grid[1], bn), _ws_wchunk(grid[0] - 1)),
                  pl.BlockSpec((bm, bn), _ws_tile()),
                  pl.BlockSpec((ms, bn), _ws_stile()),
                  pl.BlockSpec((1, bn), _ws_stile())],
        out_specs=[pl.BlockSpec((bm, bn), _ws_tile()),
                   pl.BlockSpec((ms, bn), _ws_stile()),
                   pl.BlockSpec((bm, bn), _ws_tile()),
                   pl.BlockSpec((ms, bn), _ws_stile()),
                   pl.BlockSpec((bm, LANES), ssp_map),
                   pl.BlockSpec((ms, LANES), lambda j, i: (0, 0))],
        out_shape=[jax.ShapeDtypeStruct((mp, n), F32),
                   jax.ShapeDtypeStruct((ms, n), F32),
                   jax.ShapeDtypeStruct((mp, n), BF16),
                   jax.ShapeDtypeStruct((ms, n), BF16),
                   jax.ShapeDtypeStruct((mp, LANES), F32),
                   jax.ShapeDtypeStruct((ms, LANES), F32)],
        scratch_shapes=[pltpu.VMEM((k, bn), BF16), pltpu.VMEM((k, bn), BF16),
                        pltpu.VMEM((mp, LANES), F32), pltpu.VMEM((ms, LANES), F32)],
        compiler_params=_cparams("arbitrary", "arbitrary"),
        name="out_proj",
    )(xp, xs, w, rp, rs, gain.reshape(1, n))


def _inv_rms(ss_ref):
    return lax.rsqrt(jnp.sum(ss_ref[...], axis=-1, keepdims=True) * (1.0 / D_MODEL) + EPS)


def _ffn_up_kernel(hp_ref, hs_ref, ssp_ref, sss_ref, wgf_ref, wuf_ref, cw_ref, cb_ref,
                   s0_ref, s1_ref, fp_ref, fs_ref, gs_ref, st_ref,
                   wg0_ref, wu0_ref, wg1_ref, wu1_ref, carry_ref, gbuf_ref,
                   *, tiles_per_seq, last_phase, last_width):
    i = pl.program_id(1)
    bm = hp_ref.shape[0]

    def compute(slot, width):
        wg_ref, wu_ref = slot
        cols = slice(0, width)
        w0 = cw_ref[0:1, cols]
        w1 = cw_ref[1:2, cols]
        w2 = cw_ref[2:3, cols]
        bias = cb_ref[:, cols]

        @pl.when(i % tiles_per_seq == 0)
        def _():
            carry_ref[...] = jnp.zeros(carry_ref.shape, F32)

        r = _inv_rms(ssp_ref)
        g = r * _dot(hp_ref[...], wg_ref[:, cols])
        gbuf_ref[0:SUBLANES, cols] = carry_ref[:, cols]
        gbuf_ref[SUBLANES:SUBLANES + bm, cols] = g
        g1 = gbuf_ref[SUBLANES - 1:SUBLANES - 1 + bm, cols]
        g2 = gbuf_ref[SUBLANES - 2:SUBLANES - 2 + bm, cols]
        gc = w0 * g2 + w1 * g1 + w2 * g + bias
        up = r * _dot(hp_ref[...], wu_ref[:, cols])
        fp_ref[:, cols] = (gc * _sigmoid(gc) * up).astype(fp_ref.dtype)
        tail = gbuf_ref[bm:SUBLANES + bm, cols]
        carry_ref[:, cols] = tail
        st_ref[0, :, cols] = tail

        @pl.when(_is_last_row_tile())
        def _():
            rs = _inv_rms(sss_ref)
            gsm = rs * _dot(hs_ref[...], wg_ref[:, cols])
            ups = rs * _dot(hs_ref[...], wu_ref[:, cols])
            gcs = w0 * s0_ref[:, cols] + w1 * s1_ref[:, cols] + w2 * gsm + bias
            fs_ref[:, cols] = (gcs * _sigmoid(gcs) * ups).astype(fs_ref.dtype)
            gs_ref[:, cols] = gsm

    full_width = fp_ref.shape[1]
    _ws_phases([wgf_ref, wuf_ref], [wg0_ref, wu0_ref], [wg1_ref, wu1_ref],
               functools.partial(compute, width=full_width),
               functools.partial(compute, width=last_width) if last_width < full_width else None,
               last_phase)


def _ffn_up(hp, hs, ssp, sss, wg, wu, cw, cb, s0, s1, bm, bn):
    mp, k = hp.shape
    ms = hs.shape[0]
    n = wg.shape[1]
    grid = _ws_grid(n, bn, mp, bm)
    tiles_per_seq = SEQ // bm
    wchunk = pl.BlockSpec((k // grid[1], bn), _ws_wchunk(grid[0] - 1))
    wslot = pltpu.VMEM((k, bn), BF16)
    col = _ws_stile()

    def tail_map(j, i):
        return (jnp.where(j > 0, i, 0) // tiles_per_seq, 0, jnp.maximum(j - 1, 0))

    last_phase = grid[0] - 1
    last_width = n - (last_phase - 1) * bn
    return pl.pallas_call(
        functools.partial(_ffn_up_kernel, tiles_per_seq=tiles_per_seq,
                          last_phase=last_phase, last_width=last_width),
        grid=grid,
        in_specs=[pl.BlockSpec((bm, k), _ws_rows),
                  pl.BlockSpec((ms, k), lambda j, i: (0, 0)),
                  pl.BlockSpec((bm, LANES), _ws_rows),
                  pl.BlockSpec((ms, LANES), lambda j, i: (0, 0)),
                  wchunk, wchunk,
                  pl.BlockSpec((FFN_CONV_K, bn), col),
                  pl.BlockSpec((1, bn), col),
                  pl.BlockSpec((ms, bn), col),
                  pl.BlockSpec((ms, bn), col)],
        out_specs=[pl.BlockSpec((bm, bn), _ws_tile()),
                   pl.BlockSpec((ms, bn), col),
                   pl.BlockSpec((ms, bn), col),
                   pl.BlockSpec((1, SUBLANES, bn), tail_map)],
        out_shape=[jax.ShapeDtypeStruct((mp, n), BF16),
                   jax.ShapeDtypeStruct((ms, n), BF16),
                   jax.ShapeDtypeStruct((ms, n), F32),
                   jax.ShapeDtypeStruct((BATCH, SUBLANES, n), F32)],
        scratch_shapes=[wslot, wslot, wslot, wslot,
                        pltpu.VMEM((SUBLANES, bn), F32),
                        pltpu.VMEM((SUBLANES + bm, bn), F32)],
        compiler_params=_cparams("arbitrary", "arbitrary"),
        name="ffn_up",
    )(hp, hs, ssp, sss, wg, wu, cw, cb, s0, s1)


def _window_to_kernel(state):
    return jnp.transpose(state[0], (0, 2, 3, 1)).reshape(DEC_BATCH, KV_W, WINDOW)


def _window_from_kernel(win):
    win = win.reshape(DEC_BATCH, N_KV_HEADS, HEAD_DIM, WINDOW)
    return jnp.transpose(win, (0, 3, 1, 2))[None]


def kernel(x_prompt, x_sample, state_k_window, state_v_window, state_conv, state_ffn_conv,
           attn_norm_g, w_in, conv_w, w_branch_a, w_branch_b, sinks, w_out, ffn_norm_g,
           w_ffn_gate, w_ffn_up, ffn_conv_w, ffn_conv_b, w_ffn_down, rel_bias,
           final_norm_g):
    xp = x_prompt.reshape(M_PROMPT, D_MODEL)
    xs = x_sample.reshape(DEC_BATCH, D_MODEL)

    fs0 = state_ffn_conv[0][:, 0, :]
    fs1 = state_ffn_conv[0][:, 1, :]

    cs0 = state_conv[0][:, 0, :]
    cs1 = state_conv[0][:, 1, :]
    hs = _rmsnorm(xs, attn_norm_g[0], BF16, DEC_BATCH)
    a_p, a_s, u_s, u_tail, hp, hssp = _mixer_a(
        xp, hs, attn_norm_g[0], w_in[0], conv_w[0], cs0, cs1, 512, 512)
    proj_p, proj_s = _in_proj(hp, hs, hssp, w_in[0], M_PROMPT, W_REST, PROJ_W, 1024, 1024)

    bias_tab = _bias_table(rel_bias)
    o_p = _attn_prompt(proj_p, bias_tab, sinks[0])
    q3 = proj_s[:, OFF_Q:OFF_Q + ATTN_W].reshape(DEC_BATCH, N_HEADS, HEAD_DIM)
    k_new = proj_s[:, OFF_K:OFF_K + KV_W].astype(F32)
    v_new = proj_s[:, OFF_V:OFF_V + KV_W].astype(F32)
    bias_s = bias_tab[0, :, Q_BLOCK - 1:, Q_BLOCK - 1]
    o_s3, k_win_s, v_win_s = _attn_sample(
        q3, k_new, v_new,
        _window_to_kernel(state_k_window), _window_to_kernel(state_v_window),
        bias_s, sinks[0])
    o_s = o_s3.reshape(DEC_BATCH, ATTN_W)

    mg_p, mg_s = _branch_merge(a_p, o_p, a_s, o_s, w_branch_a[0], w_branch_b[0],
                               proj_p, proj_s, 1024, 1024)
    x1p, x1s, xbp, xbs, ssp, sss = _out_proj(mg_p, mg_s, w_out[0], xp, xs, ffn_norm_g[0],
                                             512, 1024)

    f_p, f_s, g_s, g_tail = _ffn_up(xbp, xbs, ssp, sss, w_ffn_gate[0], w_ffn_up[0],
                                    ffn_conv_w[0], ffn_conv_b, fs0, fs1, 1024, 512)
    x2p, x2s = _resid_matmul(f_p, f_s, w_ffn_down[0], x1p, x1s, 512, 512, "ffn_down")
    y_p = _rmsnorm(x2p, final_norm_g, F32, 256)
    y_s = _rmsnorm(x2s, final_norm_g, F32, DEC_BATCH)

    kv_p = proj_p.reshape(BATCH, SEQ, PROJ_W)[:, SEQ - WINDOW:, OFF_K:OFF_GA].astype(F32)
    k_win_p = kv_p[:, :, :KV_W].reshape(1, BATCH, WINDOW, N_KV_HEADS, HEAD_DIM)
    v_win_p = kv_p[:, :, KV_W:].reshape(1, BATCH, WINDOW, N_KV_HEADS, HEAD_DIM)
    conv_p = u_tail[:, SUBLANES - (CONV_K - 1):, :]
    ffn_p = g_tail[:, SUBLANES - (FFN_CONV_K - 1):, :]
    conv_s = jnp.stack([cs1, u_s], axis=1)
    ffn_s = jnp.stack([fs1, g_s], axis=1)

    return (y_p.reshape(BATCH, SEQ, D_MODEL),
            y_s.reshape(DEC_BATCH, 1, D_MODEL),
            k_win_p, v_win_p, conv_p[None], ffn_p[None],
            _window_from_kernel(k_win_s), _window_from_kernel(v_win_s),
            conv_s[None], ffn_s[None])
```

```python
import functools
import math

import numpy as np
import jax
import jax.numpy as jnp
from jax import lax
from jax.experimental import pallas as pl
from jax.experimental.pallas import tpu as pltpu

F32 = jnp.float32
BF16 = jnp.bfloat16

D_MODEL = 4096
BATCH = 4
SEQ = 2048
DEC_BATCH = 128
N_HEADS = 32
N_KV_HEADS = 8
HEAD_DIM = 64
GROUP = N_HEADS // N_KV_HEADS
ATTN_W = N_HEADS * HEAD_DIM
KV_W = N_KV_HEADS * HEAD_DIM
CONV_W = D_MODEL // 2
CONV_K = 3
WINDOW = 128
Q_BLOCK = 128
N_BUCKETS = 32
MAX_DISTANCE = 128
D_FF = 11008
FFN_CONV_K = 3
EPS = 1e-5
NEG = -1e30
M_PROMPT = BATCH * SEQ

W_CB = 0
W_CC = CONV_W
W_CH = 2 * CONV_W
W_REST = 3 * CONV_W
OFF_Q = 0
OFF_K = OFF_Q + ATTN_W
OFF_V = OFF_K + KV_W
OFF_GA = OFF_V + KV_W
OFF_GB = OFF_GA + D_MODEL
PROJ_W = OFF_GB + D_MODEL

V7X_VMEM_BYTES = 64 * 1024 * 1024
VMEM_LIMIT = V7X_VMEM_BYTES - 1024 * 1024
SUBLANES = 8
LANES = 128

MIXER_TILE = (512, 512)
PROJ_TILE = (1024, 1024)
MERGE_TILE = (1024, 1024)
OUT_TILE = (512, 1024)
FFN_UP_TILE = (1024, 512)
FFN_DOWN_TILE = (512, 512)
NORM_ROWS = 256


def _cparams(*sem):
    return pltpu.CompilerParams(dimension_semantics=sem, vmem_limit_bytes=VMEM_LIMIT)


def _dot(a, b):
    return jnp.dot(a, b, preferred_element_type=F32)


def _sigmoid(x):
    return 1.0 / (1.0 + jnp.exp(-x))


def _ws_grid(n, bn, mp, bm):
    return (pl.cdiv(n, bn) + 1, mp // bm)


def _ws_rows(j, i):
    return (jnp.where(j > 0, i, 0), 0)


def _ws_wchunk(nj, col0=0):
    return lambda j, i: (i, col0 + jnp.minimum(j, nj - 1))


def _ws_tile(col0=0):
    return lambda j, i: (jnp.where(j > 0, i, 0), col0 + jnp.maximum(j - 1, 0))


def _ws_stile(col0=0):
    return lambda j, i: (0, col0 + jnp.maximum(j - 1, 0))


def _ws_cast(wf_refs, dst_refs):
    i = pl.program_id(1)
    for wf_ref, dst_ref in zip(wf_refs, dst_refs):
        rows = wf_ref.shape[0]
        r0 = pl.multiple_of(i * rows, rows)
        dst_ref[pl.ds(r0, rows), :] = wf_ref[...].astype(BF16)


def _ws_phases(wf_refs, slot0, slot1, compute, compute_last=None, last=None):
    j = pl.program_id(0)
    regular = (j > 0) if compute_last is None else ((j > 0) & (j < last))

    @pl.when(j == 0)
    def _():
        _ws_cast(wf_refs, slot0)

    @pl.when(regular & (j % 2 == 1))
    def _():
        _ws_cast(wf_refs, slot1)
        compute(slot0)

    @pl.when(regular & (j % 2 == 0))
    def _():
        _ws_cast(wf_refs, slot0)
        compute(slot1)

    if compute_last is not None:
        @pl.when(j == last)
        def _():
            compute_last(slot0 if last % 2 == 1 else slot1)


def _is_last_row_tile():
    return pl.program_id(1) == pl.num_programs(1) - 1


def _rms_kernel(x_ref, g_ref, o_ref):
    x = x_ref[...]
    r = lax.rsqrt(jnp.mean(x * x, axis=-1, keepdims=True) + EPS)
    o_ref[...] = ((x * r) * g_ref[...]).astype(o_ref.dtype)


def _rmsnorm(x, g, out_dtype, block_rows):
    rows, d = x.shape
    return pl.pallas_call(
        _rms_kernel,
        grid=(rows // block_rows,),
        in_specs=[pl.BlockSpec((block_rows, d), lambda i: (i, 0)),
                  pl.BlockSpec((1, d), lambda i: (0, 0))],
        out_specs=pl.BlockSpec((block_rows, d), lambda i: (i, 0)),
        out_shape=jax.ShapeDtypeStruct((rows, d), out_dtype),
        compiler_params=_cparams("parallel"),
        name="rmsnorm",
    )(x, g.reshape(1, d))


def _proj_kernel(xp_ref, xs_ref, ssp_ref, wf_ref, op_ref, os_ref, w0_ref, w1_ref):
    def compute(slot):
        (w_ref,) = slot
        op_ref[...] = (_inv_rms(ssp_ref) * _dot(xp_ref[...], w_ref[...])).astype(op_ref.dtype)

        @pl.when(_is_last_row_tile())
        def _():
            os_ref[...] = _dot(xs_ref[...], w_ref[...]).astype(os_ref.dtype)

    _ws_phases([wf_ref], [w0_ref], [w1_ref], compute)


def _in_proj(hp, hs, ssp, w, mp, col0, n, bm, bn):
    k = hp.shape[1]
    ms = hs.shape[0]
    grid = _ws_grid(n, bn, mp, bm)
    return pl.pallas_call(
        _proj_kernel,
        grid=grid,
        in_specs=[pl.BlockSpec((bm, k), _ws_rows),
                  pl.BlockSpec((ms, k), lambda j, i: (0, 0)),
                  pl.BlockSpec((bm, LANES), _ws_rows),
                  pl.BlockSpec((k // grid[1], bn), _ws_wchunk(grid[0] - 1, col0 // bn))],
        out_specs=[pl.BlockSpec((bm, bn), _ws_tile()),
                   pl.BlockSpec((ms, bn), _ws_stile())],
        out_shape=[jax.ShapeDtypeStruct((mp, n), BF16),
                   jax.ShapeDtypeStruct((ms, n), BF16)],
        scratch_shapes=[pltpu.VMEM((k, bn), BF16), pltpu.VMEM((k, bn), BF16)],
        compiler_params=_cparams("arbitrary", "arbitrary"),
        name="in_proj",
    )(hp, hs, ssp, w)


def _mixa_kernel(xp_ref, hs_ref, g_ref, wbf_ref, wcf_ref, whf_ref, cw_ref, s0_ref, s1_ref,
                 ap_ref, as_ref, us_ref, st_ref, hp_ref, ssp_ref,
                 wb0_ref, wc0_ref, wh0_ref, wb1_ref, wc1_ref, wh1_ref, carry_ref, ubuf_ref,
                 *, tiles_per_seq):
    i = pl.program_id(1)
    bm = xp_ref.shape[0]

    def compute(slot):
        wb_ref, wc_ref, wh_ref = slot
        w0 = cw_ref[0:1, :]
        w1 = cw_ref[1:2, :]
        w2 = cw_ref[2:3, :]

        @pl.when(i % tiles_per_seq == 0)
        def _():
            carry_ref[...] = jnp.zeros(carry_ref.shape, F32)

        hp_ref[...] = (xp_ref[...] * g_ref[...]).astype(hp_ref.dtype)
        ssp_ref[...] = _lane_partials(xp_ref[...] * xp_ref[...])
        r = _inv_rms(ssp_ref)
        u = (r * _dot(hp_ref[...], wc_ref[...])) * (r * _dot(hp_ref[...], wh_ref[...]))
        ubuf_ref[0:SUBLANES, :] = carry_ref[...]
        ubuf_ref[SUBLANES:SUBLANES + bm, :] = u
        u1 = ubuf_ref[SUBLANES - 1:SUBLANES - 1 + bm, :]
        u2 = ubuf_ref[SUBLANES - 2:SUBLANES - 2 + bm, :]
        z = w0 * u2 + w1 * u1 + w2 * u
        ap_ref[...] = ((r * _dot(hp_ref[...], wb_ref[...])) * z).astype(ap_ref.dtype)
        tail = ubuf_ref[bm:SUBLANES + bm, :]
        carry_ref[...] = tail
        st_ref[0] = tail

        @pl.when(_is_last_row_tile())
        def _():
            us = _dot(hs_ref[...], wc_ref[...]) * _dot(hs_ref[...], wh_ref[...])
            zs = w0 * s0_ref[...] + w1 * s1_ref[...] + w2 * us
            as_ref[...] = (_dot(hs_ref[...], wb_ref[...]) * zs).astype(as_ref.dtype)
            us_ref[...] = us

    _ws_phases([wbf_ref, wcf_ref, whf_ref], [wb0_ref, wc0_ref, wh0_ref],
               [wb1_ref, wc1_ref, wh1_ref], compute)


def _mixer_a(xp, hs, gain, w_in, conv_w, s0, s1, bm, bn):
    mp, k = xp.shape
    ms = hs.shape[0]
    n = CONV_W
    grid = _ws_grid(n, bn, mp, bm)
    nj = grid[0] - 1
    tiles_per_seq = SEQ // bm
    wslot = pltpu.VMEM((k, bn), BF16)
    col = _ws_stile()

    def wchunk(col0):
        return pl.BlockSpec((k // grid[1], bn), _ws_wchunk(nj, col0 // bn))

    def tail_map(j, i):
        return (jnp.where(j > 0, i, 0) // tiles_per_seq, 0, jnp.maximum(j - 1, 0))

    spare = grid[1]

    def operand_map(j, i):
        return (jnp.where(j == 1, i, jnp.where(j == 0, 0, spare)), 0)

    return pl.pallas_call(
        functools.partial(_mixa_kernel, tiles_per_seq=tiles_per_seq),
        grid=grid,
        in_specs=[pl.BlockSpec((bm, k), _ws_rows),
                  pl.BlockSpec((ms, k), lambda j, i: (0, 0)),
                  pl.BlockSpec((1, k), lambda j, i: (0, 0)),
                  wchunk(W_CB), wchunk(W_CC), wchunk(W_CH),
                  pl.BlockSpec((CONV_K, bn), col),
                  pl.BlockSpec((ms, bn), col),
                  pl.BlockSpec((ms, bn), col)],
        out_specs=[pl.BlockSpec((bm, bn), _ws_tile()),
                   pl.BlockSpec((ms, bn), col),
                   pl.BlockSpec((ms, bn), col),
                   pl.BlockSpec((1, SUBLANES, bn), tail_map),
                   pl.BlockSpec((bm, k), operand_map),
                   pl.BlockSpec((bm, LANES), operand_map)],
        out_shape=[jax.ShapeDtypeStruct((mp, n), BF16),
                   jax.ShapeDtypeStruct((ms, n), BF16),
                   jax.ShapeDtypeStruct((ms, n), F32),
                   jax.ShapeDtypeStruct((BATCH, SUBLANES, n), F32),
                   jax.ShapeDtypeStruct((mp + bm, k), BF16),
                   jax.ShapeDtypeStruct((mp + bm, LANES), F32)],
        scratch_shapes=[wslot] * 6 + [pltpu.VMEM((SUBLANES, bn), F32),
                                      pltpu.VMEM((SUBLANES + bm, bn), F32)],
        compiler_params=_cparams("arbitrary", "arbitrary"),
        name="mixer_a",
    )(xp, hs, gain.reshape(1, k), w_in, w_in, w_in, conv_w, s0, s1)


def _bucket_table():
    qi = np.arange(Q_BLOCK)[None, :]
    kj = np.arange(2 * Q_BLOCK)[:, None]
    dist = qi + Q_BLOCK - kj
    max_exact = N_BUCKETS // 2
    d = np.maximum(dist, 0)
    df = np.maximum(d, 1).astype(np.float32)
    large = max_exact + (np.log(df / np.float32(max_exact))
                         / np.float32(math.log(MAX_DISTANCE / max_exact))
                         * np.float32(N_BUCKETS - max_exact)).astype(np.int32)
    large = np.minimum(large, N_BUCKETS - 1)
    bucket = np.where(d < max_exact, d, large).astype(np.int32)
    valid = ((dist >= 0) & (dist <= WINDOW)).astype(np.int32)
    return bucket, valid


def _bias_kernel(rb_ref, bucket_ref, valid_ref, o_ref):
    bucket = bucket_ref[...]
    row = lax.broadcasted_iota(jnp.int32, bucket.shape, 0)
    keep = valid_ref[...] != 0
    keep_first = keep & (row >= Q_BLOCK)

    def head(h, carry):
        acc = jnp.zeros(bucket.shape, F32)
        for b in range(N_BUCKETS):
            acc = jnp.where(bucket == b, rb_ref[b, h], acc)
        o_ref[0, h] = jnp.where(keep, acc, NEG)
        o_ref[1, h] = jnp.where(keep_first, acc, NEG)
        return carry

    lax.fori_loop(0, N_HEADS, head, 0)


def _bias_table(rel_bias):
    bucket, valid = _bucket_table()
    shp = (2 * Q_BLOCK, Q_BLOCK)
    return pl.pallas_call(
        _bias_kernel,
        grid=(1,),
        in_specs=[pl.BlockSpec(memory_space=pltpu.SMEM),
                  pl.BlockSpec(shp, lambda v: (0, 0)),
                  pl.BlockSpec(shp, lambda v: (0, 0))],
        out_specs=pl.BlockSpec((2, N_HEADS) + shp, lambda v: (0, 0, 0, 0)),
        out_shape=jax.ShapeDtypeStruct((2, N_HEADS) + shp, F32),
        compiler_params=_cparams("arbitrary"),
        name="rel_bias_table",
    )(rel_bias, jnp.asarray(bucket), jnp.asarray(valid))


BLOCKS_PER_STEP = 2


def _attn_prompt_kernel(sink_ref, q_ref, kvp_ref, kvc_ref, bias_ref, o_ref):
    scale = HEAD_DIM ** -0.5
    first_variant = jnp.where(pl.program_id(1) == 0, 1, 0)

    def band(blk, col):
        cols = slice(col, col + HEAD_DIM)
        cur = kvc_ref[blk * Q_BLOCK:(blk + 1) * Q_BLOCK, cols]
        if blk == 0:
            return jnp.concatenate([kvp_ref[:, cols], cur], axis=0)
        return kvc_ref[(blk - 1) * Q_BLOCK:(blk + 1) * Q_BLOCK, cols]

    def scores(blk, kv):
        heads = range(kv * GROUP, (kv + 1) * GROUP)
        rows = slice(blk * Q_BLOCK, (blk + 1) * Q_BLOCK)
        variant = first_variant if blk == 0 else 0
        qg = jnp.concatenate(
            [q_ref[rows, h * HEAD_DIM:(h + 1) * HEAD_DIM] for h in heads], axis=0) * scale
        st = lax.dot_general(band(blk, kv * HEAD_DIM), qg, (((1,), (1,)), ((), ())),
                             preferred_element_type=F32)
        return st + jnp.concatenate([bias_ref[variant, h] for h in heads], axis=1)

    def finish(blk, kv, st):
        heads = range(kv * GROUP, (kv + 1) * GROUP)
        vband = band(blk, KV_W + kv * HEAD_DIM)
        sk = jnp.concatenate([jnp.full((1, Q_BLOCK), sink_ref[h], F32) for h in heads], axis=1)
        m = jnp.maximum(jnp.max(st, axis=0, keepdims=True), sk)
        e = jnp.exp(st - m)
        denom = jnp.sum(e, axis=0, keepdims=True) + jnp.exp(sk - m)
        ot = lax.dot_general(vband, e.astype(BF16), (((0,), (0,)), ((), ())),
                             preferred_element_type=F32) * (1.0 / denom)
        o2 = jnp.concatenate(
            [ot[:, g * Q_BLOCK:(g + 1) * Q_BLOCK] for g in range(GROUP)], axis=0)
        rows = slice(blk * Q_BLOCK, (blk + 1) * Q_BLOCK)
        gs = slice(kv * GROUP * HEAD_DIM, (kv + 1) * GROUP * HEAD_DIM)
        o_ref[rows, gs] = o2.T.astype(o_ref.dtype)

    groups = [(blk, kv) for blk in range(BLOCKS_PER_STEP) for kv in range(N_KV_HEADS)]
    st = scores(*groups[0])
    for idx, grp in enumerate(groups):
        st_next = scores(*groups[idx + 1]) if idx + 1 < len(groups) else None
        finish(*grp, st)
        st = st_next


def _attn_prompt(proj_p, bias_tab, sinks):
    rows = BLOCKS_PER_STEP * Q_BLOCK
    steps = SEQ // rows
    qcol = OFF_Q // ATTN_W
    kvcol = OFF_K // (2 * KV_W)

    def prev(b, n):
        return b * (SEQ // Q_BLOCK) + jnp.maximum(n * BLOCKS_PER_STEP - 1, 0)

    return pl.pallas_call(
        _attn_prompt_kernel,
        grid=(BATCH, steps),
        in_specs=[pl.BlockSpec(memory_space=pltpu.SMEM),
                  pl.BlockSpec((rows, ATTN_W), lambda b, n: (b * steps + n, qcol)),
                  pl.BlockSpec((Q_BLOCK, 2 * KV_W), lambda b, n: (prev(b, n), kvcol)),
                  pl.BlockSpec((rows, 2 * KV_W), lambda b, n: (b * steps + n, kvcol)),
                  pl.BlockSpec((2, N_HEADS, 2 * Q_BLOCK, Q_BLOCK), lambda b, n: (0, 0, 0, 0))],
        out_specs=pl.BlockSpec((rows, ATTN_W), lambda b, n: (b * steps + n, 0)),
        out_shape=jax.ShapeDtypeStruct((M_PROMPT, ATTN_W), BF16),
        compiler_params=_cparams("parallel", "arbitrary"),
        name="attn_prompt",
    )(sinks, proj_p, proj_p, proj_p, bias_tab)


SAMPLES_PER_STEP = 8


def _attn_sample_kernel(q_ref, kn_ref, vn_ref, kt_ref, vt_ref, bias_ref, sink_ref,
                        o_ref, kto_ref, vto_ref):
    g = SAMPLES_PER_STEP
    scale = HEAD_DIM ** -0.5
    erow = lax.broadcasted_iota(jnp.int32, (HEAD_DIM, KV_W), 0)
    ecol = lax.broadcasted_iota(jnp.int32, (HEAD_DIM, KV_W), 1)
    expand = (ecol % HEAD_DIM == erow).astype(BF16)
    hrow = lax.broadcasted_iota(jnp.int32, (N_HEADS, KV_W), 0)
    hcol = lax.broadcasted_iota(jnp.int32, (N_HEADS, KV_W), 1)
    own = ((hrow // GROUP) == (hcol // HEAD_DIM))[None]
    bias_w = bias_ref[:, 0:WINDOW][None]
    bias_n = bias_ref[:, WINDOW:WINDOW + 1][None]
    sk = sink_ref[...][None]
    contract_last = (((1,), (1,)), ((), ()))

    q_all = q_ref[...].reshape(g * N_HEADS, HEAD_DIM) * scale
    qrow = jnp.where(own, _dot(q_all, expand).reshape(g, N_HEADS, KV_W), 0.0)
    qrow_b = qrow.astype(BF16)
    kn = kn_ref[...][:, None, :]
    vn = vn_ref[...][:, None, :]
    s_w = jnp.stack([_dot(qrow_b[b], kt_ref[b].astype(BF16)) for b in range(g)])
    s_w = s_w + bias_w
    s_n = jnp.sum(qrow * kn, axis=-1, keepdims=True) + bias_n
    m = jnp.maximum(jnp.maximum(jnp.max(s_w, axis=-1, keepdims=True), s_n), sk)
    e_w = jnp.exp(s_w - m)
    e_n = jnp.exp(s_n - m)
    r = 1.0 / (jnp.sum(e_w, axis=-1, keepdims=True) + e_n + jnp.exp(sk - m))
    p_w = (e_w * r).astype(BF16)
    o_all = jnp.stack([lax.dot_general(p_w[b], vt_ref[b].astype(BF16), contract_last,
                                       preferred_element_type=F32) for b in range(g)])
    o_own = jnp.where(own, o_all + (e_n * r) * vn, 0.0)
    o = o_own[:, :, 0:HEAD_DIM]
    for kv in range(1, N_KV_HEADS):
        o = o + o_own[:, :, kv * HEAD_DIM:(kv + 1) * HEAD_DIM]
    o_ref[...] = o.astype(o_ref.dtype)

    rows = jnp.concatenate([kn_ref[...], vn_ref[...]], axis=0).astype(BF16)
    place = (lax.broadcasted_iota(jnp.int32, (2 * g, 2 * g * WINDOW), 1)
             == WINDOW * lax.broadcasted_iota(jnp.int32, (2 * g, 2 * g * WINDOW), 0)
             ).astype(BF16)
    newcols = lax.dot_general(rows, place, (((0,), (0,)), ((), ())),
                              preferred_element_type=F32)
    first = lax.broadcasted_iota(jnp.int32, (KV_W, WINDOW), 1) == 0
    for b in range(g):
        kcol = newcols[:, b * WINDOW:(b + 1) * WINDOW]
        vcol = newcols[:, (g + b) * WINDOW:(g + b + 1) * WINDOW]
        kto_ref[b] = pltpu.roll(jnp.where(first, kcol, kt_ref[b]), WINDOW - 1, 1)
        vto_ref[b] = pltpu.roll(jnp.where(first, vcol, vt_ref[b]), WINDOW - 1, 1)


def _attn_sample(q3, k_new, v_new, k_buf, v_buf, bias_s, sinks):
    g = SAMPLES_PER_STEP
    nb = DEC_BATCH // g
    win = pl.BlockSpec((g, KV_W, WINDOW), lambda i: (i, 0, 0))
    row = pl.BlockSpec((g, KV_W), lambda i: (i, 0))
    return pl.pallas_call(
        _attn_sample_kernel,
        grid=(nb,),
        in_specs=[pl.BlockSpec((g, N_HEADS, HEAD_DIM), lambda i: (i, 0, 0)),
                  row, row, win, win,
                  pl.BlockSpec((N_HEADS, WINDOW + 1), lambda i: (0, 0)),
                  pl.BlockSpec((N_HEADS, 1), lambda i: (0, 0))],
        out_specs=[pl.BlockSpec((g, N_HEADS, HEAD_DIM), lambda i: (i, 0, 0)), win, win],
        out_shape=[jax.ShapeDtypeStruct((DEC_BATCH, N_HEADS, HEAD_DIM), BF16),
                   jax.ShapeDtypeStruct((DEC_BATCH, KV_W, WINDOW), F32),
                   jax.ShapeDtypeStruct((DEC_BATCH, KV_W, WINDOW), F32)],
        compiler_params=_cparams("parallel"),
        name="attn_sample",
    )(q3, k_new, v_new, k_buf, v_buf, bias_s, sinks.reshape(N_HEADS, 1))


def _merge_kernel(ap_ref, bp_ref, as_ref, bs_ref, waf_ref, wbf_ref,
                  gap_ref, gbp_ref, gas_ref, gbs_ref, op_ref, os_ref,
                  wa0_ref, wb0_ref, wa1_ref, wb1_ref):
    def compute(slot):
        wa_ref, wb_ref = slot

        def merged(a_ref, b_ref, ga_ref, gb_ref):
            return (_sigmoid(ga_ref[...].astype(F32)) * _dot(a_ref[...], wa_ref[...])
                    + _sigmoid(gb_ref[...].astype(F32)) * _dot(b_ref[...], wb_ref[...]))

        op_ref[...] = merged(ap_ref, bp_ref, gap_ref, gbp_ref).astype(op_ref.dtype)

        @pl.when(_is_last_row_tile())
        def _():
            os_ref[...] = merged(as_ref, bs_ref, gas_ref, gbs_ref).astype(os_ref.dtype)

    _ws_phases([waf_ref, wbf_ref], [wa0_ref, wb0_ref], [wa1_ref, wb1_ref], compute)


def _branch_merge(a_p, o_p, a_s, o_s, wa, wb, proj_p, proj_s, bm, bn):
    mp, k = a_p.shape
    ms = a_s.shape[0]
    n = wa.shape[1]
    grid = _ws_grid(n, bn, mp, bm)
    ga0 = OFF_GA // bn
    gb0 = OFF_GB // bn
    wchunk = pl.BlockSpec((k // grid[1], bn), _ws_wchunk(grid[0] - 1))
    wslot = pltpu.VMEM((k, bn), BF16)
    return pl.pallas_call(
        _merge_kernel,
        grid=grid,
        in_specs=[pl.BlockSpec((bm, k), _ws_rows),
                  pl.BlockSpec((bm, k), _ws_rows),
                  pl.BlockSpec((ms, k), lambda j, i: (0, 0)),
                  pl.BlockSpec((ms, k), lambda j, i: (0, 0)),
                  wchunk, wchunk,
                  pl.BlockSpec((bm, bn), _ws_tile(ga0)),
                  pl.BlockSpec((bm, bn), _ws_tile(gb0)),
                  pl.BlockSpec((ms, bn), _ws_stile(ga0)),
                  pl.BlockSpec((ms, bn), _ws_stile(gb0))],
        out_specs=[pl.BlockSpec((bm, bn), _ws_tile()),
                   pl.BlockSpec((ms, bn), _ws_stile())],
        out_shape=[jax.ShapeDtypeStruct((mp, n), BF16),
                   jax.ShapeDtypeStruct((ms, n), BF16)],
        scratch_shapes=[wslot, wslot, wslot, wslot],
        compiler_params=_cparams("arbitrary", "arbitrary"),
        name="branch_merge",
    )(a_p, o_p, a_s, o_s, wa, wb, proj_p, proj_p, proj_s, proj_s)


def _resid_kernel(xp_ref, xs_ref, wf_ref, rp_ref, rs_ref, op_ref, os_ref, w0_ref, w1_ref):
    def compute(slot):
        (w_ref,) = slot
        op_ref[...] = rp_ref[...] + _dot(xp_ref[...], w_ref[...])

        @pl.when(_is_last_row_tile())
        def _():
            os_ref[...] = rs_ref[...] + _dot(xs_ref[...], w_ref[...])

    _ws_phases([wf_ref], [w0_ref], [w1_ref], compute)


def _resid_matmul(xp, xs, w, rp, rs, bm, bn, name):
    mp, k = xp.shape
    ms = xs.shape[0]
    n = w.shape[1]
    grid = _ws_grid(n, bn, mp, bm)
    return pl.pallas_call(
        _resid_kernel,
        grid=grid,
        in_specs=[pl.BlockSpec((bm, k), _ws_rows),
                  pl.BlockSpec((ms, k), lambda j, i: (0, 0)),
                  pl.BlockSpec((k // grid[1], bn), _ws_wchunk(grid[0] - 1)),
                  pl.BlockSpec((bm, bn), _ws_tile()),
                  pl.BlockSpec((ms, bn), _ws_stile())],
        out_specs=[pl.BlockSpec((bm, bn), _ws_tile()),
                   pl.BlockSpec((ms, bn), _ws_stile())],
        out_shape=[jax.ShapeDtypeStruct((mp, n), F32),
                   jax.ShapeDtypeStruct((ms, n), F32)],
        scratch_shapes=[pltpu.VMEM((k, bn), BF16), pltpu.VMEM((k, bn), BF16)],
        compiler_params=_cparams("arbitrary", "arbitrary"),
        name=name,
    )(xp, xs, w, rp, rs)


def _lane_partials(v):
    part = v[:, 0:LANES]
    for c in range(1, v.shape[1] // LANES):
        part = part + v[:, c * LANES:(c + 1) * LANES]
    return part


def _outproj_kernel(xp_ref, xs_ref, wf_ref, rp_ref, rs_ref, g_ref,
                    op_ref, os_ref, bp_ref, bs_ref, ssp_ref, sss_ref,
                    w0_ref, w1_ref, accp_ref, accs_ref):
    j = pl.program_id(0)
    i = pl.program_id(1)
    bm = xp_ref.shape[0]

    def compute(slot):
        (w_ref,) = slot

        def tile(x_ref, r_ref, o_ref, b_ref, acc_view, ss_ref):
            x1 = r_ref[...] + _dot(x_ref[...], w_ref[...])
            o_ref[...] = x1
            b_ref[...] = (x1 * g_ref[...]).astype(b_ref.dtype)
            total = jnp.where(j == 1, 0.0, acc_view[...]) + _lane_partials(x1 * x1)
            acc_view[...] = total
            ss_ref[...] = total

        rows = pl.ds(pl.multiple_of(i * bm, bm), bm)
        tile(xp_ref, rp_ref, op_ref, bp_ref, accp_ref.at[rows, :], ssp_ref)

        @pl.when(_is_last_row_tile())
        def _():
            tile(xs_ref, rs_ref, os_ref, bs_ref, accs_ref, sss_ref)

    _ws_phases([wf_ref], [w0_ref], [w1_ref], compute)


def _out_proj(xp, xs, w, rp, rs, gain, bm, bn):
    mp, k = xp.shape
    ms = xs.shape[0]
    n = w.shape[1]
    grid = _ws_grid(n, bn, mp, bm)
    last = grid[0] - 1
    ssp_map = lambda j, i: (jnp.where(j == last, i, 0), 0)
    return pl.pallas_call(
        _outproj_kernel,
        grid=grid,
        in_specs=[pl.BlockSpec((bm, k), _ws_rows),
                  pl.BlockSpec((ms, k), lambda j, i: (0, 0)),
                  pl.BlockSpec((k // grid[1], bn), _ws_wchunk(grid[0] - 1)),
                  pl.BlockSpec((bm, bn), _ws_tile()),
                  pl.BlockSpec((ms, bn), _ws_stile()),
                  pl.BlockSpec((1, bn), _ws_stile())],
        out_specs=[pl.BlockSpec((bm, bn), _ws_tile()),
                   pl.BlockSpec((ms, bn), _ws_stile()),
                   pl.BlockSpec((bm, bn), _ws_tile()),
                   pl.BlockSpec((ms, bn), _ws_stile()),
                   pl.BlockSpec((bm, LANES), ssp_map),
                   pl.BlockSpec((ms, LANES), lambda j, i: (0, 0))],
        out_shape=[jax.ShapeDtypeStruct((mp, n), F32),
                   jax.ShapeDtypeStruct((ms, n), F32),
                   jax.ShapeDtypeStruct((mp, n), BF16),
                   jax.ShapeDtypeStruct((ms, n), BF16),
                   jax.ShapeDtypeStruct((mp, LANES), F32),
                   jax.ShapeDtypeStruct((ms, LANES), F32)],
        scratch_shapes=[pltpu.VMEM((k, bn), BF16), pltpu.VMEM((k, bn), BF16),
                        pltpu.VMEM((mp, LANES), F32), pltpu.VMEM((ms, LANES), F32)],
        compiler_params=_cparams("arbitrary", "arbitrary"),
        name="out_proj",
    )(xp, xs, w, rp, rs, gain.reshape(1, n))


def _inv_rms(ss_ref):
    return lax.rsqrt(jnp.sum(ss_ref[...], axis=-1, keepdims=True) * (1.0 / D_MODEL) + EPS)


def _ffn_up_kernel(hp_ref, hs_ref, ssp_ref, sss_ref, wgf_ref, wuf_ref, cw_ref, cb_ref,
                   s0_ref, s1_ref, fp_ref, fs_ref, gs_ref, st_ref,
                   wg0_ref, wu0_ref, wg1_ref, wu1_ref, carry_ref, gbuf_ref,
                   *, tiles_per_seq, last_phase, last_width):
    i = pl.program_id(1)
    bm = hp_ref.shape[0]

    def compute(slot, width):
        wg_ref, wu_ref = slot
        cols = slice(0, width)
        w0 = cw_ref[0:1, cols]
        w1 = cw_ref[1:2, cols]
        w2 = cw_ref[2:3, cols]
        bias = cb_ref[:, cols]

        @pl.when(i % tiles_per_seq == 0)
        def _():
            carry_ref[...] = jnp.zeros(carry_ref.shape, F32)

        r = _inv_rms(ssp_ref)
        g = r * _dot(hp_ref[...], wg_ref[:, cols])
        gbuf_ref[0:SUBLANES, cols] = carry_ref[:, cols]
        gbuf_ref[SUBLANES:SUBLANES + bm, cols] = g
        g1 = gbuf_ref[SUBLANES - 1:SUBLANES - 1 + bm, cols]
        g2 = gbuf_ref[SUBLANES - 2:SUBLANES - 2 + bm, cols]
        gc = w0 * g2 + w1 * g1 + w2 * g + bias
        up = r * _dot(hp_ref[...], wu_ref[:, cols])
        fp_ref[:, cols] = (gc * _sigmoid(gc) * up).astype(fp_ref.dtype)
        tail = gbuf_ref[bm:SUBLANES + bm, cols]
        carry_ref[:, cols] = tail
        st_ref[0, :, cols] = tail

        @pl.when(_is_last_row_tile())
        def _():
            rs = _inv_rms(sss_ref)
            gsm = rs * _dot(hs_ref[...], wg_ref[:, cols])
            ups = rs * _dot(hs_ref[...], wu_ref[:, cols])
            gcs = w0 * s0_ref[:, cols] + w1 * s1_ref[:, cols] + w2 * gsm + bias
            fs_ref[:, cols] = (gcs * _sigmoid(gcs) * ups).astype(fs_ref.dtype)
            gs_ref[:, cols] = gsm

    full_width = fp_ref.shape[1]
    _ws_phases([wgf_ref, wuf_ref], [wg0_ref, wu0_ref], [wg1_ref, wu1_ref],
               functools.partial(compute, width=full_width),
               functools.partial(compute, width=last_width) if last_width < full_width else None,
               last_phase)


def _ffn_up(hp, hs, ssp, sss, wg, wu, cw, cb, s0, s1, bm, bn):
    mp, k = hp.shape
    ms = hs.shape[0]
    n = wg.shape[1]
    grid = _ws_grid(n, bn, mp, bm)
    tiles_per_seq = SEQ // bm
    wchunk = pl.BlockSpec((k // grid[1], bn), _ws_wchunk(grid[0] - 1))
    wslot = pltpu.VMEM((k, bn), BF16)
    col = _ws_stile()

    def tail_map(j, i):
        return (jnp.where(j > 0, i, 0) // tiles_per_seq, 0, jnp.maximum(j - 1, 0))

    last_phase = grid[0] - 1
    last_width = n - (last_phase - 1) * bn
    return pl.pallas_call(
        functools.partial(_ffn_up_kernel, tiles_per_seq=tiles_per_seq,
                          last_phase=last_phase, last_width=last_width),
        grid=grid,
        in_specs=[pl.BlockSpec((bm, k), _ws_rows),
                  pl.BlockSpec((ms, k), lambda j, i: (0, 0)),
                  pl.BlockSpec((bm, LANES), _ws_rows),
                  pl.BlockSpec((ms, LANES), lambda j, i: (0, 0)),
                  wchunk, wchunk,
                  pl.BlockSpec((FFN_CONV_K, bn), col),
                  pl.BlockSpec((1, bn), col),
                  pl.BlockSpec((ms, bn), col),
                  pl.BlockSpec((ms, bn), col)],
        out_specs=[pl.BlockSpec((bm, bn), _ws_tile()),
                   pl.BlockSpec((ms, bn), col),
                   pl.BlockSpec((ms, bn), col),
                   pl.BlockSpec((1, SUBLANES, bn), tail_map)],
        out_shape=[jax.ShapeDtypeStruct((mp, n), BF16),
                   jax.ShapeDtypeStruct((ms, n), BF16),
                   jax.ShapeDtypeStruct((ms, n), F32),
                   jax.ShapeDtypeStruct((BATCH, SUBLANES, n), F32)],
        scratch_shapes=[wslot, wslot, wslot, wslot,
                        pltpu.VMEM((SUBLANES, bn), F32),
                        pltpu.VMEM((SUBLANES + bm, bn), F32)],
        compiler_params=_cparams("arbitrary", "arbitrary"),
        name="ffn_up",
    )(hp, hs, ssp, sss, wg, wu, cw, cb, s0, s1)


def _window_to_kernel(state):
    return jnp.transpose(state[0], (0, 2, 3, 1)).reshape(DEC_BATCH, KV_W, WINDOW)


def _window_from_kernel(win):
    win = win.reshape(DEC_BATCH, N_KV_HEADS, HEAD_DIM, WINDOW)
    return jnp.transpose(win, (0, 3, 1, 2))[None]


def kernel(x_prompt, x_sample, state_k_window, state_v_window, state_conv, state_ffn_conv,
           attn_norm_g, w_in, conv_w, w_branch_a, w_branch_b, sinks, w_out, ffn_norm_g,
           w_ffn_gate, w_ffn_up, ffn_conv_w, ffn_conv_b, w_ffn_down, rel_bias,
           final_norm_g):
    xp = x_prompt.reshape(M_PROMPT, D_MODEL)
    xs = x_sample.reshape(DEC_BATCH, D_MODEL)

    fs0 = state_ffn_conv[0][:, 0, :]
    fs1 = state_ffn_conv[0][:, 1, :]

    cs0 = state_conv[0][:, 0, :]
    cs1 = state_conv[0][:, 1, :]
    hs = _rmsnorm(xs, attn_norm_g[0], BF16, DEC_BATCH)
    a_p, a_s, u_s, u_tail, hp, hssp = _mixer_a(
        xp, hs, attn_norm_g[0], w_in[0], conv_w[0], cs0, cs1, *MIXER_TILE)
    proj_p, proj_s = _in_proj(hp, hs, hssp, w_in[0], M_PROMPT, W_REST, PROJ_W, *PROJ_TILE)

    bias_tab = _bias_table(rel_bias)
    o_p = _attn_prompt(proj_p, bias_tab, sinks[0])
    q3 = proj_s[:, OFF_Q:OFF_Q + ATTN_W].reshape(DEC_BATCH, N_HEADS, HEAD_DIM)
    k_new = proj_s[:, OFF_K:OFF_K + KV_W].astype(F32)
    v_new = proj_s[:, OFF_V:OFF_V + KV_W].astype(F32)
    bias_s = bias_tab[0, :, Q_BLOCK - 1:, Q_BLOCK - 1]
    o_s3, k_win_s, v_win_s = _attn_sample(
        q3, k_new, v_new,
        _window_to_kernel(state_k_window), _window_to_kernel(state_v_window),
        bias_s, sinks[0])
    o_s = o_s3.reshape(DEC_BATCH, ATTN_W)

    mg_p, mg_s = _branch_merge(a_p, o_p, a_s, o_s, w_branch_a[0], w_branch_b[0],
                               proj_p, proj_s, *MERGE_TILE)
    x1p, x1s, xbp, xbs, ssp, sss = _out_proj(mg_p, mg_s, w_out[0], xp, xs, ffn_norm_g[0],
                                             *OUT_TILE)

    f_p, f_s, g_s, g_tail = _ffn_up(xbp, xbs, ssp, sss, w_ffn_gate[0], w_ffn_up[0],
                                    ffn_conv_w[0], ffn_conv_b, fs0, fs1, *FFN_UP_TILE)
    x2p, x2s = _resid_matmul(f_p, f_s, w_ffn_down[0], x1p, x1s, *FFN_DOWN_TILE, "ffn_down")
    y_p = _rmsnorm(x2p, final_norm_g, F32, NORM_ROWS)
    y_s = _rmsnorm(x2s, final_norm_g, F32, DEC_BATCH)

    kv_p = proj_p.reshape(BATCH, SEQ, PROJ_W)[:, SEQ - WINDOW:, OFF_K:OFF_GA].astype(F32)
    k_win_p = kv_p[:, :, :KV_W].reshape(1, BATCH, WINDOW, N_KV_HEADS, HEAD_DIM)
    v_win_p = kv_p[:, :, KV_W:].reshape(1, BATCH, WINDOW, N_KV_HEADS, HEAD_DIM)
    conv_p = u_tail[:, SUBLANES - (CONV_K - 1):, :]
    ffn_p = g_tail[:, SUBLANES - (FFN_CONV_K - 1):, :]
    conv_s = jnp.stack([cs1, u_s], axis=1)
    ffn_s = jnp.stack([fs1, g_s], axis=1)

    return (y_p.reshape(BATCH, SEQ, D_MODEL),
            y_s.reshape(DEC_BATCH, 1, D_MODEL),
            k_win_p, v_win_p, conv_p[None], ffn_p[None],
            _window_from_kernel(k_win_s), _window_from_kernel(v_win_s),
            conv_s[None], ffn_s[None])
```

```python
import functools
import math

import numpy as np
import jax
import jax.numpy as jnp
from jax import lax
from jax.experimental import pallas as pl
from jax.experimental.pallas import tpu as pltpu

F32 = jnp.float32
BF16 = jnp.bfloat16

D_MODEL = 4096
BATCH = 4
SEQ = 2048
DEC_BATCH = 128
N_HEADS = 32
N_KV_HEADS = 8
HEAD_DIM = 64
GROUP = N_HEADS // N_KV_HEADS
ATTN_W = N_HEADS * HEAD_DIM
KV_W = N_KV_HEADS * HEAD_DIM
CONV_W = D_MODEL // 2
CONV_K = 3
WINDOW = 128
Q_BLOCK = 128
N_BUCKETS = 32
MAX_DISTANCE = 128
D_FF = 11008
FFN_CONV_K = 3
EPS = 1e-5
NEG = -1e30
M_PROMPT = BATCH * SEQ

W_CB = 0
W_CC = CONV_W
W_CH = 2 * CONV_W
W_REST = 3 * CONV_W
OFF_Q = 0
OFF_K = OFF_Q + ATTN_W
OFF_V = OFF_K + KV_W
OFF_GA = OFF_V + KV_W
OFF_GB = OFF_GA + D_MODEL
PROJ_W = OFF_GB + D_MODEL

V7X_VMEM_BYTES = 64 * 1024 * 1024
VMEM_LIMIT = V7X_VMEM_BYTES - 1024 * 1024
SUBLANES = 8
LANES = 128

MIXER_TILE = (512, 512)
PROJ_TILE = (1024, 1024)
MERGE_TILE = (1024, 1024)
OUT_TILE = (512, 1024)
FFN_UP_TILE = (1024, 512)
FFN_DOWN_TILE = (512, 512)
NORM_ROWS = 512


def _cparams(*sem):
    return pltpu.CompilerParams(dimension_semantics=sem, vmem_limit_bytes=VMEM_LIMIT)


def _dot(a, b):
    return jnp.dot(a, b, preferred_element_type=F32)


def _sigmoid(x):
    return 1.0 / (1.0 + jnp.exp(-x))


def _ws_grid(n, bn, mp, bm):
    return (pl.cdiv(n, bn) + 1, mp // bm)


def _ws_rows(j, i):
    return (jnp.where(j > 0, i, 0), 0)


def _ws_wchunk(nj, col0=0):
    return lambda j, i: (i, col0 + jnp.minimum(j, nj - 1))


def _ws_tile(col0=0):
    return lambda j, i: (jnp.where(j > 0, i, 0), col0 + jnp.maximum(j - 1, 0))


def _ws_stile(col0=0):
    return lambda j, i: (0, col0 + jnp.maximum(j - 1, 0))


def _ws_cast(wf_refs, dst_refs):
    i = pl.program_id(1)
    for wf_ref, dst_ref in zip(wf_refs, dst_refs):
        rows = wf_ref.shape[0]
        r0 = pl.multiple_of(i * rows, rows)
        dst_ref[pl.ds(r0, rows), :] = wf_ref[...].astype(BF16)


def _ws_phases(wf_refs, slot0, slot1, compute, compute_last=None, last=None):
    j = pl.program_id(0)
    regular = (j > 0) if compute_last is None else ((j > 0) & (j < last))

    @pl.when(j == 0)
    def _():
        _ws_cast(wf_refs, slot0)

    @pl.when(regular & (j % 2 == 1))
    def _():
        _ws_cast(wf_refs, slot1)
        compute(slot0)

    @pl.when(regular & (j % 2 == 0))
    def _():
        _ws_cast(wf_refs, slot0)
        compute(slot1)

    if compute_last is not None:
        @pl.when(j == last)
        def _():
            compute_last(slot0 if last % 2 == 1 else slot1)


def _is_last_row_tile():
    return pl.program_id(1) == pl.num_programs(1) - 1


def _rms_kernel(x_ref, g_ref, o_ref):
    x = x_ref[...]
    r = lax.rsqrt(jnp.mean(x * x, axis=-1, keepdims=True) + EPS)
    o_ref[...] = ((x * r) * g_ref[...]).astype(o_ref.dtype)


def _rmsnorm(x, g, out_dtype, block_rows):
    rows, d = x.shape
    return pl.pallas_call(
        _rms_kernel,
        grid=(rows // block_rows,),
        in_specs=[pl.BlockSpec((block_rows, d), lambda i: (i, 0)),
                  pl.BlockSpec((1, d), lambda i: (0, 0))],
        out_specs=pl.BlockSpec((block_rows, d), lambda i: (i, 0)),
        out_shape=jax.ShapeDtypeStruct((rows, d), out_dtype),
        compiler_params=_cparams("parallel"),
        name="rmsnorm",
    )(x, g.reshape(1, d))


def _proj_kernel(xp_ref, xs_ref, ssp_ref, wf_ref, op_ref, os_ref, w0_ref, w1_ref):
    def compute(slot):
        (w_ref,) = slot
        op_ref[...] = (_inv_rms(ssp_ref) * _dot(xp_ref[...], w_ref[...])).astype(op_ref.dtype)

        @pl.when(_is_last_row_tile())
        def _():
            os_ref[...] = _dot(xs_ref[...], w_ref[...]).astype(os_ref.dtype)

    _ws_phases([wf_ref], [w0_ref], [w1_ref], compute)


def _in_proj(hp, hs, ssp, w, mp, col0, n, bm, bn):
    k = hp.shape[1]
    ms = hs.shape[0]
    grid = _ws_grid(n, bn, mp, bm)
    return pl.pallas_call(
        _proj_kernel,
        grid=grid,
        in_specs=[pl.BlockSpec((bm, k), _ws_rows),
                  pl.BlockSpec((ms, k), lambda j, i: (0, 0)),
                  pl.BlockSpec((bm, LANES), _ws_rows),
                  pl.BlockSpec((k // grid[1], bn), _ws_wchunk(grid[0] - 1, col0 // bn))],
        out_specs=[pl.BlockSpec((bm, bn), _ws_tile()),
                   pl.BlockSpec((ms, bn), _ws_stile())],
        out_shape=[jax.ShapeDtypeStruct((mp, n), BF16),
                   jax.ShapeDtypeStruct((ms, n), BF16)],
        scratch_shapes=[pltpu.VMEM((k, bn), BF16), pltpu.VMEM((k, bn), BF16)],
        compiler_params=_cparams("arbitrary", "arbitrary"),
        name="in_proj",
    )(hp, hs, ssp, w)


def _mixa_kernel(xp_ref, hs_ref, g_ref, wbf_ref, wcf_ref, whf_ref, cw_ref, s0_ref, s1_ref,
                 ap_ref, as_ref, us_ref, st_ref, hp_ref, ssp_ref,
                 wb0_ref, wc0_ref, wh0_ref, wb1_ref, wc1_ref, wh1_ref, carry_ref, ubuf_ref,
                 *, tiles_per_seq):
    i = pl.program_id(1)
    bm = xp_ref.shape[0]

    def compute(slot):
        wb_ref, wc_ref, wh_ref = slot
        w0 = cw_ref[0:1, :]
        w1 = cw_ref[1:2, :]
        w2 = cw_ref[2:3, :]

        @pl.when(i % tiles_per_seq == 0)
        def _():
            carry_ref[...] = jnp.zeros(carry_ref.shape, F32)

        hp_ref[...] = (xp_ref[...] * g_ref[...]).astype(hp_ref.dtype)
        ssp_ref[...] = _lane_partials(xp_ref[...] * xp_ref[...])
        r = _inv_rms(ssp_ref)
        u = (r * _dot(hp_ref[...], wc_ref[...])) * (r * _dot(hp_ref[...], wh_ref[...]))
        cb = r * _dot(hp_ref[...], wb_ref[...])
        ubuf_ref[0:SUBLANES, :] = carry_ref[...]
        ubuf_ref[SUBLANES:SUBLANES + bm, :] = u
        u1 = ubuf_ref[SUBLANES - 1:SUBLANES - 1 + bm, :]
        u2 = ubuf_ref[SUBLANES - 2:SUBLANES - 2 + bm, :]
        z = w0 * u2 + w1 * u1 + w2 * u
        ap_ref[...] = (cb * z).astype(ap_ref.dtype)
        tail = ubuf_ref[bm:SUBLANES + bm, :]
        carry_ref[...] = tail
        st_ref[0] = tail

        @pl.when(_is_last_row_tile())
        def _():
            us = _dot(hs_ref[...], wc_ref[...]) * _dot(hs_ref[...], wh_ref[...])
            zs = w0 * s0_ref[...] + w1 * s1_ref[...] + w2 * us
            as_ref[...] = (_dot(hs_ref[...], wb_ref[...]) * zs).astype(as_ref.dtype)
            us_ref[...] = us

    _ws_phases([wbf_ref, wcf_ref, whf_ref], [wb0_ref, wc0_ref, wh0_ref],
               [wb1_ref, wc1_ref, wh1_ref], compute)


def _mixer_a(xp, hs, gain, w_in, conv_w, s0, s1, bm, bn):
    mp, k = xp.shape
    ms = hs.shape[0]
    n = CONV_W
    grid = _ws_grid(n, bn, mp, bm)
    nj = grid[0] - 1
    tiles_per_seq = SEQ // bm
    wslot = pltpu.VMEM((k, bn), BF16)
    col = _ws_stile()

    def wchunk(col0):
        return pl.BlockSpec((k // grid[1], bn), _ws_wchunk(nj, col0 // bn))

    def tail_map(j, i):
        return (jnp.where(j > 0, i, 0) // tiles_per_seq, 0, jnp.maximum(j - 1, 0))

    spare = grid[1]

    def operand_map(j, i):
        return (jnp.where(j == 1, i, jnp.where(j == 0, 0, spare)), 0)

    return pl.pallas_call(
        functools.partial(_mixa_kernel, tiles_per_seq=tiles_per_seq),
        grid=grid,
        in_specs=[pl.BlockSpec((bm, k), _ws_rows),
                  pl.BlockSpec((ms, k), lambda j, i: (0, 0)),
                  pl.BlockSpec((1, k), lambda j, i: (0, 0)),
                  wchunk(W_CB), wchunk(W_CC), wchunk(W_CH),
                  pl.BlockSpec((CONV_K, bn), col),
                  pl.BlockSpec((ms, bn), col),
                  pl.BlockSpec((ms, bn), col)],
        out_specs=[pl.BlockSpec((bm, bn), _ws_tile()),
                   pl.BlockSpec((ms, bn), col),
                   pl.BlockSpec((ms, bn), col),
                   pl.BlockSpec((1, SUBLANES, bn), tail_map),
                   pl.BlockSpec((bm, k), operand_map),
                   pl.BlockSpec((bm, LANES), operand_map)],
        out_shape=[jax.ShapeDtypeStruct((mp, n), BF16),
                   jax.ShapeDtypeStruct((ms, n), BF16),
                   jax.ShapeDtypeStruct((ms, n), F32),
                   jax.ShapeDtypeStruct((BATCH, SUBLANES, n), F32),
                   jax.ShapeDtypeStruct((mp + bm, k), BF16),
                   jax.ShapeDtypeStruct((mp + bm, LANES), F32)],
        scratch_shapes=[wslot] * 6 + [pltpu.VMEM((SUBLANES, bn), F32),
                                      pltpu.VMEM((SUBLANES + bm, bn), F32)],
        compiler_params=_cparams("arbitrary", "arbitrary"),
        name="mixer_a",
    )(xp, hs, gain.reshape(1, k), w_in, w_in, w_in, conv_w, s0, s1)


def _bucket_table():
    qi = np.arange(Q_BLOCK)[None, :]
    kj = np.arange(2 * Q_BLOCK)[:, None]
    dist = qi + Q_BLOCK - kj
    max_exact = N_BUCKETS // 2
    d = np.maximum(dist, 0)
    df = np.maximum(d, 1).astype(np.float32)
    large = max_exact + (np.log(df / np.float32(max_exact))
                         / np.float32(math.log(MAX_DISTANCE / max_exact))
                         * np.float32(N_BUCKETS - max_exact)).astype(np.int32)
    large = np.minimum(large, N_BUCKETS - 1)
    bucket = np.where(d < max_exact, d, large).astype(np.int32)
    valid = ((dist >= 0) & (dist <= WINDOW)).astype(np.int32)
    return bucket, valid


def _bias_kernel(rb_ref, bucket_ref, valid_ref, o_ref):
    bucket = bucket_ref[...]
    row = lax.broadcasted_iota(jnp.int32, bucket.shape, 0)
    keep = valid_ref[...] != 0
    keep_first = keep & (row >= Q_BLOCK)

    def head(h, carry):
        acc = jnp.zeros(bucket.shape, F32)
        for b in range(N_BUCKETS):
            acc = jnp.where(bucket == b, rb_ref[b, h], acc)
        o_ref[0, h] = jnp.where(keep, acc, NEG)
        o_ref[1, h] = jnp.where(keep_first, acc, NEG)
        return carry

    lax.fori_loop(0, N_HEADS, head, 0)


def _bias_table(rel_bias):
    bucket, valid = _bucket_table()
    shp = (2 * Q_BLOCK, Q_BLOCK)
    return pl.pallas_call(
        _bias_kernel,
        grid=(1,),
        in_specs=[pl.BlockSpec(memory_space=pltpu.SMEM),
                  pl.BlockSpec(shp, lambda v: (0, 0)),
                  pl.BlockSpec(shp, lambda v: (0, 0))],
        out_specs=pl.BlockSpec((2, N_HEADS) + shp, lambda v: (0, 0, 0, 0)),
        out_shape=jax.ShapeDtypeStruct((2, N_HEADS) + shp, F32),
        compiler_params=_cparams("arbitrary"),
        name="rel_bias_table",
    )(rel_bias, jnp.asarray(bucket), jnp.asarray(valid))


BLOCKS_PER_STEP = 2


def _attn_prompt_kernel(sink_ref, q_ref, kvp_ref, kvc_ref, bias_ref, o_ref):
    scale = HEAD_DIM ** -0.5
    first_variant = jnp.where(pl.program_id(1) == 0, 1, 0)

    def band(blk, col):
        cols = slice(col, col + HEAD_DIM)
        cur = kvc_ref[blk * Q_BLOCK:(blk + 1) * Q_BLOCK, cols]
        if blk == 0:
            return jnp.concatenate([kvp_ref[:, cols], cur], axis=0)
        return kvc_ref[(blk - 1) * Q_BLOCK:(blk + 1) * Q_BLOCK, cols]

    def scores(blk, kv):
        heads = range(kv * GROUP, (kv + 1) * GROUP)
        rows = slice(blk * Q_BLOCK, (blk + 1) * Q_BLOCK)
        variant = first_variant if blk == 0 else 0
        qg = jnp.concatenate(
            [q_ref[rows, h * HEAD_DIM:(h + 1) * HEAD_DIM] for h in heads], axis=0) * scale
        st = lax.dot_general(band(blk, kv * HEAD_DIM), qg, (((1,), (1,)), ((), ())),
                             preferred_element_type=F32)
        return st + jnp.concatenate([bias_ref[variant, h] for h in heads], axis=1)

    def finish(blk, kv, st):
        heads = range(kv * GROUP, (kv + 1) * GROUP)
        vband = band(blk, KV_W + kv * HEAD_DIM)
        sk = jnp.concatenate([jnp.full((1, Q_BLOCK), sink_ref[h], F32) for h in heads], axis=1)
        m = jnp.maximum(jnp.max(st, axis=0, keepdims=True), sk)
        e = jnp.exp(st - m)
        denom = jnp.sum(e, axis=0, keepdims=True) + jnp.exp(sk - m)
        ot = lax.dot_general(vband, e.astype(BF16), (((0,), (0,)), ((), ())),
                             preferred_element_type=F32) * (1.0 / denom)
        o2 = jnp.concatenate(
            [ot[:, g * Q_BLOCK:(g + 1) * Q_BLOCK] for g in range(GROUP)], axis=0)
        rows = slice(blk * Q_BLOCK, (blk + 1) * Q_BLOCK)
        gs = slice(kv * GROUP * HEAD_DIM, (kv + 1) * GROUP * HEAD_DIM)
        o_ref[rows, gs] = o2.T.astype(o_ref.dtype)

    groups =[(blk, kv) for blk in range(BLOCKS_PER_STEP) for kv in range(N_KV_HEADS)]
    pairs = [groups[p:p + 2] for p in range(0, len(groups), 2)]
    sts = [scores(*grp) for grp in pairs[0]]
    for idx, pair in enumerate(pairs):
        sts_next = [scores(*grp) for grp in pairs[idx + 1]] if idx + 1 < len(pairs) else None
        for grp, st in zip(pair, sts):
            finish(*grp, st)
        sts = sts_next


def _attn_prompt(proj_p, bias_tab, sinks):
    rows = BLOCKS_PER_STEP * Q_BLOCK
    steps = SEQ // rows
    qcol = OFF_Q // ATTN_W
    kvcol = OFF_K // (2 * KV_W)

    def prev(b, n):
        return b * (SEQ // Q_BLOCK) + jnp.maximum(n * BLOCKS_PER_STEP - 1, 0)

    return pl.pallas_call(
        _attn_prompt_kernel,
        grid=(BATCH, steps),
        in_specs=[pl.BlockSpec(memory_space=pltpu.SMEM),
                  pl.BlockSpec((rows, ATTN_W), lambda b, n: (b * steps + n, qcol)),
                  pl.BlockSpec((Q_BLOCK, 2 * KV_W), lambda b, n: (prev(b, n), kvcol)),
                  pl.BlockSpec((rows, 2 * KV_W), lambda b, n: (b * steps + n, kvcol)),
                  pl.BlockSpec((2, N_HEADS, 2 * Q_BLOCK, Q_BLOCK), lambda b, n: (0, 0, 0, 0))],
        out_specs=pl.BlockSpec((rows, ATTN_W), lambda b, n: (b * steps + n, 0)),
        out_shape=jax.ShapeDtypeStruct((M_PROMPT, ATTN_W), BF16),
        compiler_params=_cparams("parallel", "arbitrary"),
        name="attn_prompt",
    )(sinks, proj_p, proj_p, proj_p, bias_tab)


SAMPLES_PER_STEP = 8


def _attn_sample_kernel(q_ref, kn_ref, vn_ref, kt_ref, vt_ref, bias_ref, sink_ref,
                        o_ref, kto_ref, vto_ref):
    g = SAMPLES_PER_STEP
    scale = HEAD_DIM ** -0.5
    erow = lax.broadcasted_iota(jnp.int32, (HEAD_DIM, KV_W), 0)
    ecol = lax.broadcasted_iota(jnp.int32, (HEAD_DIM, KV_W), 1)
    expand = (ecol % HEAD_DIM == erow).astype(BF16)
    hrow = lax.broadcasted_iota(jnp.int32, (N_HEADS, KV_W), 0)
    hcol = lax.broadcasted_iota(jnp.int32, (N_HEADS, KV_W), 1)
    own = ((hrow // GROUP) == (hcol // HEAD_DIM))[None]
    bias_w = bias_ref[:, 0:WINDOW][None]
    bias_n = bias_ref[:, WINDOW:WINDOW + 1][None]
    sk = sink_ref[...][None]
    contract_last = (((1,), (1,)), ((), ()))

    q_all = q_ref[...].reshape(g * N_HEADS, HEAD_DIM) * scale
    qrow = jnp.where(own, _dot(q_all, expand).reshape(g, N_HEADS, KV_W), 0.0)
    qrow_b = qrow.astype(BF16)
    kn = kn_ref[...][:, None, :]
    vn = vn_ref[...][:, None, :]
    s_w = jnp.stack([_dot(qrow_b[b], kt_ref[b].astype(BF16)) for b in range(g)])
    s_w = s_w + bias_w
    s_n = jnp.sum(qrow * kn, axis=-1, keepdims=True) + bias_n
    m = jnp.maximum(jnp.maximum(jnp.max(s_w, axis=-1, keepdims=True), s_n), sk)
    e_w = jnp.exp(s_w - m)
    e_n = jnp.exp(s_n - m)
    r = 1.0 / (jnp.sum(e_w, axis=-1, keepdims=True) + e_n + jnp.exp(sk - m))
    p_w = (e_w * r).astype(BF16)
    o_all = jnp.stack([lax.dot_general(p_w[b], vt_ref[b].astype(BF16), contract_last,
                                       preferred_element_type=F32) for b in range(g)])
    o_own = jnp.where(own, o_all + (e_n * r) * vn, 0.0)
    o = o_own[:, :, 0:HEAD_DIM]
    for kv in range(1, N_KV_HEADS):
        o = o + o_own[:, :, kv * HEAD_DIM:(kv + 1) * HEAD_DIM]
    o_ref[...] = o.astype(o_ref.dtype)

    rows = jnp.concatenate([kn_ref[...], vn_ref[...]], axis=0).astype(BF16)
    place = (lax.broadcasted_iota(jnp.int32, (2 * g, 2 * g * WINDOW), 1)
             == WINDOW * lax.broadcasted_iota(jnp.int32, (2 * g, 2 * g * WINDOW), 0)
             ).astype(BF16)
    newcols = lax.dot_general(rows, place, (((0,), (0,)), ((), ())),
                              preferred_element_type=F32)
    first = lax.broadcasted_iota(jnp.int32, (KV_W, WINDOW), 1) == 0
    for b in range(g):
        kcol = newcols[:, b * WINDOW:(b + 1) * WINDOW]
        vcol = newcols[:, (g + b) * WINDOW:(g + b + 1) * WINDOW]
        kto_ref[b] = pltpu.roll(jnp.where(first, kcol, kt_ref[b]), WINDOW - 1, 1)
        vto_ref[b] = pltpu.roll(jnp.where(first, vcol, vt_ref[b]), WINDOW - 1, 1)


def _attn_sample(q3, k_new, v_new, k_buf, v_buf, bias_s, sinks):
    g = SAMPLES_PER_STEP
    nb = DEC_BATCH // g
    win = pl.BlockSpec((g, KV_W, WINDOW), lambda i: (i, 0, 0))
    row = pl.BlockSpec((g, KV_W), lambda i: (i, 0))
    return pl.pallas_call(
        _attn_sample_kernel,
        grid=(nb,),
        in_specs=[pl.BlockSpec((g, N_HEADS, HEAD_DIM), lambda i: (i, 0, 0)),
                  row, row, win, win,
                  pl.BlockSpec((N_HEADS, WINDOW + 1), lambda i: (0, 0)),
                  pl.BlockSpec((N_HEADS, 1), lambda i: (0, 0))],
        out_specs=[pl.BlockSpec((g, N_HEADS, HEAD_DIM), lambda i: (i, 0, 0)), win, win],
        out_shape=[jax.ShapeDtypeStruct((DEC_BATCH, N_HEADS, HEAD_DIM), BF16),
                   jax.ShapeDtypeStruct((DEC_BATCH, KV_W, WINDOW), F32),
                   jax.ShapeDtypeStruct((DEC_BATCH, KV_W, WINDOW), F32)],
        compiler_params=_cparams("parallel"),
        name="attn_sample",
    )(q3, k_new, v_new, k_buf, v_buf, bias_s, sinks.reshape(N_HEADS, 1))


def _merge_kernel(ap_ref, bp_ref, as_ref, bs_ref, waf_ref, wbf_ref,
                  gap_ref, gbp_ref, gas_ref, gbs_ref, op_ref, os_ref,
                  wa0_ref, wb0_ref, wa1_ref, wb1_ref):
    def compute(slot):
        wa_ref, wb_ref = slot

        def merged(a_ref, b_ref, ga_ref, gb_ref):
            return (_sigmoid(ga_ref[...].astype(F32)) * _dot(a_ref[...], wa_ref[...])
                    + _sigmoid(gb_ref[...].astype(F32)) * _dot(b_ref[...], wb_ref[...]))

        op_ref[...] = merged(ap_ref, bp_ref, gap_ref, gbp_ref).astype(op_ref.dtype)

        @pl.when(_is_last_row_tile())
        def _():
            os_ref[...] = merged(as_ref, bs_ref, gas_ref, gbs_ref).astype(os_ref.dtype)

    _ws_phases([waf_ref, wbf_ref], [wa0_ref, wb0_ref], [wa1_ref, wb1_ref], compute)


def _branch_merge(a_p, o_p, a_s, o_s, wa, wb, proj_p, proj_s, bm, bn):
    mp, k = a_p.shape
    ms = a_s.shape[0]
    n = wa.shape[1]
    grid = _ws_grid(n, bn, mp, bm)
    ga0 = OFF_GA // bn
    gb0 = OFF_GB // bn
    wchunk = pl.BlockSpec((k // grid[1], bn), _ws_wchunk(grid[0] - 1))
    wslot = pltpu.VMEM((k, bn), BF16)
    return pl.pallas_call(
        _merge_kernel,
        grid=grid,
        in_specs=[pl.BlockSpec((bm, k), _ws_rows),
                  pl.BlockSpec((bm, k), _ws_rows),
                  pl.BlockSpec((ms, k), lambda j, i: (0, 0)),
                  pl.BlockSpec((ms, k), lambda j, i: (0, 0)),
                  wchunk, wchunk,
                  pl.BlockSpec((bm, bn), _ws_tile(ga0)),
                  pl.BlockSpec((bm, bn), _ws_tile(gb0)),
                  pl.BlockSpec((ms, bn), _ws_stile(ga0)),
                  pl.BlockSpec((ms, bn), _ws_stile(gb0))],
        out_specs=[pl.BlockSpec((bm, bn), _ws_tile()),
                   pl.BlockSpec((ms, bn), _ws_stile())],
        out_shape=[jax.ShapeDtypeStruct((mp, n), BF16),
                   jax.ShapeDtypeStruct((ms, n), BF16)],
        scratch_shapes=[wslot, wslot, wslot, wslot],
        compiler_params=_cparams("arbitrary", "arbitrary"),
        name="branch_merge",
    )(a_p, o_p, a_s, o_s, wa, wb, proj_p, proj_p, proj_s, proj_s)


def _resid_kernel(xp_ref, xs_ref, wf_ref, rp_ref, rs_ref, op_ref, os_ref, w0_ref, w1_ref):
    def compute(slot):
        (w_ref,) = slot
        op_ref[...] = rp_ref[...] + _dot(xp_ref[...], w_ref[...])

        @pl.when(_is_last_row_tile())
        def _():
            os_ref[...] = rs_ref[...] + _dot(xs_ref[...], w_ref[...])

    _ws_phases([wf_ref], [w0_ref], [w1_ref], compute)


def _resid_matmul(xp, xs, w, rp, rs, bm, bn, name):
    mp, k = xp.shape
    ms = xs.shape[0]
    n = w.shape[1]
    grid = _ws_grid(n, bn, mp, bm)
    return pl.pallas_call(
        _resid_kernel,
        grid=grid,
        in_specs=[pl.BlockSpec((bm, k), _ws_rows),
                  pl.BlockSpec((ms, k), lambda j, i: (0, 0)),
                  pl.BlockSpec((k // grid[1], bn), _ws_wchunk(grid[0] - 1)),
                  pl.BlockSpec((bm, bn), _ws_tile()),
                  pl.BlockSpec((ms, bn), _ws_stile())],
        out_specs=[pl.BlockSpec((bm, bn), _ws_tile()),
                   pl.BlockSpec((ms, bn), _ws_stile())],
        out_shape=[jax.ShapeDtypeStruct((mp, n), F32),
                   jax.ShapeDtypeStruct((ms, n), F32)],
        scratch_shapes=[pltpu.VMEM((k, bn), BF16), pltpu.VMEM((k, bn), BF16)],
        compiler_params=_cparams("arbitrary", "arbitrary"),
        name=name,
    )(xp, xs, w, rp, rs)


def _lane_partials(v):
    part = v[:, 0:LANES]
    for c in range(1, v.shape[1] // LANES):
        part = part + v[:, c * LANES:(c + 1) * LANES]
    return part


def _outproj_kernel(xp_ref, xs_ref, wf_ref, rp_ref, rs_ref, g_ref,
                    op_ref, os_ref, bp_ref, bs_ref, ssp_ref, sss_ref,
                    w0_ref, w1_ref, accp_ref, accs_ref):
    j = pl.program_id(0)
    i = pl.program_id(1)
    bm = xp_ref.shape[0]

    def compute(slot):
        (w_ref,) = slot

        def tile(x_ref, r_ref, o_ref, b_ref, acc_view, ss_ref):
            x1 = r_ref[...] + _dot(x_ref[...], w_ref[...])
            o_ref[...] = x1
            b_ref[...] = (x1 * g_ref[...]).astype(b_ref.dtype)
            total = jnp.where(j == 1, 0.0, acc_view[...]) + _lane_partials(x1 * x1)
            acc_view[...] = total
            ss_ref[...] = total

        rows = pl.ds(pl.multiple_of(i * bm, bm), bm)
        tile(xp_ref, rp_ref, op_ref, bp_ref, accp_ref.at[rows, :], ssp_ref)

        @pl.when(_is_last_row_tile())
        def _():
            tile(xs_ref, rs_ref, os_ref, bs_ref, accs_ref, sss_ref)

    _ws_phases([wf_ref], [w0_ref], [w1_ref], compute)


def _out_proj(xp, xs, w, rp, rs, gain, bm, bn):
    mp, k = xp.shape
    ms = xs.shape[0]
    n = w.shape[1]
    grid = _ws_grid(n, bn, mp, bm)
    last = grid[0] - 1
    ssp_map = lambda j, i: (jnp.where(j == last, i, 0), 0)
    return pl.pallas_call(
        _outproj_kernel,
        grid=grid,
        in_specs=[pl.BlockSpec((bm, k), _ws_rows),
                  pl.BlockSpec((ms, k), lambda j, i: (0, 0)),
                  pl.BlockSpec((k // grid[1], bn), _ws_wchunk(grid[0] - 1)),
                  pl.BlockSpec((bm, bn), _ws_tile()),
                  pl.BlockSpec((ms, bn), _ws_stile()),
                  pl.BlockSpec((1, bn), _ws_stile())],
        out_specs=[pl.BlockSpec((bm, bn), _ws_tile()),
                   pl.BlockSpec((ms, bn), _ws_stile()),
                   pl.BlockSpec((bm, bn), _ws_tile()),
                   pl.BlockSpec((ms, bn), _ws_stile()),
                   pl.BlockSpec((bm, LANES), ssp_map),
                   pl.BlockSpec((ms, LANES), lambda j, i: (0, 0))],
        out_shape=[jax.ShapeDtypeStruct((mp, n), F32),
                   jax.ShapeDtypeStruct((ms, n), F32),
                   jax.ShapeDtypeStruct((mp, n), BF16),
                   jax.ShapeDtypeStruct((ms, n), BF16),
                   jax.ShapeDtypeStruct((mp, LANES), F32),
                   jax.ShapeDtypeStruct((ms, LANES), F32)],
        scratch_shapes=[pltpu.VMEM((k, bn), BF16), pltpu.VMEM((k, bn), BF16),
                        pltpu.VMEM((mp, LANES), F32), pltpu.VMEM((ms, LANES), F32)],
        compiler_params=_cparams("arbitrary", "arbitrary"),
        name="out_proj",
    )(xp, xs, w, rp, rs, gain.reshape(1, n))


def _inv_rms(ss_ref):
    return lax.rsqrt(jnp.sum(ss_ref[...], axis=-1, keepdims=True) * (1.0 / D_MODEL) + EPS)


def _ffn_up_kernel(hp_ref, hs_ref, ssp_ref, sss_ref, wgf_ref, wuf_ref, cw_ref, cb_ref,
                   s0_ref, s1_ref, fp_ref, fs_ref, gs_ref, st_ref,
                   wg0_ref, wu0_ref, wg1_ref, wu1_ref, carry_ref, gbuf_ref,
                   *, tiles_per_seq, last_phase, last_width):
    i = pl.program_id(1)
    bm = hp_ref.shape[0]

    def compute(slot, width):
        wg_ref, wu_ref = slot
        cols = slice(0, width)
        w0 = cw_ref[0:1, cols]
        w1 = cw_ref[1:2, cols]
        w2 = cw_ref[2:3, cols]
        bias = cb_ref[:, cols]

        @pl.when(i % tiles_per_seq == 0)
        def _():
            carry_ref[...] = jnp.zeros(carry_ref.shape, F32)

        r = _inv_rms(ssp_ref)
        g = r * _dot(hp_ref[...], wg_ref[:, cols])
        gbuf_ref[0:SUBLANES, cols] = carry_ref[:, cols]
        gbuf_ref[SUBLANES:SUBLANES + bm, cols] = g
        g1 = gbuf_ref[SUBLANES - 1:SUBLANES - 1 + bm, cols]
        g2 = gbuf_ref[SUBLANES - 2:SUBLANES - 2 + bm, cols]
        gc = w0 * g2 + w1 * g1 + w2 * g + bias
        up = r * _dot(hp_ref[...], wu_ref[:, cols])
        fp_ref[:, cols] = (gc * _sigmoid(gc) * up).astype(fp_ref.dtype)
        tail = gbuf_ref[bm:SUBLANES + bm, cols]
        carry_ref[:, cols] = tail
        st_ref[0, :, cols] = tail

        @pl.when(_is_last_row_tile())
        def _():
            rs = _inv_rms(sss_ref)
            gsm = rs * _dot(hs_ref[...], wg_ref[:, cols])
            ups = rs * _dot(hs_ref[...], wu_ref[:, cols])
            gcs = w0 * s0_ref[:, cols] + w1 * s1_ref[:, cols] + w2 * gsm + bias
            fs_ref[:, cols] = (gcs * _sigmoid(gcs) * ups).astype(fs_ref.dtype)
            gs_ref[:, cols] = gsm

    full_width = fp_ref.shape[1]
    _ws_phases([wgf_ref, wuf_ref], [wg0_ref, wu0_ref], [wg1_ref, wu1_ref],
               functools.partial(compute, width=full_width),
               functools.partial(compute, width=last_width) if last_width < full_width else None,
               last_phase)


def _ffn_up(hp, hs, ssp, sss, wg, wu, cw, cb, s0, s1, bm, bn):
    mp, k = hp.shape
    ms = hs.shape[0]
    n = wg.shape[1]
    grid = _ws_grid(n, bn, mp, bm)
    tiles_per_seq = SEQ // bm
    wchunk = pl.BlockSpec((k // grid[1], bn), _ws_wchunk(grid[0] - 1))
    wslot = pltpu.VMEM((k, bn), BF16)
    col = _ws_stile()

    def tail_map(j, i):
        return (jnp.where(j > 0, i, 0) // tiles_per_seq, 0, jnp.maximum(j - 1, 0))

    last_phase = grid[0] - 1
    last_width = n - (last_phase - 1) * bn
    return pl.pallas_call(
        functools.partial(_ffn_up_kernel, tiles_per_seq=tiles_per_seq,
                          last_phase=last_phase, last_width=last_width),
        grid=grid,
        in_specs=[pl.BlockSpec((bm, k), _ws_rows),
                  pl.BlockSpec((ms, k), lambda j, i: (0, 0)),
                  pl.BlockSpec((bm, LANES), _ws_rows),
                  pl.BlockSpec((ms, LANES), lambda j, i: (0, 0)),
                  wchunk, wchunk,
                  pl.BlockSpec((FFN_CONV_K, bn), col),
                  pl.BlockSpec((1, bn), col),
                  pl.BlockSpec((ms, bn), col),
                  pl.BlockSpec((ms, bn), col)],
        out_specs=[pl.BlockSpec((bm, bn), _ws_tile()),
                   pl.BlockSpec((ms, bn), col),
                   pl.BlockSpec((ms, bn), col),
                   pl.BlockSpec((1, SUBLANES, bn), tail_map)],
        out_shape=[jax.ShapeDtypeStruct((mp, n), BF16),
                   jax.ShapeDtypeStruct((ms, n), BF16),
                   jax.ShapeDtypeStruct((ms, n), F32),
                   jax.ShapeDtypeStruct((BATCH, SUBLANES, n), F32)],
        scratch_shapes=[wslot, wslot, wslot, wslot,
                        pltpu.VMEM((SUBLANES, bn), F32),
                        pltpu.VMEM((SUBLANES + bm, bn), F32)],
        compiler_params=_cparams("arbitrary", "arbitrary"),
        name="ffn_up",
    )(hp, hs, ssp, sss, wg, wu, cw, cb, s0, s1)


def _window_to_kernel(state):
    return jnp.transpose(state[0], (0, 2, 3, 1)).reshape(DEC_BATCH, KV_W, WINDOW)


def _window_from_kernel(win):
    win = win.reshape(DEC_BATCH, N_KV_HEADS, HEAD_DIM, WINDOW)
    return jnp.transpose(win, (0, 3, 1, 2))[None]


def kernel(x_prompt, x_sample, state_k_window, state_v_window, state_conv, state_ffn_conv,
           attn_norm_g, w_in, conv_w, w_branch_a, w_branch_b, sinks, w_out, ffn_norm_g,
           w_ffn_gate, w_ffn_up, ffn_conv_w, ffn_conv_b, w_ffn_down, rel_bias,
           final_norm_g):
    xp = x_prompt.reshape(M_PROMPT, D_MODEL)
    xs = x_sample.reshape(DEC_BATCH, D_MODEL)

    fs0 = state_ffn_conv[0][:, 0, :]
    fs1 = state_ffn_conv[0][:, 1, :]

    cs0 = state_conv[0][:, 0, :]
    cs1 = state_conv[0][:, 1, :]
    hs = _rmsnorm(xs, attn_norm_g[0], BF16, DEC_BATCH)
    a_p, a_s, u_s, u_tail, hp, hssp = _mixer_a(
        xp, hs, attn_norm_g[0], w_in[0], conv_w[0], cs0, cs1, *MIXER_TILE)
    proj_p, proj_s = _in_proj(hp, hs, hssp, w_in[0], M_PROMPT, W_REST, PROJ_W, *PROJ_TILE)

    bias_tab = _bias_table(rel_bias)
    o_p = _attn_prompt(proj_p, bias_tab, sinks[0])
    q3 = proj_s[:, OFF_Q:OFF_Q + ATTN_W].reshape(DEC_BATCH, N_HEADS, HEAD_DIM)
    k_new = proj_s[:, OFF_K:OFF_K + KV_W].astype(F32)
    v_new = proj_s[:, OFF_V:OFF_V + KV_W].astype(F32)
    bias_s = bias_tab[0, :, Q_BLOCK - 1:, Q_BLOCK - 1]
    o_s3, k_win_s, v_win_s = _attn_sample(
        q3, k_new, v_new,
        _window_to_kernel(state_k_window), _window_to_kernel(state_v_window),
        bias_s, sinks[0])
    o_s = o_s3.reshape(DEC_BATCH, ATTN_W)

    mg_p, mg_s = _branch_merge(a_p, o_p, a_s, o_s, w_branch_a[0], w_branch_b[0],
                               proj_p, proj_s, *MERGE_TILE)
    x1p, x1s, xbp, xbs, ssp, sss = _out_proj(mg_p, mg_s, w_out[0], xp, xs, ffn_norm_g[0],
                                             *OUT_TILE)

    f_p, f_s, g_s, g_tail = _ffn_up(xbp, xbs, ssp, sss, w_ffn_gate[0], w_ffn_up[0],
                                    ffn_conv_w[0], ffn_conv_b, fs0, fs1, *FFN_UP_TILE)
    x2p, x2s = _resid_matmul(f_p, f_s, w_ffn_down[0], x1p, x1s, *FFN_DOWN_TILE, "ffn_down")
    y_p = _rmsnorm(x2p, final_norm_g, F32, NORM_ROWS)
    y_s = _rmsnorm(x2s, final_norm_g, F32, DEC_BATCH)

    kv_p = proj_p.reshape(BATCH, SEQ, PROJ_W)[:, SEQ - WINDOW:, OFF_K:OFF_GA].astype(F32)
    k_win_p = kv_p[:, :, :KV_W].reshape(1, BATCH, WINDOW, N_KV_HEADS, HEAD_DIM)
    v_win_p = kv_p[:, :, KV_W:].reshape(1, BATCH, WINDOW, N_KV_HEADS, HEAD_DIM)
    conv_p = u_tail[:, SUBLANES - (CONV_K - 1):, :]
    ffn_p = g_tail[:, SUBLANES - (FFN_CONV_K - 1):, :]
    conv_s = jnp.stack([cs1, u_s], axis=1)
    ffn_s = jnp.stack([fs1, g_s], axis=1)

    return (y_p.reshape(BATCH, SEQ, D_MODEL),
            y_s.reshape(DEC_BATCH, 1, D_MODEL),
            k_win_p, v_win_p, conv_p[None], ffn_p[None],
            _window_from_kernel(k_win_s), _window_from_kernel(v_win_s),
            conv_s[None], ffn_s[None])
```

```python
import functools
import math

import numpy as np
import jax
import jax.numpy as jnp
from jax import lax
from jax.experimental import pallas as pl
from jax.experimental.pallas import tpu as pltpu

F32 = jnp.float32
BF16 = jnp.bfloat16

D_MODEL = 4096
BATCH = 4
SEQ = 2048
DEC_BATCH = 128
N_HEADS = 32
N_KV_HEADS = 8
HEAD_DIM = 64
GROUP = N_HEADS // N_KV_HEADS
ATTN_W = N_HEADS * HEAD_DIM
KV_W = N_KV_HEADS * HEAD_DIM
CONV_W = D_MODEL // 2
CONV_K = 3
WINDOW = 128
Q_BLOCK = 128
N_BUCKETS = 32
MAX_DISTANCE = 128
D_FF = 11008
FFN_CONV_K = 3
EPS = 1e-5
NEG = -1e30
M_PROMPT = BATCH * SEQ

W_CB = 0
W_CC = CONV_W
W_CH = 2 * CONV_W
W_REST = 3 * CONV_W
OFF_Q = 0
OFF_K = OFF_Q + ATTN_W
OFF_V = OFF_K + KV_W
OFF_GA = OFF_V + KV_W
OFF_GB = OFF_GA + D_MODEL
PROJ_W = OFF_GB + D_MODEL

V7X_VMEM_BYTES = 64 * 1024 * 1024
VMEM_LIMIT = V7X_VMEM_BYTES - 1024 * 1024
SUBLANES = 8
LANES = 128

MIXER_TILE = (512, 512)
PROJ_TILE = (1024, 1024)
MERGE_TILE = (1024, 1024)
OUT_TILE = (512, 1024)
FFN_UP_TILE = (1024, 512)
FFN_DOWN_TILE = (512, 512)
NORM_ROWS = 512


def _cparams(*sem):
    return pltpu.CompilerParams(dimension_semantics=sem, vmem_limit_bytes=VMEM_LIMIT)


def _dot(a, b):
    return jnp.dot(a, b, preferred_element_type=F32)


def _sigmoid(x):
    return 1.0 / (1.0 + jnp.exp(-x))


def _ws_grid(n, bn, mp, bm):
    return (pl.cdiv(n, bn) + 1, mp // bm)


def _ws_rows(j, i):
    return (jnp.where(j > 0, i, 0), 0)


def _ws_wchunk(nj, col0=0):
    return lambda j, i: (i, col0 + jnp.minimum(j, nj - 1))


def _ws_tile(col0=0):
    return lambda j, i: (jnp.where(j > 0, i, 0), col0 + jnp.maximum(j - 1, 0))


def _ws_stile(col0=0):
    return lambda j, i: (0, col0 + jnp.maximum(j - 1, 0))


def _ws_cast(wf_refs, dst_refs):
    i = pl.program_id(1)
    for wf_ref, dst_ref in zip(wf_refs, dst_refs):
        rows = wf_ref.shape[0]
        r0 = pl.multiple_of(i * rows, rows)
        dst_ref[pl.ds(r0, rows), :] = wf_ref[...].astype(BF16)


def _ws_phases(wf_refs, slot0, slot1, compute, compute_last=None, last=None):
    j = pl.program_id(0)
    regular = (j > 0) if compute_last is None else ((j > 0) & (j < last))

    @pl.when(j == 0)
    def _():
        _ws_cast(wf_refs, slot0)

    @pl.when(regular & (j % 2 == 1))
    def _():
        _ws_cast(wf_refs, slot1)
        compute(slot0)

    @pl.when(regular & (j % 2 == 0))
    def _():
        _ws_cast(wf_refs, slot0)
        compute(slot1)

    if compute_last is not None:
        @pl.when(j == last)
        def _():
            compute_last(slot0 if last % 2 == 1 else slot1)


def _is_last_row_tile():
    return pl.program_id(1) == pl.num_programs(1) - 1


def _rms_kernel(x_ref, g_ref, o_ref):
    x = x_ref[...] if len(x_ref.shape) == 2 else x_ref[:, 0, :]
    r = lax.rsqrt(jnp.mean(x * x, axis=-1, keepdims=True) + EPS)
    y = ((x * r) * g_ref[...]).astype(o_ref.dtype)
    if len(o_ref.shape) == 2:
        o_ref[...] = y
    else:
        o_ref[:, 0, :] = y


def _rmsnorm(x, g, out_dtype, block_rows, out_3d=False):
    rows, d = x.shape[0], x.shape[-1]

    def spec(three_d):
        if three_d:
            return pl.BlockSpec((block_rows, 1, d), lambda i: (i, 0, 0))
        return pl.BlockSpec((block_rows, d), lambda i: (i, 0))

    out_shape = (rows, 1, d) if out_3d else (rows, d)
    return pl.pallas_call(
        _rms_kernel,
        grid=(rows // block_rows,),
        in_specs=[spec(x.ndim == 3), pl.BlockSpec((1, d), lambda i: (0, 0))],
        out_specs=spec(out_3d),
        out_shape=jax.ShapeDtypeStruct(out_shape, out_dtype),
        compiler_params=_cparams("parallel"),
        name="rmsnorm",
    )(x, g.reshape(1, d))


def _proj_kernel(xp_ref, xs_ref, ssp_ref, wf_ref, op_ref, os_ref, w0_ref, w1_ref):
    def compute(slot):
        (w_ref,) = slot
        op_ref[...] = (_inv_rms(ssp_ref) * _dot(xp_ref[...], w_ref[...])).astype(op_ref.dtype)

        @pl.when(_is_last_row_tile())
        def _():
            os_ref[...] = _dot(xs_ref[...], w_ref[...]).astype(os_ref.dtype)

    _ws_phases([wf_ref], [w0_ref], [w1_ref], compute)


def _in_proj(hp, hs, ssp, w, mp, col0, n, bm, bn):
    k = hp.shape[1]
    ms = hs.shape[0]
    grid = _ws_grid(n, bn, mp, bm)
    return pl.pallas_call(
        _proj_kernel,
        grid=grid,
        in_specs=[pl.BlockSpec((bm, k), _ws_rows),
                  pl.BlockSpec((ms, k), lambda j, i: (0, 0)),
                  pl.BlockSpec((bm, LANES), _ws_rows),
                  pl.BlockSpec((k // grid[1], bn), _ws_wchunk(grid[0] - 1, col0 // bn))],
        out_specs=[pl.BlockSpec((bm, bn), _ws_tile()),
                   pl.BlockSpec((ms, bn), _ws_stile())],
        out_shape=[jax.ShapeDtypeStruct((mp, n), BF16),
                   jax.ShapeDtypeStruct((ms, n), BF16)],
        scratch_shapes=[pltpu.VMEM((k, bn), BF16), pltpu.VMEM((k, bn), BF16)],
        compiler_params=_cparams("arbitrary", "arbitrary"),
        name="in_proj",
    )(hp, hs, ssp, w)


def _mixa_kernel(xp_ref, hs_ref, g_ref, wbf_ref, wcf_ref, whf_ref, cw_ref, s0_ref, s1_ref,
                 ap_ref, as_ref, us_ref, st_ref, hp_ref, ssp_ref,
                 wb0_ref, wc0_ref, wh0_ref, wb1_ref, wc1_ref, wh1_ref, carry_ref, ubuf_ref,
                 *, tiles_per_seq):
    i = pl.program_id(1)
    bm = xp_ref.shape[0]

    def compute(slot):
        wb_ref, wc_ref, wh_ref = slot
        w0 = cw_ref[0:1, :]
        w1 = cw_ref[1:2, :]
        w2 = cw_ref[2:3, :]

        @pl.when(i % tiles_per_seq == 0)
        def _():
            carry_ref[...] = jnp.zeros(carry_ref.shape, F32)

        hp_ref[...] = (xp_ref[...] * g_ref[...]).astype(hp_ref.dtype)
        ssp_ref[...] = _lane_partials(xp_ref[...] * xp_ref[...])
        r = _inv_rms(ssp_ref)
        u = (r * _dot(hp_ref[...], wc_ref[...])) * (r * _dot(hp_ref[...], wh_ref[...]))
        cb = r * _dot(hp_ref[...], wb_ref[...])
        ubuf_ref[0:SUBLANES, :] = carry_ref[...]
        ubuf_ref[SUBLANES:SUBLANES + bm, :] = u
        u1 = ubuf_ref[SUBLANES - 1:SUBLANES - 1 + bm, :]
        u2 = ubuf_ref[SUBLANES - 2:SUBLANES - 2 + bm, :]
        z = w0 * u2 + w1 * u1 + w2 * u
        ap_ref[...] = (cb * z).astype(ap_ref.dtype)
        tail = ubuf_ref[bm:SUBLANES + bm, :]
        carry_ref[...] = tail
        st_ref[0] = tail

        @pl.when(_is_last_row_tile())
        def _():
            us = _dot(hs_ref[...], wc_ref[...]) * _dot(hs_ref[...], wh_ref[...])
            zs = w0 * s0_ref[...] + w1 * s1_ref[...] + w2 * us
            as_ref[...] = (_dot(hs_ref[...], wb_ref[...]) * zs).astype(as_ref.dtype)
            us_ref[...] = us

    _ws_phases([wbf_ref, wcf_ref, whf_ref], [wb0_ref, wc0_ref, wh0_ref],
               [wb1_ref, wc1_ref, wh1_ref], compute)


def _mixer_a(xp, hs, gain, w_in, conv_w, s0, s1, bm, bn):
    mp, k = xp.shape
    ms = hs.shape[0]
    n = CONV_W
    grid = _ws_grid(n, bn, mp, bm)
    nj = grid[0] - 1
    tiles_per_seq = SEQ // bm
    wslot = pltpu.VMEM((k, bn), BF16)
    col = _ws_stile()

    def wchunk(col0):
        return pl.BlockSpec((k // grid[1], bn), _ws_wchunk(nj, col0 // bn))

    def tail_map(j, i):
        return (jnp.where(j > 0, i, 0) // tiles_per_seq, 0, jnp.maximum(j - 1, 0))

    spare = grid[1]

    def operand_map(j, i):
        return (jnp.where(j == 1, i, jnp.where(j == 0, 0, spare)), 0)

    return pl.pallas_call(
        functools.partial(_mixa_kernel, tiles_per_seq=tiles_per_seq),
        grid=grid,
        in_specs=[pl.BlockSpec((bm, k), _ws_rows),
                  pl.BlockSpec((ms, k), lambda j, i: (0, 0)),
                  pl.BlockSpec((1, k), lambda j, i: (0, 0)),
                  wchunk(W_CB), wchunk(W_CC), wchunk(W_CH),
                  pl.BlockSpec((CONV_K, bn), col),
                  pl.BlockSpec((ms, bn), col),
                  pl.BlockSpec((ms, bn), col)],
        out_specs=[pl.BlockSpec((bm, bn), _ws_tile()),
                   pl.BlockSpec((ms, bn), col),
                   pl.BlockSpec((ms, bn), col),
                   pl.BlockSpec((1, SUBLANES, bn), tail_map),
                   pl.BlockSpec((bm, k), operand_map),
                   pl.BlockSpec((bm, LANES), operand_map)],
        out_shape=[jax.ShapeDtypeStruct((mp, n), BF16),
                   jax.ShapeDtypeStruct((ms, n), BF16),
                   jax.ShapeDtypeStruct((ms, n), F32),
                   jax.ShapeDtypeStruct((BATCH, SUBLANES, n), F32),
                   jax.ShapeDtypeStruct((mp + bm, k), BF16),
                   jax.ShapeDtypeStruct((mp + bm, LANES), F32)],
        scratch_shapes=[wslot] * 6 + [pltpu.VMEM((SUBLANES, bn), F32),
                                      pltpu.VMEM((SUBLANES + bm, bn), F32)],
        compiler_params=_cparams("arbitrary", "arbitrary"),
        name="mixer_a",
    )(xp, hs, gain.reshape(1, k), w_in, w_in, w_in, conv_w, s0, s1)


def _bucket_table():
    qi = np.arange(Q_BLOCK)[None, :]
    kj = np.arange(2 * Q_BLOCK)[:, None]
    dist = qi + Q_BLOCK - kj
    max_exact = N_BUCKETS // 2
    d = np.maximum(dist, 0)
    df = np.maximum(d, 1).astype(np.float32)
    large = max_exact + (np.log(df / np.float32(max_exact))
                         / np.float32(math.log(MAX_DISTANCE / max_exact))
                         * np.float32(N_BUCKETS - max_exact)).astype(np.int32)
    large = np.minimum(large, N_BUCKETS - 1)
    bucket = np.where(d < max_exact, d, large).astype(np.int32)
    valid = ((dist >= 0) & (dist <= WINDOW)).astype(np.int32)
    return bucket, valid


def _bias_kernel(rb_ref, bucket_ref, valid_ref, o_ref):
    bucket = bucket_ref[...]
    row = lax.broadcasted_iota(jnp.int32, bucket.shape, 0)
    keep = valid_ref[...] != 0
    keep_first = keep & (row >= Q_BLOCK)

    def head(h, carry):
        acc = jnp.zeros(bucket.shape, F32)
        for b in range(N_BUCKETS):
            acc = jnp.where(bucket == b, rb_ref[b, h], acc)
        o_ref[0, h] = jnp.where(keep, acc, NEG)
        o_ref[1, h] = jnp.where(keep_first, acc, NEG)
        return carry

    lax.fori_loop(0, N_HEADS, head, 0)


def _bias_table(rel_bias):
    bucket, valid = _bucket_table()
    shp = (2 * Q_BLOCK, Q_BLOCK)
    return pl.pallas_call(
        _bias_kernel,
        grid=(1,),
        in_specs=[pl.BlockSpec(memory_space=pltpu.SMEM),
                  pl.BlockSpec(shp, lambda v: (0, 0)),
                  pl.BlockSpec(shp, lambda v: (0, 0))],
        out_specs=pl.BlockSpec((2, N_HEADS) + shp, lambda v: (0, 0, 0, 0)),
        out_shape=jax.ShapeDtypeStruct((2, N_HEADS) + shp, F32),
        compiler_params=_cparams("arbitrary"),
        name="rel_bias_table",
    )(rel_bias, jnp.asarray(bucket), jnp.asarray(valid))


BLOCKS_PER_STEP = 2


def _attn_prompt_kernel(sink_ref, q_ref, kvp_ref, kvc_ref, bias_ref, o_ref):
    scale = HEAD_DIM ** -0.5
    first_variant = jnp.where(pl.program_id(1) == 0, 1, 0)

    def band(blk, col):
        cols = slice(col, col + HEAD_DIM)
        cur = kvc_ref[blk * Q_BLOCK:(blk + 1) * Q_BLOCK, cols]
        if blk == 0:
            return jnp.concatenate([kvp_ref[:, cols], cur], axis=0)
        return kvc_ref[(blk - 1) * Q_BLOCK:(blk + 1) * Q_BLOCK, cols]

    def scores(blk, kv):
        heads = range(kv * GROUP, (kv + 1) * GROUP)
        rows = slice(blk * Q_BLOCK, (blk + 1) * Q_BLOCK)
        variant = first_variant if blk == 0 else 0
        qg = jnp.concatenate(
            [q_ref[rows, h * HEAD_DIM:(h + 1) * HEAD_DIM] for h in heads], axis=0) * scale
        st = lax.dot_general(band(blk, kv * HEAD_DIM), qg, (((1,), (1,)), ((), ())),
                             preferred_element_type=F32)
        return st + jnp.concatenate([bias_ref[variant, h] for h in heads], axis=1)

    def finish(blk, kv, st):
        heads = range(kv * GROUP, (kv + 1) * GROUP)
        vband = band(blk, KV_W + kv * HEAD_DIM)
        sk = jnp.concatenate([jnp.full((1, Q_BLOCK), sink_ref[h], F32) for h in heads], axis=1)
        m = jnp.maximum(jnp.max(st, axis=0, keepdims=True), sk)
        e = jnp.exp(st - m)
        denom = jnp.sum(e, axis=0, keepdims=True) + jnp.exp(sk - m)
        ot = lax.dot_general(vband, e.astype(BF16), (((0,), (0,)), ((), ())),
                             preferred_element_type=F32) * (1.0 / denom)
        o2 = jnp.concatenate(
            [ot[:, g * Q_BLOCK:(g + 1) * Q_BLOCK] for g in range(GROUP)], axis=0)
        rows = slice(blk * Q_BLOCK, (blk + 1) * Q_BLOCK)
        gs = slice(kv * GROUP * HEAD_DIM, (kv + 1) * GROUP * HEAD_DIM)
        o_ref[rows, gs] = o2.T.astype(o_ref.dtype)

    groups =[(blk, kv) for blk in range(BLOCKS_PER_STEP) for kv in range(N_KV_HEADS)]
    pairs = [groups[p:p + 2] for p in range(0, len(groups), 2)]
    sts = [scores(*grp) for grp in pairs[0]]
    for idx, pair in enumerate(pairs):
        sts_next = [scores(*grp) for grp in pairs[idx + 1]] if idx + 1 < len(pairs) else None
        for grp, st in zip(pair, sts):
            finish(*grp, st)
        sts = sts_next


def _attn_prompt(proj_p, bias_tab, sinks):
    rows = BLOCKS_PER_STEP * Q_BLOCK
    steps = SEQ // rows
    qcol = OFF_Q // ATTN_W
    kvcol = OFF_K // (2 * KV_W)

    def prev(b, n):
        return b * (SEQ // Q_BLOCK) + jnp.maximum(n * BLOCKS_PER_STEP - 1, 0)

    return pl.pallas_call(
        _attn_prompt_kernel,
        grid=(BATCH, steps),
        in_specs=[pl.BlockSpec(memory_space=pltpu.SMEM),
                  pl.BlockSpec((rows, ATTN_W), lambda b, n: (b * steps + n, qcol)),
                  pl.BlockSpec((Q_BLOCK, 2 * KV_W), lambda b, n: (prev(b, n), kvcol)),
                  pl.BlockSpec((rows, 2 * KV_W), lambda b, n: (b * steps + n, kvcol)),
                  pl.BlockSpec((2, N_HEADS, 2 * Q_BLOCK, Q_BLOCK), lambda b, n: (0, 0, 0, 0))],
        out_specs=pl.BlockSpec((rows, ATTN_W), lambda b, n: (b * steps + n, 0)),
        out_shape=jax.ShapeDtypeStruct((M_PROMPT, ATTN_W), BF16),
        compiler_params=_cparams("parallel", "arbitrary"),
        name="attn_prompt",
    )(sinks, proj_p, proj_p, proj_p, bias_tab)


SAMPLES_PER_STEP = 8


def _attn_sample_kernel(q_ref, kn_ref, vn_ref, kt_ref, vt_ref, bias_ref, sink_ref,
                        o_ref, kto_ref, vto_ref):
    g = SAMPLES_PER_STEP
    scale = HEAD_DIM ** -0.5
    erow = lax.broadcasted_iota(jnp.int32, (HEAD_DIM, KV_W), 0)
    ecol = lax.broadcasted_iota(jnp.int32, (HEAD_DIM, KV_W), 1)
    expand = (ecol % HEAD_DIM == erow).astype(BF16)
    hrow = lax.broadcasted_iota(jnp.int32, (N_HEADS, KV_W), 0)
    hcol = lax.broadcasted_iota(jnp.int32, (N_HEADS, KV_W), 1)
    own = ((hrow // GROUP) == (hcol // HEAD_DIM))[None]
    bias_w = bias_ref[:, 0:WINDOW][None]
    bias_n = bias_ref[:, WINDOW:WINDOW + 1][None]
    sk = sink_ref[...][None]
    contract_last = (((1,), (1,)), ((), ()))

    q_all = q_ref[...].reshape(g * N_HEADS, HEAD_DIM) * scale
    qrow = jnp.where(own, _dot(q_all, expand).reshape(g, N_HEADS, KV_W), 0.0)
    qrow_b = qrow.astype(BF16)
    kn = kn_ref[...][:, None, :]
    vn = vn_ref[...][:, None, :]
    s_w = jnp.stack([_dot(qrow_b[b], kt_ref[b].astype(BF16)) for b in range(g)])
    s_w = s_w + bias_w
    s_n = jnp.sum(qrow * kn, axis=-1, keepdims=True) + bias_n
    m = jnp.maximum(jnp.maximum(jnp.max(s_w, axis=-1, keepdims=True), s_n), sk)
    e_w = jnp.exp(s_w - m)
    e_n = jnp.exp(s_n - m)
    r = 1.0 / (jnp.sum(e_w, axis=-1, keepdims=True) + e_n + jnp.exp(sk - m))
    p_w = (e_w * r).astype(BF16)
    o_all = jnp.stack([lax.dot_general(p_w[b], vt_ref[b].astype(BF16), contract_last,
                                       preferred_element_type=F32) for b in range(g)])
    o_own = jnp.where(own, o_all + (e_n * r) * vn, 0.0)
    o = o_own[:, :, 0:HEAD_DIM]
    for kv in range(1, N_KV_HEADS):
        o = o + o_own[:, :, kv * HEAD_DIM:(kv + 1) * HEAD_DIM]
    o_ref[...] = o.astype(o_ref.dtype)

    rows = jnp.concatenate([kn_ref[...], vn_ref[...]], axis=0).astype(BF16)
    place = (lax.broadcasted_iota(jnp.int32, (2 * g, 2 * g * WINDOW), 1)
             == WINDOW * lax.broadcasted_iota(jnp.int32, (2 * g, 2 * g * WINDOW), 0)
             ).astype(BF16)
    newcols = lax.dot_general(rows, place, (((0,), (0,)), ((), ())),
                              preferred_element_type=F32)
    first = lax.broadcasted_iota(jnp.int32, (KV_W, WINDOW), 1) == 0
    for b in range(g):
        kcol = newcols[:, b * WINDOW:(b + 1) * WINDOW]
        vcol = newcols[:, (g + b) * WINDOW:(g + b + 1) * WINDOW]
        kto_ref[b] = pltpu.roll(jnp.where(first, kcol, kt_ref[b]), WINDOW - 1, 1)
        vto_ref[b] = pltpu.roll(jnp.where(first, vcol, vt_ref[b]), WINDOW - 1, 1)


def _attn_sample(q3, k_new, v_new, k_buf, v_buf, bias_s, sinks):
    g = SAMPLES_PER_STEP
    nb = DEC_BATCH // g
    win = pl.BlockSpec((g, KV_W, WINDOW), lambda i: (i, 0, 0))
    row = pl.BlockSpec((g, KV_W), lambda i: (i, 0))
    return pl.pallas_call(
        _attn_sample_kernel,
        grid=(nb,),
        in_specs=[pl.BlockSpec((g, N_HEADS, HEAD_DIM), lambda i: (i, 0, 0)),
                  row, row, win, win,
                  pl.BlockSpec((N_HEADS, WINDOW + 1), lambda i: (0, 0)),
                  pl.BlockSpec((N_HEADS, 1), lambda i: (0, 0))],
        out_specs=[pl.BlockSpec((g, N_HEADS, HEAD_DIM), lambda i: (i, 0, 0)), win, win],
        out_shape=[jax.ShapeDtypeStruct((DEC_BATCH, N_HEADS, HEAD_DIM), BF16),
                   jax.ShapeDtypeStruct((DEC_BATCH, KV_W, WINDOW), F32),
                   jax.ShapeDtypeStruct((DEC_BATCH, KV_W, WINDOW), F32)],
        compiler_params=_cparams("parallel"),
        name="attn_sample",
    )(q3, k_new, v_new, k_buf, v_buf, bias_s, sinks.reshape(N_HEADS, 1))


def _merge_kernel(ap_ref, bp_ref, as_ref, bs_ref, waf_ref, wbf_ref,
                  gap_ref, gbp_ref, gas_ref, gbs_ref, op_ref, os_ref,
                  wa0_ref, wb0_ref, wa1_ref, wb1_ref):
    def compute(slot):
        wa_ref, wb_ref = slot

        def merged(a_ref, b_ref, ga_ref, gb_ref):
            return (_sigmoid(ga_ref[...].astype(F32)) * _dot(a_ref[...], wa_ref[...])
                    + _sigmoid(gb_ref[...].astype(F32)) * _dot(b_ref[...], wb_ref[...]))

        op_ref[...] = merged(ap_ref, bp_ref, gap_ref, gbp_ref).astype(op_ref.dtype)

        @pl.when(_is_last_row_tile())
        def _():
            os_ref[...] = merged(as_ref, bs_ref, gas_ref, gbs_ref).astype(os_ref.dtype)

    _ws_phases([waf_ref, wbf_ref], [wa0_ref, wb0_ref], [wa1_ref, wb1_ref], compute)


def _branch_merge(a_p, o_p, a_s, o_s, wa, wb, proj_p, proj_s, bm, bn):
    mp, k = a_p.shape
    ms = a_s.shape[0]
    n = wa.shape[1]
    grid = _ws_grid(n, bn, mp, bm)
    ga0 = OFF_GA // bn
    gb0 = OFF_GB // bn
    wchunk = pl.BlockSpec((k // grid[1], bn), _ws_wchunk(grid[0] - 1))
    wslot = pltpu.VMEM((k, bn), BF16)
    return pl.pallas_call(
        _merge_kernel,
        grid=grid,
        in_specs=[pl.BlockSpec((bm, k), _ws_rows),
                  pl.BlockSpec((bm, k), _ws_rows),
                  pl.BlockSpec((ms, k), lambda j, i: (0, 0)),
                  pl.BlockSpec((ms, k), lambda j, i: (0, 0)),
                  wchunk, wchunk,
                  pl.BlockSpec((bm, bn), _ws_tile(ga0)),
                  pl.BlockSpec((bm, bn), _ws_tile(gb0)),
                  pl.BlockSpec((ms, bn), _ws_stile(ga0)),
                  pl.BlockSpec((ms, bn), _ws_stile(gb0))],
        out_specs=[pl.BlockSpec((bm, bn), _ws_tile()),
                   pl.BlockSpec((ms, bn), _ws_stile())],
        out_shape=[jax.ShapeDtypeStruct((mp, n), BF16),
                   jax.ShapeDtypeStruct((ms, n), BF16)],
        scratch_shapes=[wslot, wslot, wslot, wslot],
        compiler_params=_cparams("arbitrary", "arbitrary"),
        name="branch_merge",
    )(a_p, o_p, a_s, o_s, wa, wb, proj_p, proj_p, proj_s, proj_s)


def _resid_kernel(xp_ref, xs_ref, wf_ref, rp_ref, rs_ref, op_ref, os_ref, w0_ref, w1_ref):
    def compute(slot):
        (w_ref,) = slot
        op_ref[...] = rp_ref[...] + _dot(xp_ref[...], w_ref[...])

        @pl.when(_is_last_row_tile())
        def _():
            os_ref[...] = rs_ref[...] + _dot(xs_ref[...], w_ref[...])

    _ws_phases([wf_ref], [w0_ref], [w1_ref], compute)


def _resid_matmul(xp, xs, w, rp, rs, bm, bn, name):
    mp, k = xp.shape
    ms = xs.shape[0]
    n = w.shape[1]
    grid = _ws_grid(n, bn, mp, bm)
    return pl.pallas_call(
        _resid_kernel,
        grid=grid,
        in_specs=[pl.BlockSpec((bm, k), _ws_rows),
                  pl.BlockSpec((ms, k), lambda j, i: (0, 0)),
                  pl.BlockSpec((k // grid[1], bn), _ws_wchunk(grid[0] - 1)),
                  pl.BlockSpec((bm, bn), _ws_tile()),
                  pl.BlockSpec((ms, bn), _ws_stile())],
        out_specs=[pl.BlockSpec((bm, bn), _ws_tile()),
                   pl.BlockSpec((ms, bn), _ws_stile())],
        out_shape=[jax.ShapeDtypeStruct((mp, n), F32),
                   jax.ShapeDtypeStruct((ms, n), F32)],
        scratch_shapes=[pltpu.VMEM((k, bn), BF16), pltpu.VMEM((k, bn), BF16)],
        compiler_params=_cparams("arbitrary", "arbitrary"),
        name=name,
    )(xp, xs, w, rp, rs)


def _lane_partials(v):
    part = v[:, 0:LANES]
    for c in range(1, v.shape[1] // LANES):
        part = part + v[:, c * LANES:(c + 1) * LANES]
    return part


def _outproj_kernel(xp_ref, xs_ref, wf_ref, rp_ref, rs_ref, g_ref,
                    op_ref, os_ref, bp_ref, bs_ref, ssp_ref, sss_ref,
                    w0_ref, w1_ref, accp_ref, accs_ref):
    j = pl.program_id(0)
    i = pl.program_id(1)
    bm = xp_ref.shape[0]

    def compute(slot):
        (w_ref,) = slot

        def tile(x_ref, r_ref, o_ref, b_ref, acc_view, ss_ref):
            resid = r_ref[...] if len(r_ref.shape) == 2 else r_ref[:, 0, :]
            x1 = resid + _dot(x_ref[...], w_ref[...])
            o_ref[...] = x1
            b_ref[...] = (x1 * g_ref[...]).astype(b_ref.dtype)
            total = jnp.where(j == 1, 0.0, acc_view[...]) + _lane_partials(x1 * x1)
            acc_view[...] = total
            ss_ref[...] = total

        rows = pl.ds(pl.multiple_of(i * bm, bm), bm)
        tile(xp_ref, rp_ref, op_ref, bp_ref, accp_ref.at[rows, :], ssp_ref)

        @pl.when(_is_last_row_tile())
        def _():
            tile(xs_ref, rs_ref, os_ref, bs_ref, accs_ref, sss_ref)

    _ws_phases([wf_ref], [w0_ref], [w1_ref], compute)


def _out_proj(xp, xs, w, rp, rs, gain, bm, bn):
    mp, k = xp.shape
    ms = xs.shape[0]
    n = w.shape[1]
    grid = _ws_grid(n, bn, mp, bm)
    last = grid[0] - 1
    ssp_map = lambda j, i: (jnp.where(j == last, i, 0), 0)
    return pl.pallas_call(
        _outproj_kernel,
        grid=grid,
        in_specs=[pl.BlockSpec((bm, k), _ws_rows),
                  pl.BlockSpec((ms, k), lambda j, i: (0, 0)),
                  pl.BlockSpec((k // grid[1], bn), _ws_wchunk(grid[0] - 1)),
                  pl.BlockSpec((bm, bn), _ws_tile()),
                  pl.BlockSpec((ms, 1, bn), lambda j, i: (0, 0, jnp.maximum(j - 1, 0))),
                  pl.BlockSpec((1, bn), _ws_stile())],
        out_specs=[pl.BlockSpec((bm, bn), _ws_tile()),
                   pl.BlockSpec((ms, bn), _ws_stile()),
                   pl.BlockSpec((bm, bn), _ws_tile()),
                   pl.BlockSpec((ms, bn), _ws_stile()),
                   pl.BlockSpec((bm, LANES), ssp_map),
                   pl.BlockSpec((ms, LANES), lambda j, i: (0, 0))],
        out_shape=[jax.ShapeDtypeStruct((mp, n), F32),
                   jax.ShapeDtypeStruct((ms, n), F32),
                   jax.ShapeDtypeStruct((mp, n), BF16),
                   jax.ShapeDtypeStruct((ms, n), BF16),
                   jax.ShapeDtypeStruct((mp, LANES), F32),
                   jax.ShapeDtypeStruct((ms, LANES), F32)],
        scratch_shapes=[pltpu.VMEM((k, bn), BF16), pltpu.VMEM((k, bn), BF16),
                        pltpu.VMEM((mp, LANES), F32), pltpu.VMEM((ms, LANES), F32)],
        compiler_params=_cparams("arbitrary", "arbitrary"),
        name="out_proj",
    )(xp, xs, w, rp, rs, gain.reshape(1, n))


def _inv_rms(ss_ref):
    return lax.rsqrt(jnp.sum(ss_ref[...], axis=-1, keepdims=True) * (1.0 / D_MODEL) + EPS)


def _ffn_up_kernel(hp_ref, hs_ref, ssp_ref, sss_ref, wgf_ref, wuf_ref, cw_ref, cb_ref,
                   sst_ref, fp_ref, fs_ref, sso_ref, st_ref,
                   wg0_ref, wu0_ref, wg1_ref, wu1_ref, carry_ref, gbuf_ref,
                   *, tiles_per_seq, last_phase, last_width):
    i = pl.program_id(1)
    bm = hp_ref.shape[0]

    def compute(slot, width):
        wg_ref, wu_ref = slot
        cols = slice(0, width)
        w0 = cw_ref[0:1, cols]
        w1 = cw_ref[1:2, cols]
        w2 = cw_ref[2:3, cols]
        bias = cb_ref[:, cols]

        @pl.when(i % tiles_per_seq == 0)
        def _():
            carry_ref[...] = jnp.zeros(carry_ref.shape, F32)

        r = _inv_rms(ssp_ref)
        g = r * _dot(hp_ref[...], wg_ref[:, cols])
        gbuf_ref[0:SUBLANES, cols] = carry_ref[:, cols]
        gbuf_ref[SUBLANES:SUBLANES + bm, cols] = g
        g1 = gbuf_ref[SUBLANES - 1:SUBLANES - 1 + bm, cols]
        g2 = gbuf_ref[SUBLANES - 2:SUBLANES - 2 + bm, cols]
        gc = w0 * g2 + w1 * g1 + w2 * g + bias
        up = r * _dot(hp_ref[...], wu_ref[:, cols])
        fp_ref[:, cols] = (gc * _sigmoid(gc) * up).astype(fp_ref.dtype)
        tail = gbuf_ref[bm:SUBLANES + bm, cols]
        carry_ref[:, cols] = tail
        st_ref[0, :, cols] = tail

        @pl.when(_is_last_row_tile())
        def _():
            rs = _inv_rms(sss_ref)
            gsm = rs * _dot(hs_ref[...], wg_ref[:, cols])
            ups = rs * _dot(hs_ref[...], wu_ref[:, cols])
            groups = range(width // LANES)
            s0 = jnp.concatenate([sst_ref[:, c, 0, :] for c in groups], axis=-1)
            s1 = jnp.concatenate([sst_ref[:, c, 1, :] for c in groups], axis=-1)
            gcs = w0 * s0 + w1 * s1 + w2 * gsm + bias
            fs_ref[:, cols] = (gcs * _sigmoid(gcs) * ups).astype(fs_ref.dtype)
            for c in groups:
                sso_ref[:, c, 0, :] = sst_ref[:, c, 1, :]
                sso_ref[:, c, 1, :] = gsm[:, c * LANES:(c + 1) * LANES]

    full_width = fp_ref.shape[1]
    _ws_phases([wgf_ref, wuf_ref], [wg0_ref, wu0_ref], [wg1_ref, wu1_ref],
               functools.partial(compute, width=full_width),
               functools.partial(compute, width=last_width) if last_width < full_width else None,
               last_phase)


def _ffn_up(hp, hs, ssp, sss, wg, wu, cw, cb, sst, bm, bn):
    mp, k = hp.shape
    ms = hs.shape[0]
    n = wg.shape[1]
    grid = _ws_grid(n, bn, mp, bm)
    tiles_per_seq = SEQ // bm
    wchunk = pl.BlockSpec((k // grid[1], bn), _ws_wchunk(grid[0] - 1))
    wslot = pltpu.VMEM((k, bn), BF16)
    col = _ws_stile()
    state = pl.BlockSpec((ms, bn // LANES, FFN_CONV_K - 1, LANES),
                         lambda j, i: (0, jnp.maximum(j - 1, 0), 0, 0))

    def tail_map(j, i):
        return (jnp.where(j > 0, i, 0) // tiles_per_seq, 0, jnp.maximum(j - 1, 0))

    last_phase = grid[0] - 1
    last_width = n - (last_phase - 1) * bn
    return pl.pallas_call(
        functools.partial(_ffn_up_kernel, tiles_per_seq=tiles_per_seq,
                          last_phase=last_phase, last_width=last_width),
        grid=grid,
        in_specs=[pl.BlockSpec((bm, k), _ws_rows),
                  pl.BlockSpec((ms, k), lambda j, i: (0, 0)),
                  pl.BlockSpec((bm, LANES), _ws_rows),
                  pl.BlockSpec((ms, LANES), lambda j, i: (0, 0)),
                  wchunk, wchunk,
                  pl.BlockSpec((FFN_CONV_K, bn), col),
                  pl.BlockSpec((1, bn), col),
                  state],
        out_specs=[pl.BlockSpec((bm, bn), _ws_tile()),
                   pl.BlockSpec((ms, bn), col),
                   state,
                   pl.BlockSpec((1, SUBLANES, bn), tail_map)],
        out_shape=[jax.ShapeDtypeStruct((mp, n), BF16),
                   jax.ShapeDtypeStruct((ms, n), BF16),
                   jax.ShapeDtypeStruct(sst.shape, F32),
                   jax.ShapeDtypeStruct((BATCH, SUBLANES, n), F32)],
        scratch_shapes=[wslot, wslot, wslot, wslot,
                        pltpu.VMEM((SUBLANES, bn), F32),
                        pltpu.VMEM((SUBLANES + bm, bn), F32)],
        compiler_params=_cparams("arbitrary", "arbitrary"),
        name="ffn_up",
    )(hp, hs, ssp, sss, wg, wu, cw, cb, sst)


def _window_to_kernel(state):
    return jnp.transpose(state[0], (0, 2, 3, 1)).reshape(DEC_BATCH, KV_W, WINDOW)


def _window_from_kernel(win):
    win = win.reshape(DEC_BATCH, N_KV_HEADS, HEAD_DIM, WINDOW)
    return jnp.transpose(win, (0, 3, 1, 2))[None]


def _conv_state_to_kernel(state):
    _, ms, rows, width = state.shape
    return jnp.transpose(state[0].reshape(ms, rows, width // LANES, LANES), (0, 2, 1, 3))


def _conv_state_from_kernel(state):
    ms, groups, rows, _ = state.shape
    return jnp.transpose(state, (0, 2, 1, 3)).reshape(1, ms, rows, groups * LANES)


def kernel(x_prompt, x_sample, state_k_window, state_v_window, state_conv, state_ffn_conv,
           attn_norm_g, w_in, conv_w, w_branch_a, w_branch_b, sinks, w_out, ffn_norm_g,
           w_ffn_gate, w_ffn_up, ffn_conv_w, ffn_conv_b, w_ffn_down, rel_bias,
           final_norm_g):
    xp = x_prompt.reshape(M_PROMPT, D_MODEL)
    xs = x_sample

    cs0 = state_conv[0][:, 0, :]
    cs1 = state_conv[0][:, 1, :]
    hs = _rmsnorm(xs, attn_norm_g[0], BF16, DEC_BATCH)
    a_p, a_s, u_s, u_tail, hp, hssp = _mixer_a(
        xp, hs, attn_norm_g[0], w_in[0], conv_w[0], cs0, cs1, *MIXER_TILE)
    proj_p, proj_s = _in_proj(hp, hs, hssp, w_in[0], M_PROMPT, W_REST, PROJ_W, *PROJ_TILE)

    bias_tab = _bias_table(rel_bias)
    o_p = _attn_prompt(proj_p, bias_tab, sinks[0])
    q3 = proj_s[:, OFF_Q:OFF_Q + ATTN_W].reshape(DEC_BATCH, N_HEADS, HEAD_DIM)
    k_new = proj_s[:, OFF_K:OFF_K + KV_W].astype(F32)
    v_new = proj_s[:, OFF_V:OFF_V + KV_W].astype(F32)
    bias_s = bias_tab[0, :, Q_BLOCK - 1:, Q_BLOCK - 1]
    o_s3, k_win_s, v_win_s = _attn_sample(
        q3, k_new, v_new,
        _window_to_kernel(state_k_window), _window_to_kernel(state_v_window),
        bias_s, sinks[0])
    o_s = o_s3.reshape(DEC_BATCH, ATTN_W)

    mg_p, mg_s = _branch_merge(a_p, o_p, a_s, o_s, w_branch_a[0], w_branch_b[0],
                               proj_p, proj_s, *MERGE_TILE)
    x1p, x1s, xbp, xbs, ssp, sss = _out_proj(mg_p, mg_s, w_out[0], xp, xs, ffn_norm_g[0],
                                             *OUT_TILE)

    f_p, f_s, ffn_state_new, g_tail = _ffn_up(xbp, xbs, ssp, sss, w_ffn_gate[0], w_ffn_up[0],
                                              ffn_conv_w[0], ffn_conv_b,
                                              _conv_state_to_kernel(state_ffn_conv),
                                              *FFN_UP_TILE)
    x2p, x2s = _resid_matmul(f_p, f_s, w_ffn_down[0], x1p, x1s, *FFN_DOWN_TILE, "ffn_down")
    y_p = _rmsnorm(x2p, final_norm_g, F32, NORM_ROWS)
    y_s = _rmsnorm(x2s, final_norm_g, F32, DEC_BATCH, out_3d=True)

    kv_p = proj_p.reshape(BATCH, SEQ, PROJ_W)[:, SEQ - WINDOW:, OFF_K:OFF_GA].astype(F32)
    k_win_p = kv_p[:, :, :KV_W].reshape(1, BATCH, WINDOW, N_KV_HEADS, HEAD_DIM)
    v_win_p = kv_p[:, :, KV_W:].reshape(1, BATCH, WINDOW, N_KV_HEADS, HEAD_DIM)
    conv_p = u_tail[:, SUBLANES - (CONV_K - 1):, :]
    ffn_p = g_tail[:, SUBLANES - (FFN_CONV_K - 1):, :]
    conv_s = jnp.stack([cs1, u_s], axis=1)

    return (y_p.reshape(BATCH, SEQ, D_MODEL),
            y_s,
            k_win_p, v_win_p, conv_p[None], ffn_p[None],
            _window_from_kernel(k_win_s), _window_from_kernel(v_win_s),
            conv_s[None], _conv_state_from_kernel(ffn_state_new))
```

```python
import functools
import math

import numpy as np
import jax
import jax.numpy as jnp
from jax import lax
from jax.experimental import pallas as pl
from jax.experimental.pallas import tpu as pltpu

F32 = jnp.float32
BF16 = jnp.bfloat16

D_MODEL = 4096
BATCH = 4
SEQ = 2048
DEC_BATCH = 128
N_HEADS = 32
N_KV_HEADS = 8
HEAD_DIM = 64
GROUP = N_HEADS // N_KV_HEADS
ATTN_W = N_HEADS * HEAD_DIM
KV_W = N_KV_HEADS * HEAD_DIM
CONV_W = D_MODEL // 2
CONV_K = 3
WINDOW = 128
Q_BLOCK = 128
N_BUCKETS = 32
MAX_DISTANCE = 128
D_FF = 11008
FFN_CONV_K = 3
EPS = 1e-5
NEG = -1e30
M_PROMPT = BATCH * SEQ

W_CB = 0
W_CC = CONV_W
W_CH = 2 * CONV_W
W_REST = 3 * CONV_W
OFF_Q = 0
OFF_K = OFF_Q + ATTN_W
OFF_V = OFF_K + KV_W
OFF_GA = OFF_V + KV_W
OFF_GB = OFF_GA + D_MODEL
PROJ_W = OFF_GB + D_MODEL

V7X_VMEM_BYTES = 64 * 1024 * 1024
VMEM_LIMIT = V7X_VMEM_BYTES - 1024 * 1024
SUBLANES = 8
LANES = 128

MIXER_TILE = (512, 512)
PROJ_TILE = (1024, 1024)
MERGE_TILE = (1024, 1024)
OUT_TILE = (512, 1024)
FFN_UP_TILE = (1024, 512)
FFN_DOWN_TILE = (512, 512)
NORM_ROWS = 512


def _cparams(*sem):
    return pltpu.CompilerParams(dimension_semantics=sem, vmem_limit_bytes=VMEM_LIMIT)


def _dot(a, b):
    return jnp.dot(a, b, preferred_element_type=F32)


def _sigmoid(x):
    return 1.0 / (1.0 + jnp.exp(-x))


def _ws_grid(n, bn, mp, bm):
    return (pl.cdiv(n, bn) + 1, mp // bm)


def _ws_rows(j, i):
    return (jnp.where(j > 0, i, 0), 0)


def _ws_wchunk(nj, col0=0):
    return lambda j, i: (i, col0 + jnp.minimum(j, nj - 1))


def _ws_tile(col0=0):
    return lambda j, i: (jnp.where(j > 0, i, 0), col0 + jnp.maximum(j - 1, 0))


def _ws_stile(col0=0):
    return lambda j, i: (0, col0 + jnp.maximum(j - 1, 0))


def _ws_cast(wf_refs, dst_refs):
    i = pl.program_id(1)
    for wf_ref, dst_ref in zip(wf_refs, dst_refs):
        rows = wf_ref.shape[0]
        r0 = pl.multiple_of(i * rows, rows)
        dst_ref[pl.ds(r0, rows), :] = wf_ref[...].astype(BF16)


def _ws_phases(wf_refs, slot0, slot1, compute, compute_last=None, last=None):
    j = pl.program_id(0)
    regular = (j > 0) if compute_last is None else ((j > 0) & (j < last))

    @pl.when(j == 0)
    def _():
        _ws_cast(wf_refs, slot0)

    @pl.when(regular & ((j & 1) == 1))
    def _():
        _ws_cast(wf_refs, slot1)
        compute(slot0)

    @pl.when(regular & ((j & 1) == 0))
    def _():
        _ws_cast(wf_refs, slot0)
        compute(slot1)

    if compute_last is not None:
        @pl.when(j == last)
        def _():
            compute_last(slot0 if last % 2 == 1 else slot1)


def _is_last_row_tile():
    return pl.program_id(1) == pl.num_programs(1) - 1


def _rms_kernel(x_ref, g_ref, o_ref):
    x = x_ref[...]
    r = lax.rsqrt(jnp.mean(x * x, axis=-1, keepdims=True) + EPS)
    o_ref[...] = ((x * r) * g_ref[...]).astype(o_ref.dtype)


def _rmsnorm(x, g, out_dtype, block_rows):
    rows, d = x.shape
    return pl.pallas_call(
        _rms_kernel,
        grid=(rows // block_rows,),
        in_specs=[pl.BlockSpec((block_rows, d), lambda i: (i, 0)),
                  pl.BlockSpec((1, d), lambda i: (0, 0))],
        out_specs=pl.BlockSpec((block_rows, d), lambda i: (i, 0)),
        out_shape=jax.ShapeDtypeStruct((rows, d), out_dtype),
        compiler_params=_cparams("parallel"),
        name="rmsnorm",
    )(x, g.reshape(1, d))


def _proj_kernel(xp_ref, xs_ref, ssp_ref, wf_ref, op_ref, os_ref, w0_ref, w1_ref):
    def compute(slot):
        (w_ref,) = slot
        op_ref[...] = (_inv_rms(ssp_ref) * _dot(xp_ref[...], w_ref[...])).astype(op_ref.dtype)

        @pl.when(_is_last_row_tile())
        def _():
            os_ref[...] = _dot(xs_ref[...], w_ref[...]).astype(os_ref.dtype)

    _ws_phases([wf_ref], [w0_ref], [w1_ref], compute)


def _in_proj(hp, hs, ssp, w, mp, col0, n, bm, bn):
    k = hp.shape[1]
    ms = hs.shape[0]
    grid = _ws_grid(n, bn, mp, bm)
    return pl.pallas_call(
        _proj_kernel,
        grid=grid,
        in_specs=[pl.BlockSpec((bm, k), _ws_rows),
                  pl.BlockSpec((ms, k), lambda j, i: (0, 0)),
                  pl.BlockSpec((bm, LANES), _ws_rows),
                  pl.BlockSpec((k // grid[1], bn), _ws_wchunk(grid[0] - 1, col0 // bn))],
        out_specs=[pl.BlockSpec((bm, bn), _ws_tile()),
                   pl.BlockSpec((ms, bn), _ws_stile())],
        out_shape=[jax.ShapeDtypeStruct((mp, n), BF16),
                   jax.ShapeDtypeStruct((ms, n), BF16)],
        scratch_shapes=[pltpu.VMEM((k, bn), BF16), pltpu.VMEM((k, bn), BF16)],
        compiler_params=_cparams("arbitrary", "arbitrary"),
        name="in_proj",
    )(hp, hs, ssp, w)


def _mixa_kernel(xp_ref, hs_ref, g_ref, wbf_ref, wcf_ref, whf_ref, cw_ref, s0_ref, s1_ref,
                 ap_ref, as_ref, us_ref, st_ref, hp_ref, ssp_ref,
                 wb0_ref, wc0_ref, wh0_ref, wb1_ref, wc1_ref, wh1_ref, carry_ref, ubuf_ref,
                 *, tiles_per_seq):
    i = pl.program_id(1)
    bm = xp_ref.shape[0]

    def compute(slot):
        wb_ref, wc_ref, wh_ref = slot
        w0 = cw_ref[0:1, :]
        w1 = cw_ref[1:2, :]
        w2 = cw_ref[2:3, :]

        @pl.when((i & (tiles_per_seq - 1)) == 0)
        def _():
            carry_ref[...] = jnp.zeros(carry_ref.shape, F32)

        hp_ref[...] = (xp_ref[...] * g_ref[...]).astype(hp_ref.dtype)
        ssp_ref[...] = _lane_partials(xp_ref[...] * xp_ref[...])
        r = _inv_rms(ssp_ref)
        u = (r * _dot(hp_ref[...], wc_ref[...])) * (r * _dot(hp_ref[...], wh_ref[...]))
        cb = r * _dot(hp_ref[...], wb_ref[...])
        ubuf_ref[0:SUBLANES, :] = carry_ref[...]
        ubuf_ref[SUBLANES:SUBLANES + bm, :] = u
        u1 = ubuf_ref[SUBLANES - 1:SUBLANES - 1 + bm, :]
        u2 = ubuf_ref[SUBLANES - 2:SUBLANES - 2 + bm, :]
        z = w0 * u2 + w1 * u1 + w2 * u
        ap_ref[...] = (cb * z).astype(ap_ref.dtype)
        tail = ubuf_ref[bm:SUBLANES + bm, :]
        carry_ref[...] = tail
        st_ref[0] = tail

        @pl.when(_is_last_row_tile())
        def _():
            us = _dot(hs_ref[...], wc_ref[...]) * _dot(hs_ref[...], wh_ref[...])
            zs = w0 * s0_ref[...] + w1 * s1_ref[...] + w2 * us
            as_ref[...] = (_dot(hs_ref[...], wb_ref[...]) * zs).astype(as_ref.dtype)
            us_ref[...] = us

    _ws_phases([wbf_ref, wcf_ref, whf_ref], [wb0_ref, wc0_ref, wh0_ref],
               [wb1_ref, wc1_ref, wh1_ref], compute)


def _mixer_a(xp, hs, gain, w_in, conv_w, s0, s1, bm, bn):
    mp, k = xp.shape
    ms = hs.shape[0]
    n = CONV_W
    grid = _ws_grid(n, bn, mp, bm)
    nj = grid[0] - 1
    tiles_per_seq = SEQ // bm
    wslot = pltpu.VMEM((k, bn), BF16)
    col = _ws_stile()

    def wchunk(col0):
        return pl.BlockSpec((k // grid[1], bn), _ws_wchunk(nj, col0 // bn))

    def tail_map(j, i):
        seq = lax.shift_right_logical(jnp.where(j > 0, i, 0), tiles_per_seq.bit_length() - 1)
        return (seq, 0, jnp.maximum(j - 1, 0))

    spare = grid[1]

    def operand_map(j, i):
        return (jnp.where(j == 1, i, jnp.where(j == 0, 0, spare)), 0)

    return pl.pallas_call(
        functools.partial(_mixa_kernel, tiles_per_seq=tiles_per_seq),
        grid=grid,
        in_specs=[pl.BlockSpec((bm, k), _ws_rows),
                  pl.BlockSpec((ms, k), lambda j, i: (0, 0)),
                  pl.BlockSpec((1, k), lambda j, i: (0, 0)),
                  wchunk(W_CB), wchunk(W_CC), wchunk(W_CH),
                  pl.BlockSpec((CONV_K, bn), col),
                  pl.BlockSpec((ms, bn), col),
                  pl.BlockSpec((ms, bn), col)],
        out_specs=[pl.BlockSpec((bm, bn), _ws_tile()),
                   pl.BlockSpec((ms, bn), col),
                   pl.BlockSpec((ms, bn), col),
                   pl.BlockSpec((1, SUBLANES, bn), tail_map),
                   pl.BlockSpec((bm, k), operand_map),
                   pl.BlockSpec((bm, LANES), operand_map)],
        out_shape=[jax.ShapeDtypeStruct((mp, n), BF16),
                   jax.ShapeDtypeStruct((ms, n), BF16),
                   jax.ShapeDtypeStruct((ms, n), F32),
                   jax.ShapeDtypeStruct((BATCH, SUBLANES, n), F32),
                   jax.ShapeDtypeStruct((mp + bm, k), BF16),
                   jax.ShapeDtypeStruct((mp + bm, LANES), F32)],
        scratch_shapes=[wslot] * 6 + [pltpu.VMEM((SUBLANES, bn), F32),
                                      pltpu.VMEM((SUBLANES + bm, bn), F32)],
        compiler_params=_cparams("arbitrary", "arbitrary"),
        name="mixer_a",
    )(xp, hs, gain.reshape(1, k), w_in, w_in, w_in, conv_w, s0, s1)


def _bucket_vector():
    max_exact = N_BUCKETS // 2
    d = np.arange(2 * Q_BLOCK)
    df = np.maximum(d, 1).astype(np.float32)
    large = max_exact + (np.log(df / np.float32(max_exact))
                         / np.float32(math.log(MAX_DISTANCE / max_exact))
                         * np.float32(N_BUCKETS - max_exact)).astype(np.int32)
    large = np.minimum(large, N_BUCKETS - 1)
    bucket = np.where(d < max_exact, d, large).astype(np.int32)
    valid = (d <= WINDOW).astype(np.int32)
    tile = lambda v: np.ascontiguousarray(np.broadcast_to(v[None, :], (SUBLANES, v.size)))
    return tile(bucket), tile(valid)


def _bias_kernel(rb_ref, bucket_ref, valid_ref, o_ref):
    bucket = bucket_ref[...]
    keep = valid_ref[...] != 0
    shp = (2 * Q_BLOCK, 2 * Q_BLOCK)
    first = lax.broadcasted_iota(jnp.int32, (2 * Q_BLOCK, Q_BLOCK), 0) >= Q_BLOCK

    def head(h, carry):
        acc = jnp.zeros(bucket.shape, F32)
        for b in range(N_BUCKETS):
            acc = jnp.where(bucket == b, rb_ref[b, h], acc)
        tvec = jnp.where(keep, acc, NEG)
        full = jnp.broadcast_to(tvec[0:1, :], shp)
        band = pltpu.roll(full, Q_BLOCK, 1, stride=1, stride_axis=0)[:, :Q_BLOCK]
        o_ref[0, h] = band
        o_ref[1, h] = jnp.where(first, band, NEG)
        return carry

    lax.fori_loop(0, N_HEADS, head, 0)


def _bias_table(rel_bias):
    bucket, valid = _bucket_vector()
    shp = (2 * Q_BLOCK, Q_BLOCK)
    vec = (SUBLANES, 2 * Q_BLOCK)
    return pl.pallas_call(
        _bias_kernel,
        grid=(1,),
        in_specs=[pl.BlockSpec(memory_space=pltpu.SMEM),
                  pl.BlockSpec(vec, lambda v: (0, 0)),
                  pl.BlockSpec(vec, lambda v: (0, 0))],
        out_specs=pl.BlockSpec((2, N_HEADS) + shp, lambda v: (0, 0, 0, 0)),
        out_shape=jax.ShapeDtypeStruct((2, N_HEADS) + shp, F32),
        compiler_params=_cparams("arbitrary"),
        name="rel_bias_table",
    )(rel_bias, jnp.asarray(bucket), jnp.asarray(valid))


BLOCKS_PER_STEP = 2


def _attn_prompt_kernel(sink_ref, q_ref, kvp_ref, kvc_ref, bias_ref, o_ref):
    scale = HEAD_DIM ** -0.5
    first_variant = jnp.where(pl.program_id(1) == 0, 1, 0)

    def band(blk, col):
        cols = slice(col, col + HEAD_DIM)
        cur = kvc_ref[blk * Q_BLOCK:(blk + 1) * Q_BLOCK, cols]
        if blk == 0:
            return jnp.concatenate([kvp_ref[:, cols], cur], axis=0)
        return kvc_ref[(blk - 1) * Q_BLOCK:(blk + 1) * Q_BLOCK, cols]

    def scores(blk, kv):
        heads = range(kv * GROUP, (kv + 1) * GROUP)
        rows = slice(blk * Q_BLOCK, (blk + 1) * Q_BLOCK)
        variant = first_variant if blk == 0 else 0
        qg = jnp.concatenate(
            [q_ref[rows, h * HEAD_DIM:(h + 1) * HEAD_DIM] for h in heads], axis=0) * scale
        st = lax.dot_general(band(blk, kv * HEAD_DIM), qg, (((1,), (1,)), ((), ())),
                             preferred_element_type=F32)
        return st + jnp.concatenate([bias_ref[variant, h] for h in heads], axis=1)

    def finish(blk, kv, st):
        heads = range(kv * GROUP, (kv + 1) * GROUP)
        vband = band(blk, KV_W + kv * HEAD_DIM)
        sk = jnp.concatenate([jnp.full((1, Q_BLOCK), sink_ref[h], F32) for h in heads], axis=1)
        m = jnp.maximum(jnp.max(st, axis=0, keepdims=True), sk)
        e = jnp.exp(st - m)
        denom = jnp.sum(e, axis=0, keepdims=True) + jnp.exp(sk - m)
        ot = lax.dot_general(vband, e.astype(BF16), (((0,), (0,)), ((), ())),
                             preferred_element_type=F32) * (1.0 / denom)
        o2 = jnp.concatenate(
            [ot[:, g * Q_BLOCK:(g + 1) * Q_BLOCK] for g in range(GROUP)], axis=0)
        rows = slice(blk * Q_BLOCK, (blk + 1) * Q_BLOCK)
        gs = slice(kv * GROUP * HEAD_DIM, (kv + 1) * GROUP * HEAD_DIM)
        o_ref[rows, gs] = o2.T.astype(o_ref.dtype)

    groups =[(blk, kv) for blk in range(BLOCKS_PER_STEP) for kv in range(N_KV_HEADS)]
    pairs = [groups[p:p + 2] for p in range(0, len(groups), 2)]
    sts = [scores(*grp) for grp in pairs[0]]
    for idx, pair in enumerate(pairs):
        sts_next = [scores(*grp) for grp in pairs[idx + 1]] if idx + 1 < len(pairs) else None
        for grp, st in zip(pair, sts):
            finish(*grp, st)
        sts = sts_next


def _attn_prompt(proj_p, bias_tab, sinks):
    rows = BLOCKS_PER_STEP * Q_BLOCK
    steps = SEQ // rows
    qcol = OFF_Q // ATTN_W
    kvcol = OFF_K // (2 * KV_W)

    def prev(b, n):
        return b * (SEQ // Q_BLOCK) + jnp.maximum(n * BLOCKS_PER_STEP - 1, 0)

    return pl.pallas_call(
        _attn_prompt_kernel,
        grid=(BATCH, steps),
        in_specs=[pl.BlockSpec(memory_space=pltpu.SMEM),
                  pl.BlockSpec((rows, ATTN_W), lambda b, n: (b * steps + n, qcol)),
                  pl.BlockSpec((Q_BLOCK, 2 * KV_W), lambda b, n: (prev(b, n), kvcol)),
                  pl.BlockSpec((rows, 2 * KV_W), lambda b, n: (b * steps + n, kvcol)),
                  pl.BlockSpec((2, N_HEADS, 2 * Q_BLOCK, Q_BLOCK), lambda b, n: (0, 0, 0, 0))],
        out_specs=pl.BlockSpec((rows, ATTN_W), lambda b, n: (b * steps + n, 0)),
        out_shape=jax.ShapeDtypeStruct((M_PROMPT, ATTN_W), BF16),
        compiler_params=_cparams("parallel", "arbitrary"),
        name="attn_prompt",
    )(sinks, proj_p, proj_p, proj_p, bias_tab)


SAMPLES_PER_STEP = 8


def _attn_sample_kernel(q_ref, kn_ref, vn_ref, kt_ref, vt_ref, bias_ref, sink_ref,
                        o_ref, kto_ref, vto_ref):
    g = SAMPLES_PER_STEP
    scale = HEAD_DIM ** -0.5
    erow = lax.broadcasted_iota(jnp.int32, (HEAD_DIM, KV_W), 0)
    ecol = lax.broadcasted_iota(jnp.int32, (HEAD_DIM, KV_W), 1)
    expand = (ecol % HEAD_DIM == erow).astype(BF16)
    hrow = lax.broadcasted_iota(jnp.int32, (N_HEADS, KV_W), 0)
    hcol = lax.broadcasted_iota(jnp.int32, (N_HEADS, KV_W), 1)
    own = ((hrow // GROUP) == (hcol // HEAD_DIM))[None]
    bias_w = bias_ref[:, 0:WINDOW][None]
    bias_n = bias_ref[:, WINDOW:WINDOW + 1][None]
    sk = sink_ref[...][None]
    contract_last = (((1,), (1,)), ((), ()))

    q_all = q_ref[...].reshape(g * N_HEADS, HEAD_DIM) * scale
    qrow = jnp.where(own, _dot(q_all, expand).reshape(g, N_HEADS, KV_W), 0.0)
    qrow_b = qrow.astype(BF16)
    kn = kn_ref[...][:, None, :]
    vn = vn_ref[...][:, None, :]
    s_w = jnp.stack([_dot(qrow_b[b], kt_ref[b].astype(BF16)) for b in range(g)])
    s_w = s_w + bias_w
    s_n = jnp.sum(qrow * kn, axis=-1, keepdims=True) + bias_n
    m = jnp.maximum(jnp.maximum(jnp.max(s_w, axis=-1, keepdims=True), s_n), sk)
    e_w = jnp.exp(s_w - m)
    e_n = jnp.exp(s_n - m)
    r = 1.0 / (jnp.sum(e_w, axis=-1, keepdims=True) + e_n + jnp.exp(sk - m))
    p_w = (e_w * r).astype(BF16)
    o_all = jnp.stack([lax.dot_general(p_w[b], vt_ref[b].astype(BF16), contract_last,
                                       preferred_element_type=F32) for b in range(g)])
    o_own = jnp.where(own, o_all + (e_n * r) * vn, 0.0)
    o = o_own[:, :, 0:HEAD_DIM]
    for kv in range(1, N_KV_HEADS):
        o = o + o_own[:, :, kv * HEAD_DIM:(kv + 1) * HEAD_DIM]
    o_ref[...] = o.astype(o_ref.dtype)

    rows = jnp.concatenate([kn_ref[...], vn_ref[...]], axis=0).astype(BF16)
    place = (lax.broadcasted_iota(jnp.int32, (2 * g, 2 * g * WINDOW), 1)
             == WINDOW * lax.broadcasted_iota(jnp.int32, (2 * g, 2 * g * WINDOW), 0)
             ).astype(BF16)
    newcols = lax.dot_general(rows, place, (((0,), (0,)), ((), ())),
                              preferred_element_type=F32)
    first = lax.broadcasted_iota(jnp.int32, (KV_W, WINDOW), 1) == 0
    for b in range(g):
        kcol = newcols[:, b * WINDOW:(b + 1) * WINDOW]
        vcol = newcols[:, (g + b) * WINDOW:(g + b + 1) * WINDOW]
        kto_ref[b] = pltpu.roll(jnp.where(first, kcol, kt_ref[b]), WINDOW - 1, 1)
        vto_ref[b] = pltpu.roll(jnp.where(first, vcol, vt_ref[b]), WINDOW - 1, 1)


def _attn_sample(q3, k_new, v_new, k_buf, v_buf, bias_s, sinks):
    g = SAMPLES_PER_STEP
    nb = DEC_BATCH // g
    win = pl.BlockSpec((g, KV_W, WINDOW), lambda i: (i, 0, 0))
    row = pl.BlockSpec((g, KV_W), lambda i: (i, 0))
    return pl.pallas_call(
        _attn_sample_kernel,
        grid=(nb,),
        in_specs=[pl.BlockSpec((g, N_HEADS, HEAD_DIM), lambda i: (i, 0, 0)),
                  row, row, win, win,
                  pl.BlockSpec((N_HEADS, WINDOW + 1), lambda i: (0, 0)),
                  pl.BlockSpec((N_HEADS, 1), lambda i: (0, 0))],
        out_specs=[pl.BlockSpec((g, N_HEADS, HEAD_DIM), lambda i: (i, 0, 0)), win, win],
        out_shape=[jax.ShapeDtypeStruct((DEC_BATCH, N_HEADS, HEAD_DIM), BF16),
                   jax.ShapeDtypeStruct((DEC_BATCH, KV_W, WINDOW), F32),
                   jax.ShapeDtypeStruct((DEC_BATCH, KV_W, WINDOW), F32)],
        compiler_params=_cparams("parallel"),
        name="attn_sample",
    )(q3, k_new, v_new, k_buf, v_buf, bias_s, sinks.reshape(N_HEADS, 1))


def _merge_kernel(ap_ref, bp_ref, as_ref, bs_ref, waf_ref, wbf_ref,
                  gap_ref, gbp_ref, gas_ref, gbs_ref, op_ref, os_ref,
                  wa0_ref, wb0_ref, wa1_ref, wb1_ref):
    def compute(slot):
        wa_ref, wb_ref = slot

        def merged(a_ref, b_ref, ga_ref, gb_ref):
            return (_sigmoid(ga_ref[...].astype(F32)) * _dot(a_ref[...], wa_ref[...])
                    + _sigmoid(gb_ref[...].astype(F32)) * _dot(b_ref[...], wb_ref[...]))

        op_ref[...] = merged(ap_ref, bp_ref, gap_ref, gbp_ref).astype(op_ref.dtype)

        @pl.when(_is_last_row_tile())
        def _():
            os_ref[...] = merged(as_ref, bs_ref, gas_ref, gbs_ref).astype(os_ref.dtype)

    _ws_phases([waf_ref, wbf_ref], [wa0_ref, wb0_ref], [wa1_ref, wb1_ref], compute)


def _branch_merge(a_p, o_p, a_s, o_s, wa, wb, proj_p, proj_s, bm, bn):
    mp, k = a_p.shape
    ms = a_s.shape[0]
    n = wa.shape[1]
    grid = _ws_grid(n, bn, mp, bm)
    ga0 = OFF_GA // bn
    gb0 = OFF_GB // bn
    wchunk = pl.BlockSpec((k // grid[1], bn), _ws_wchunk(grid[0] - 1))
    wslot = pltpu.VMEM((k, bn), BF16)
    return pl.pallas_call(
        _merge_kernel,
        grid=grid,
        in_specs=[pl.BlockSpec((bm, k), _ws_rows),
                  pl.BlockSpec((bm, k), _ws_rows),
                  pl.BlockSpec((ms, k), lambda j, i: (0, 0)),
                  pl.BlockSpec((ms, k), lambda j, i: (0, 0)),
                  wchunk, wchunk,
                  pl.BlockSpec((bm, bn), _ws_tile(ga0)),
                  pl.BlockSpec((bm, bn), _ws_tile(gb0)),
                  pl.BlockSpec((ms, bn), _ws_stile(ga0)),
                  pl.BlockSpec((ms, bn), _ws_stile(gb0))],
        out_specs=[pl.BlockSpec((bm, bn), _ws_tile()),
                   pl.BlockSpec((ms, bn), _ws_stile())],
        out_shape=[jax.ShapeDtypeStruct((mp, n), BF16),
                   jax.ShapeDtypeStruct((ms, n), BF16)],
        scratch_shapes=[wslot, wslot, wslot, wslot],
        compiler_params=_cparams("arbitrary", "arbitrary"),
        name="branch_merge",
    )(a_p, o_p, a_s, o_s, wa, wb, proj_p, proj_p, proj_s, proj_s)


def _resid_kernel(xp_ref, xs_ref, wf_ref, rp_ref, rs_ref, op_ref, os_ref, w0_ref, w1_ref):
    def compute(slot):
        (w_ref,) = slot
        op_ref[...] = rp_ref[...] + _dot(xp_ref[...], w_ref[...])

        @pl.when(_is_last_row_tile())
        def _():
            os_ref[...] = rs_ref[...] + _dot(xs_ref[...], w_ref[...])

    _ws_phases([wf_ref], [w0_ref], [w1_ref], compute)


def _resid_matmul(xp, xs, w, rp, rs, bm, bn, name):
    mp, k = xp.shape
    ms = xs.shape[0]
    n = w.shape[1]
    grid = _ws_grid(n, bn, mp, bm)
    return pl.pallas_call(
        _resid_kernel,
        grid=grid,
        in_specs=[pl.BlockSpec((bm, k), _ws_rows),
                  pl.BlockSpec((ms, k), lambda j, i: (0, 0)),
                  pl.BlockSpec((k // grid[1], bn), _ws_wchunk(grid[0] - 1)),
                  pl.BlockSpec((bm, bn), _ws_tile()),
                  pl.BlockSpec((ms, bn), _ws_stile())],
        out_specs=[pl.BlockSpec((bm, bn), _ws_tile()),
                   pl.BlockSpec((ms, bn), _ws_stile())],
        out_shape=[jax.ShapeDtypeStruct((mp, n), F32),
                   jax.ShapeDtypeStruct((ms, n), F32)],
        scratch_shapes=[pltpu.VMEM((k, bn), BF16), pltpu.VMEM((k, bn), BF16)],
        compiler_params=_cparams("arbitrary", "arbitrary"),
        name=name,
    )(xp, xs, w, rp, rs)


def _lane_partials(v):
    part = v[:, 0:LANES]
    for c in range(1, v.shape[1] // LANES):
        part = part + v[:, c * LANES:(c + 1) * LANES]
    return part


def _outproj_kernel(xp_ref, xs_ref, wf_ref, rp_ref, rs_ref, g_ref,
                    op_ref, os_ref, bp_ref, bs_ref, ssp_ref, sss_ref,
                    w0_ref, w1_ref, accp_ref, accs_ref):
    j = pl.program_id(0)
    i = pl.program_id(1)
    bm = xp_ref.shape[0]

    def compute(slot):
        (w_ref,) = slot

        def tile(x_ref, r_ref, o_ref, b_ref, acc_view, ss_ref):
            x1 = r_ref[...] + _dot(x_ref[...], w_ref[...])
            o_ref[...] = x1
            b_ref[...] = (x1 * g_ref[...]).astype(b_ref.dtype)
            total = jnp.where(j == 1, 0.0, acc_view[...]) + _lane_partials(x1 * x1)
            acc_view[...] = total
            ss_ref[...] = total

        rows = pl.ds(pl.multiple_of(i * bm, bm), bm)
        tile(xp_ref, rp_ref, op_ref, bp_ref, accp_ref.at[rows, :], ssp_ref)

        @pl.when(_is_last_row_tile())
        def _():
            tile(xs_ref, rs_ref, os_ref, bs_ref, accs_ref, sss_ref)

    _ws_phases([wf_ref], [w0_ref], [w1_ref], compute)


def _out_proj(xp, xs, w, rp, rs, gain, bm, bn):
    mp, k = xp.shape
    ms = xs.shape[0]
    n = w.shape[1]
    grid = _ws_grid(n, bn, mp, bm)
    last = grid[0] - 1
    ssp_map = lambda j, i: (jnp.where(j == last, i, 0), 0)
    return pl.pallas_call(
        _outproj_kernel,
        grid=grid,
        in_specs=[pl.BlockSpec((bm, k), _ws_rows),
                  pl.BlockSpec((ms, k), lambda j, i: (0, 0)),
                  pl.BlockSpec((k // grid[1], bn), _ws_wchunk(grid[0] - 1)),
                  pl.BlockSpec((bm, bn), _ws_tile()),
                  pl.BlockSpec((ms, bn), _ws_stile()),
                  pl.BlockSpec((1, bn), _ws_stile())],
        out_specs=[pl.BlockSpec((bm, bn), _ws_tile()),
                   pl.BlockSpec((ms, bn), _ws_stile()),
                   pl.BlockSpec((bm, bn), _ws_tile()),
                   pl.BlockSpec((ms, bn), _ws_stile()),
                   pl.BlockSpec((bm, LANES), ssp_map),
                   pl.BlockSpec((ms, LANES), lambda j, i: (0, 0))],
        out_shape=[jax.ShapeDtypeStruct((mp, n), F32),
                   jax.ShapeDtypeStruct((ms, n), F32),
                   jax.ShapeDtypeStruct((mp, n), BF16),
                   jax.ShapeDtypeStruct((ms, n), BF16),
                   jax.ShapeDtypeStruct((mp, LANES), F32),
                   jax.ShapeDtypeStruct((ms, LANES), F32)],
        scratch_shapes=[pltpu.VMEM((k, bn), BF16), pltpu.VMEM((k, bn), BF16),
                        pltpu.VMEM((mp, LANES), F32), pltpu.VMEM((ms, LANES), F32)],
        compiler_params=_cparams("arbitrary", "arbitrary"),
        name="out_proj",
    )(xp, xs, w, rp, rs, gain.reshape(1, n))


def _inv_rms(ss_ref):
    return lax.rsqrt(jnp.sum(ss_ref[...], axis=-1, keepdims=True) * (1.0 / D_MODEL) + EPS)


def _ffn_up_kernel(hp_ref, hs_ref, ssp_ref, sss_ref, wgf_ref, wuf_ref, cw_ref, cb_ref,
                   sst_ref, fp_ref, fs_ref, sso_ref, st_ref,
                   wg0_ref, wu0_ref, wg1_ref, wu1_ref, carry_ref, gbuf_ref,
                   *, tiles_per_seq, last_phase, last_width):
    i = pl.program_id(1)
    bm = hp_ref.shape[0]

    def compute(slot, width):
        wg_ref, wu_ref = slot
        cols = slice(0, width)
        w0 = cw_ref[0:1, cols]
        w1 = cw_ref[1:2, cols]
        w2 = cw_ref[2:3, cols]
        bias = cb_ref[:, cols]

        @pl.when((i & (tiles_per_seq - 1)) == 0)
        def _():
            carry_ref[...] = jnp.zeros(carry_ref.shape, F32)

        r = _inv_rms(ssp_ref)
        g = r * _dot(hp_ref[...], wg_ref[:, cols])
        gbuf_ref[0:SUBLANES, cols] = carry_ref[:, cols]
        gbuf_ref[SUBLANES:SUBLANES + bm, cols] = g
        g1 = gbuf_ref[SUBLANES - 1:SUBLANES - 1 + bm, cols]
        g2 = gbuf_ref[SUBLANES - 2:SUBLANES - 2 + bm, cols]
        gc = w0 * g2 + w1 * g1 + w2 * g + bias
        up = r * _dot(hp_ref[...], wu_ref[:, cols])
        fp_ref[:, cols] = (gc * _sigmoid(gc) * up).astype(fp_ref.dtype)
        tail = gbuf_ref[bm:SUBLANES + bm, cols]
        carry_ref[:, cols] = tail
        st_ref[0, :, cols] = tail

        @pl.when(_is_last_row_tile())
        def _():
            rs = _inv_rms(sss_ref)
            gsm = rs * _dot(hs_ref[...], wg_ref[:, cols])
            ups = rs * _dot(hs_ref[...], wu_ref[:, cols])
            groups = range(width // LANES)
            s0 = jnp.concatenate([sst_ref[:, c, 0, :] for c in groups], axis=-1)
            s1 = jnp.concatenate([sst_ref[:, c, 1, :] for c in groups], axis=-1)
            gcs = w0 * s0 + w1 * s1 + w2 * gsm + bias
            fs_ref[:, cols] = (gcs * _sigmoid(gcs) * ups).astype(fs_ref.dtype)
            for c in groups:
                sso_ref[:, c, 0, :] = sst_ref[:, c, 1, :]
                sso_ref[:, c, 1, :] = gsm[:, c * LANES:(c + 1) * LANES]

    full_width = fp_ref.shape[1]
    _ws_phases([wgf_ref, wuf_ref], [wg0_ref, wu0_ref], [wg1_ref, wu1_ref],
               functools.partial(compute, width=full_width),
               functools.partial(compute, width=last_width) if last_width < full_width else None,
               last_phase)


def _ffn_up(hp, hs, ssp, sss, wg, wu, cw, cb, sst, bm, bn):
    mp, k = hp.shape
    ms = hs.shape[0]
    n = wg.shape[1]
    grid = _ws_grid(n, bn, mp, bm)
    tiles_per_seq = SEQ // bm
    wchunk = pl.BlockSpec((k // grid[1], bn), _ws_wchunk(grid[0] - 1))
    wslot = pltpu.VMEM((k, bn), BF16)
    col = _ws_stile()
    state = pl.BlockSpec((ms, bn // LANES, FFN_CONV_K - 1, LANES),
                         lambda j, i: (0, jnp.maximum(j - 1, 0), 0, 0))

    def tail_map(j, i):
        seq = lax.shift_right_logical(jnp.where(j > 0, i, 0), tiles_per_seq.bit_length() - 1)
        return (seq, 0, jnp.maximum(j - 1, 0))

    last_phase = grid[0] - 1
    last_width = n - (last_phase - 1) * bn
    return pl.pallas_call(
        functools.partial(_ffn_up_kernel, tiles_per_seq=tiles_per_seq,
                          last_phase=last_phase, last_width=last_width),
        grid=grid,
        in_specs=[pl.BlockSpec((bm, k), _ws_rows),
                  pl.BlockSpec((ms, k), lambda j, i: (0, 0)),
                  pl.BlockSpec((bm, LANES), _ws_rows),
                  pl.BlockSpec((ms, LANES), lambda j, i: (0, 0)),
                  wchunk, wchunk,
                  pl.BlockSpec((FFN_CONV_K, bn), col),
                  pl.BlockSpec((1, bn), col),
                  state],
        out_specs=[pl.BlockSpec((bm, bn), _ws_tile()),
                   pl.BlockSpec((ms, bn), col),
                   state,
                   pl.BlockSpec((1, SUBLANES, bn), tail_map)],
        out_shape=[jax.ShapeDtypeStruct((mp, n), BF16),
                   jax.ShapeDtypeStruct((ms, n), BF16),
                   jax.ShapeDtypeStruct(sst.shape, F32),
                   jax.ShapeDtypeStruct((BATCH, SUBLANES, n), F32)],
        scratch_shapes=[wslot, wslot, wslot, wslot,
                        pltpu.VMEM((SUBLANES, bn), F32),
                        pltpu.VMEM((SUBLANES + bm, bn), F32)],
        compiler_params=_cparams("arbitrary", "arbitrary"),
        name="ffn_up",
    )(hp, hs, ssp, sss, wg, wu, cw, cb, sst)


def _window_to_kernel(state):
    return jnp.transpose(state[0], (0, 2, 3, 1)).reshape(DEC_BATCH, KV_W, WINDOW)


def _window_from_kernel(win):
    win = win.reshape(DEC_BATCH, N_KV_HEADS, HEAD_DIM, WINDOW)
    return jnp.transpose(win, (0, 3, 1, 2))[None]


def _conv_state_to_kernel(state):
    _, ms, rows, width = state.shape
    return jnp.transpose(state[0].reshape(ms, rows, width // LANES, LANES), (0, 2, 1, 3))


def _conv_state_from_kernel(state):
    ms, groups, rows, _ = state.shape
    return jnp.transpose(state, (0, 2, 1, 3)).reshape(1, ms, rows, groups * LANES)


def kernel(x_prompt, x_sample, state_k_window, state_v_window, state_conv, state_ffn_conv,
           attn_norm_g, w_in, conv_w, w_branch_a, w_branch_b, sinks, w_out, ffn_norm_g,
           w_ffn_gate, w_ffn_up, ffn_conv_w, ffn_conv_b, w_ffn_down, rel_bias,
           final_norm_g):
    xp = x_prompt.reshape(M_PROMPT, D_MODEL)
    xs = x_sample.reshape(DEC_BATCH, D_MODEL)

    cs0 = state_conv[0][:, 0, :]
    cs1 = state_conv[0][:, 1, :]
    hs = _rmsnorm(xs, attn_norm_g[0], BF16, DEC_BATCH)
    a_p, a_s, u_s, u_tail, hp, hssp = _mixer_a(
        xp, hs, attn_norm_g[0], w_in[0], conv_w[0], cs0, cs1, *MIXER_TILE)
    proj_p, proj_s = _in_proj(hp, hs, hssp, w_in[0], M_PROMPT, W_REST, PROJ_W, *PROJ_TILE)

    bias_tab = _bias_table(rel_bias)
    o_p = _attn_prompt(proj_p, bias_tab, sinks[0])
    q3 = proj_s[:, OFF_Q:OFF_Q + ATTN_W].reshape(DEC_BATCH, N_HEADS, HEAD_DIM)
    k_new = proj_s[:, OFF_K:OFF_K + KV_W].astype(F32)
    v_new = proj_s[:, OFF_V:OFF_V + KV_W].astype(F32)
    bias_s = bias_tab[0, :, Q_BLOCK - 1:, Q_BLOCK - 1]
    o_s3, k_win_s, v_win_s = _attn_sample(
        q3, k_new, v_new,
        _window_to_kernel(state_k_window), _window_to_kernel(state_v_window),
        bias_s, sinks[0])
    o_s = o_s3.reshape(DEC_BATCH, ATTN_W)

    mg_p, mg_s = _branch_merge(a_p, o_p, a_s, o_s, w_branch_a[0], w_branch_b[0],
                               proj_p, proj_s, *MERGE_TILE)
    x1p, x1s, xbp, xbs, ssp, sss = _out_proj(mg_p, mg_s, w_out[0], xp, xs, ffn_norm_g[0],
                                             *OUT_TILE)

    f_p, f_s, ffn_state_new, g_tail = _ffn_up(xbp, xbs, ssp, sss, w_ffn_gate[0], w_ffn_up[0],
                                              ffn_conv_w[0], ffn_conv_b,
                                              _conv_state_to_kernel(state_ffn_conv),
                                              *FFN_UP_TILE)
    x2p, x2s = _resid_matmul(f_p, f_s, w_ffn_down[0], x1p, x1s, *FFN_DOWN_TILE, "ffn_down")
    y_p = _rmsnorm(x2p, final_norm_g, F32, NORM_ROWS)
    y_s = _rmsnorm(x2s, final_norm_g, F32, DEC_BATCH)

    kv_p = proj_p.reshape(BATCH, SEQ, PROJ_W)[:, SEQ - WINDOW:, OFF_K:OFF_GA].astype(F32)
    k_win_p = kv_p[:, :, :KV_W].reshape(1, BATCH, WINDOW, N_KV_HEADS, HEAD_DIM)
    v_win_p = kv_p[:, :, KV_W:].reshape(1, BATCH, WINDOW, N_KV_HEADS, HEAD_DIM)
    conv_p = u_tail[:, SUBLANES - (CONV_K - 1):, :]
    ffn_p = g_tail[:, SUBLANES - (FFN_CONV_K - 1):, :]
    conv_s = jnp.stack([cs1, u_s], axis=1)

    return (y_p.reshape(BATCH, SEQ, D_MODEL),
            y_s.reshape(DEC_BATCH, 1, D_MODEL),
            k_win_p, v_win_p, conv_p[None], ffn_p[None],
            _window_from_kernel(k_win_s), _window_from_kernel(v_win_s),
            conv_s[None], _conv_state_from_kernel(ffn_state_new))
```
